```python
import math
import jax
import jax.numpy as jnp
from jax import lax
import numpy as np

D_MODEL = 2048
BATCH = 1
SEQ = 8192
DEPTH = 2

GRID_W = 64
CTX_LEN = 256

S5_WIDTH = 512
S5_GROUP = 16
S5_GROUPS = S5_WIDTH // S5_GROUP
S5_STATE = 64
N_DIR = 2
FFT_WIDTH = 512
FFT_GROUPS = 4
FFT_GROUP = FFT_WIDTH // FFT_GROUPS
SGU_WIDTH = 512
SGU_HEADS = 4
SGU_HEAD = SGU_WIDTH // SGU_HEADS
CHUNK = 128
CHUNK_ROWS = CHUNK // GRID_W
N_BRANCH = 3
BRANCH_WIDTH = 512
IN_WIDTH = S5_WIDTH + FFT_WIDTH + 2 * SGU_WIDTH
N_EXPERTS = 32
TOP_K = 4
D_FF = D_MODEL
SWIGLU_LIMIT = 7.0
SWIGLU_ALPHA = 1.702
MOE_BLOCK = 128
N_MOD = 6
EPS = 1e-6

kernel_name = 'hybrid_s5_fnet_sgu_moe_dit'


def rmsnorm(x, g):
    xf = x.astype(jnp.float32)
    y = xf * lax.rsqrt(jnp.mean(xf * xf, axis=-1, keepdims=True) + EPS)
    return (y * g.astype(jnp.float32)).astype(x.dtype)


def adaln(cond, w_mod, b_mod):
    m = jax.nn.silu(cond) @ w_mod + b_mod
    return jnp.split(m, N_MOD, axis=-1)


def modulate(h, shift, scale):
    return h * (1.0 + scale) + shift


def s5_discretize(a_re, a_im, log_dt, b_re, b_im):
    lam = lax.complex(a_re.astype(jnp.float32), a_im.astype(jnp.float32))
    dt = jnp.exp(log_dt.astype(jnp.float32))[..., None]
    a_bar = jnp.exp(lam * dt)
    b = lax.complex(b_re.astype(jnp.float32), b_im.astype(jnp.float32))
    b_bar = ((a_bar - 1.0) / lam)[..., None] * b
    return a_bar, b_bar


def _ssm_combine(left, right):
    a_l, b_l = left
    a_r, b_r = right
    return a_r * a_l, a_r * b_l + b_r


def s5_scan(u, a_bar, b_bar, h0):
    u_dir = jnp.stack([u, jnp.flip(u, axis=1)]).astype(jnp.float32).astype(jnp.complex64)
    bu = jnp.einsum('dblgh,dgph->dblgp', u_dir, b_bar)
    bu = bu.at[:, :, 0].add(a_bar[:, None] * h0)
    a = jnp.broadcast_to(a_bar[:, None, None], bu.shape)
    _, states = lax.associative_scan(_ssm_combine, (a, bu), axis=2)
    return states


def s5_readout(states, c_re, c_im, d_skip, u):
    c = lax.complex(c_re.astype(jnp.float32), c_im.astype(jnp.float32))
    y = jnp.real(jnp.einsum('dblgp,dghp->dblgh', states, c))
    y = y[0] + jnp.flip(y[1], axis=1)
    return y + d_skip.astype(jnp.float32).reshape(S5_GROUPS, S5_GROUP) * u.astype(jnp.float32)


def fourier_mix(z):
    bsz, n, _ = z.shape
    zg = z.astype(jnp.float32).reshape(bsz, n, FFT_GROUPS, FFT_GROUP).transpose(0, 2, 1, 3)
    f = jnp.real(jnp.fft.fft2(zg, norm='ortho'))
    return f.transpose(0, 2, 1, 3).reshape(bsz, n, FFT_WIDTH).astype(z.dtype)


def spatial_gating(z, g_v, w_s, b_s, n_chunks):
    u, v = jnp.split(z, 2, axis=-1)
    v = rmsnorm(v, g_v)
    bsz, n, _ = v.shape
    vc = v.reshape(bsz, n_chunks, CHUNK, SGU_HEADS, SGU_HEAD)
    s = jnp.einsum('hqk,bnkhc->bnqhc', w_s, vc) + b_s.T[None, None, :, :, None]
    return u * s.reshape(bsz, n, SGU_WIDTH)


def branch_features(z, states, c_re, c_im, d_skip, w_glu, g_v, w_s, b_s, n_chunks):
    bsz, n, _ = z.shape
    z_s5 = z[..., :S5_WIDTH]
    z_fft = z[..., S5_WIDTH:S5_WIDTH + FFT_WIDTH]
    z_sgu = z[..., S5_WIDTH + FFT_WIDTH:]
    y = s5_readout(states, c_re, c_im, d_skip, z_s5.reshape(bsz, n, S5_GROUPS, S5_GROUP))
    y = jax.nn.gelu(y.reshape(bsz, n, S5_WIDTH)).astype(z.dtype)
    y_s5 = y * jax.nn.sigmoid(y @ w_glu)
    y_fft = fourier_mix(z_fft)
    y_sgu = spatial_gating(jax.nn.gelu(z_sgu), g_v, w_s, b_s, n_chunks)
    return jnp.stack([y_s5, y_fft, y_sgu], axis=2)


def gated_merge(h, feats, w_branch, w_gate, b_gate, w_out):
    bsz, n, _ = h.shape
    br = jnp.einsum('blkf,kfd->blkd', feats, w_branch)
    gates = jax.nn.sigmoid(h @ w_gate + b_gate).reshape(bsz, n, N_BRANCH, D_MODEL)
    return jnp.einsum('blkd,blkd->bld', gates, br) @ w_out


def moe_ffn(h, router_w, router_b, w_up, b_up, w_down, b_down):
    n = h.shape[0]
    logits = (h @ router_w).astype(jnp.float32) + router_b.astype(jnp.float32)
    top_val, top_idx = lax.top_k(logits, TOP_K)
    top_w = jax.nn.softmax(top_val, axis=-1)
    n_assign = n * TOP_K
    flat_e = top_idx.reshape(-1)
    flat_tok = jnp.repeat(jnp.arange(n, dtype=jnp.int32), TOP_K)
    flat_w = top_w.reshape(-1)
    order = jnp.argsort(flat_e)
    e_sorted = flat_e[order]
    tok_sorted = flat_tok[order]
    w_sorted = flat_w[order]
    counts = jnp.bincount(flat_e, length=N_EXPERTS)
    padded = (counts + MOE_BLOCK - 1) // MOE_BLOCK * MOE_BLOCK
    start = jnp.cumsum(counts) - counts
    pend = jnp.cumsum(padded)
    pstart = pend - padded
    dest = pstart[e_sorted] + (jnp.arange(n_assign) - start[e_sorted])
    n_rows = (n_assign + MOE_BLOCK - 1) // MOE_BLOCK * MOE_BLOCK + N_EXPERTS * MOE_BLOCK
    n_blocks = n_rows // MOE_BLOCK
    row_tok = jnp.zeros((n_rows,), jnp.int32).at[dest].set(tok_sorted)
    row_w = jnp.zeros((n_rows,), jnp.float32).at[dest].set(w_sorted)
    block_e = jnp.searchsorted(pend, jnp.arange(n_blocks) * MOE_BLOCK, side='right')
    block_e = jnp.minimum(block_e, N_EXPERTS - 1)

    def expert_block(args):
        tok, w, e = args
        xb = h[tok]
        gu = xb @ w_up[e] + b_up[e]
        gate, up = jnp.split(gu, 2, axis=-1)
        gate = jnp.minimum(gate, SWIGLU_LIMIT)
        up = jnp.clip(up, -SWIGLU_LIMIT, SWIGLU_LIMIT)
        act = gate * jax.nn.sigmoid(SWIGLU_ALPHA * gate) * (up + 1.0)
        yb = act @ w_down[e] + b_down[e]
        return yb * w[:, None].astype(yb.dtype)

    rows = lax.map(expert_block, (row_tok.reshape(n_blocks, MOE_BLOCK),
                                  row_w.reshape(n_blocks, MOE_BLOCK), block_e))
    return jnp.zeros_like(h).at[row_tok].add(rows.reshape(n_rows, D_MODEL))


def setup_inputs(seed: int = 0) -> dict:
    key = jax.random.key(seed)
    ks = jax.random.split(key, 32)
    f32 = jnp.float32

    def nrm(k, shape, scale):
        return jax.random.normal(k, shape, f32) * scale

    L = DEPTH
    n_idx = jnp.arange(S5_STATE, dtype=f32)
    s5_shape = (L, N_DIR, S5_GROUPS, S5_STATE)
    return {
        'x': nrm(ks[0], (BATCH, SEQ, D_MODEL), 1.0),
        'c': nrm(ks[1], (BATCH, D_MODEL), 1.0),
        'ctx': nrm(ks[2], (BATCH, CTX_LEN, D_MODEL), 1.0),
        'c_ctx': nrm(ks[3], (D_MODEL,), 1.0),
        'w_mod': nrm(ks[4], (L, D_MODEL, N_MOD * D_MODEL), 0.5 * D_MODEL ** -0.5),
        'b_mod': nrm(ks[5], (L, N_MOD * D_MODEL), 0.01),
        'norm1_g': 1.0 + nrm(ks[6], (L, D_MODEL), 0.1),
        'norm2_g': 1.0 + nrm(ks[7], (L, D_MODEL), 0.1),
        'w_in': nrm(ks[8], (L, D_MODEL, IN_WIDTH), D_MODEL ** -0.5),
        's5_a_re': -0.5 + nrm(ks[9], s5_shape, 0.02),
        's5_a_im': math.pi * n_idx + nrm(ks[10], s5_shape, 0.02),
        's5_log_dt': jax.random.uniform(ks[11], (L, N_DIR, S5_GROUPS), f32,
                                        math.log(1e-3), math.log(1e-1)),
        's5_b_re': nrm(ks[12], (L, N_DIR, S5_GROUPS, S5_STATE, S5_GROUP), (2 * S5_GROUP) ** -0.5),
        's5_b_im': nrm(ks[13], (L, N_DIR, S5_GROUPS, S5_STATE, S5_GROUP), (2 * S5_GROUP) ** -0.5),
        's5_c_re': nrm(ks[14], (L, N_DIR, S5_GROUPS, S5_GROUP, S5_STATE), 0.5),
        's5_c_im': nrm(ks[15], (L, N_DIR, S5_GROUPS, S5_GROUP, S5_STATE), 0.5),
        's5_d': nrm(ks[16], (L, S5_WIDTH), 0.5),
        's5_w_glu': nrm(ks[17], (L, S5_WIDTH, S5_WIDTH), S5_WIDTH ** -0.5),
        'sgu_norm_g': 1.0 + nrm(ks[18], (L, SGU_WIDTH), 0.1),
        'sgu_w': nrm(ks[19], (L, SGU_HEADS, CHUNK, CHUNK), 0.5 * CHUNK ** -0.5),
        'sgu_b': 1.0 + nrm(ks[20], (L, SGU_HEADS, CHUNK), 0.1),
        'w_branch': nrm(ks[21], (L, N_BRANCH, BRANCH_WIDTH, D_MODEL), BRANCH_WIDTH ** -0.5),
        'w_gate': nrm(ks[22], (L, D_MODEL, N_BRANCH * D_MODEL), D_MODEL ** -0.5),
        'b_gate': nrm(ks[23], (L, N_BRANCH * D_MODEL), 0.1),
        'w_out': nrm(ks[24], (L, D_MODEL, D_MODEL), D_MODEL ** -0.5),
        'router_w': nrm(ks[25], (L, D_MODEL, N_EXPERTS), D_MODEL ** -0.5),
        'router_b': nrm(ks[26], (L, N_EXPERTS), 0.01),
        'moe_w_up': nrm(ks[27], (L, N_EXPERTS, D_MODEL, 2 * D_FF), D_MODEL ** -0.5),
        'moe_b_up': nrm(ks[28], (L, N_EXPERTS, 2 * D_FF), 0.01),
        'moe_w_down': nrm(ks[29], (L, N_EXPERTS, D_FF, D_MODEL), D_FF ** -0.5),
        'moe_b_down': nrm(ks[30], (L, N_EXPERTS, D_MODEL), 0.01),
        'final_g': 1.0 + nrm(ks[31], (D_MODEL,), 0.1),
    }


def reference(x, c, ctx, c_ctx, w_mod, b_mod, norm1_g, norm2_g, w_in,
              s5_a_re, s5_a_im, s5_log_dt, s5_b_re, s5_b_im, s5_c_re, s5_c_im, s5_d, s5_w_glu,
              sgu_norm_g, sgu_w, sgu_b, w_branch, w_gate, b_gate, w_out,
              router_w, router_b, moe_w_up, moe_b_up, moe_w_down, moe_b_down, final_g):
    bsz, seq_len, _ = x.shape
    rows = seq_len // GRID_W
    n_chunks_lat = rows // CHUNK_ROWS
    ctx_len = ctx.shape[1]
    n_chunks_ctx = ctx_len // CHUNK
    xc = ctx
    for i in range(DEPTH):
        last = i == DEPTH - 1
        sh1, sc1, g1, sh2, sc2, g2 = [m[:, None, :] for m in adaln(c, w_mod[i], b_mod[i])]
        csh1, csc1, cg1, csh2, csc2, cg2 = adaln(c_ctx, w_mod[i], b_mod[i])

        h = modulate(rmsnorm(x, norm1_g[i]), sh1, sc1)
        hc = modulate(rmsnorm(xc, norm1_g[i]), csh1, csc1)
        a_bar, b_bar = s5_discretize(s5_a_re[i], s5_a_im[i], s5_log_dt[i], s5_b_re[i], s5_b_im[i])
        zc = hc @ (w_in[i][:, :S5_WIDTH] if last else w_in[i])
        uc = zc[..., :S5_WIDTH].reshape(bsz, ctx_len, S5_GROUPS, S5_GROUP)
        h0 = jnp.zeros((N_DIR, bsz, S5_GROUPS, S5_STATE), jnp.complex64)
        states_c = s5_scan(uc, a_bar, b_bar, h0)
        z = h @ w_in[i]
        u = z[..., :S5_WIDTH].reshape(bsz, seq_len, S5_GROUPS, S5_GROUP)
        states = s5_scan(u, a_bar, b_bar, states_c[:, :, -1])
        feats = branch_features(z, states, s5_c_re[i], s5_c_im[i], s5_d[i], s5_w_glu[i],
                                sgu_norm_g[i], sgu_w[i], sgu_b[i], n_chunks_lat)
        x = x + g1 * gated_merge(h, feats, w_branch[i], w_gate[i], b_gate[i], w_out[i])
        if not last:
            feats_c = branch_features(zc, states_c, s5_c_re[i], s5_c_im[i], s5_d[i], s5_w_glu[i],
                                      sgu_norm_g[i], sgu_w[i], sgu_b[i], n_chunks_ctx)
            xc = xc + cg1 * gated_merge(hc, feats_c, w_branch[i], w_gate[i], b_gate[i], w_out[i])

        h2 = modulate(rmsnorm(x, norm2_g[i]), sh2, sc2).reshape(bsz * seq_len, D_MODEL)
        if not last:
            h2c = modulate(rmsnorm(xc, norm2_g[i]), csh2, csc2).reshape(bsz * ctx_len, D_MODEL)
            y = moe_ffn(jnp.concatenate([h2, h2c], axis=0), router_w[i], router_b[i],
                        moe_w_up[i], moe_b_up[i], moe_w_down[i], moe_b_down[i])
            x = x + g2 * y[:bsz * seq_len].reshape(bsz, seq_len, D_MODEL)
            xc = xc + cg2 * y[bsz * seq_len:].reshape(bsz, ctx_len, D_MODEL)
        else:
            y = moe_ffn(h2, router_w[i], router_b[i],
                        moe_w_up[i], moe_b_up[i], moe_w_down[i], moe_b_down[i])
            x = x + g2 * y.reshape(bsz, seq_len, D_MODEL)
    return rmsnorm(x, final_g)
```

```python
import functools
import math

import jax
import jax.numpy as jnp
from jax import lax
from jax.experimental import pallas as pl
from jax.experimental.pallas import tpu as pltpu

F32 = jnp.float32
BF16 = jnp.bfloat16
I32 = jnp.int32

EPS = 1e-6
S5_WIDTH = 512
S5_GROUP = 16
S5_GROUPS = 32
S5_STATE = 64
FFT_WIDTH = 512
FFT_GROUPS = 4
FFT_GROUP = 128
SGU_WIDTH = 512
SGU_HEADS = 4
SGU_HEAD = 128
CHUNK = 128
N_BRANCH = 3
N_EXPERTS = 32
TOP_K = 4
SWIGLU_LIMIT = 7.0
SWIGLU_ALPHA = 1.702
N_MOD = 6

LANES = 128
VMEM_LIMIT_BYTES = 56 * 1024 * 1024

S5_T = 16
ROW_TILE = 256
MERGE_TILE = 384
MERGE_COLS = 512
FFT_KTILE = 256
MOE_CH = 256
MOE_RMAX = 2048
MOE_TF = 256
DISP_TILE = 256


def _cparams(sem, vmem=VMEM_LIMIT_BYTES):
    return pltpu.CompilerParams(dimension_semantics=sem, vmem_limit_bytes=vmem)


def _gelu(x):
    return 0.5 * x * (1.0 + jnp.tanh(math.sqrt(2.0 / math.pi) * (x + 0.044715 * x * x * x)))


def _sigmoid(x):
    return 1.0 / (1.0 + jnp.exp(-x))


def _adaln_kernel(c_ref, w_ref, b_ref, o_ref):
    c = c_ref[...]
    s = c * _sigmoid(c)
    o_ref[...] = jnp.dot(s, w_ref[...], preferred_element_type=F32,
                         precision=lax.Precision.HIGHEST) + b_ref[...]


def _adaln(cond8, w_mod, b_mod):
    n_layers, d, nm = w_mod.shape
    tn = 1024
    return pl.pallas_call(
        _adaln_kernel,
        grid=(n_layers, nm // tn),
        in_specs=[pl.BlockSpec((8, d), lambda l, j: (0, 0)),
                  pl.BlockSpec((None, d, tn), lambda l, j: (l, 0, j)),
                  pl.BlockSpec((None, 1, tn), lambda l, j: (l, 0, j))],
        out_specs=pl.BlockSpec((None, 8, tn), lambda l, j: (l, 0, j)),
        out_shape=jax.ShapeDtypeStruct((n_layers, 8, nm), F32),
        compiler_params=_cparams(("arbitrary", "arbitrary")),
        name="adaln",
    )(cond8, w_mod, b_mod.reshape(n_layers, 1, nm))


def _inproj_kernel(x_ref, g_ref, sh_ref, sc_ref, w_ref, h_ref, zs_ref, zf_ref, zg_ref):
    x = x_ref[...]
    y = x * lax.rsqrt(jnp.mean(x * x, axis=-1, keepdims=True) + EPS) * g_ref[...]
    h = (y * (1.0 + sc_ref[...]) + sh_ref[...]).astype(BF16)
    h_ref[...] = h
    z = jnp.dot(h, w_ref[...], preferred_element_type=F32)
    zs_ref[...] = z[:, :S5_WIDTH]
    zf_ref[...] = z[:, S5_WIDTH:S5_WIDTH + FFT_WIDTH]
    zg_ref[...] = z[:, S5_WIDTH + FFT_WIDTH:]


def _inproj(xa, n_lat, g, sh, sc, w_in_bf):
    n, d = xa.shape
    tm = ROW_TILE
    lat_tiles = n_lat // tm
    row = lambda i: (i, 0)
    kind = lambda i: (jnp.where(i >= lat_tiles, 1, 0), 0, 0)
    return pl.pallas_call(
        _inproj_kernel,
        grid=(n // tm,),
        in_specs=[pl.BlockSpec((tm, d), row),
                  pl.BlockSpec((1, d), lambda i: (0, 0)),
                  pl.BlockSpec((None, 1, d), kind),
                  pl.BlockSpec((None, 1, d), kind),
                  pl.BlockSpec((d, w_in_bf.shape[1]), lambda i: (0, 0))],
        out_specs=[pl.BlockSpec((tm, d), row),
                   pl.BlockSpec((tm, S5_WIDTH), row),
                   pl.BlockSpec((tm, FFT_WIDTH), row),
                   pl.BlockSpec((tm, 2 * SGU_WIDTH), row)],
        out_shape=[jax.ShapeDtypeStruct((n, d), BF16),
                   jax.ShapeDtypeStruct((n, S5_WIDTH), F32),
                   jax.ShapeDtypeStruct((n, FFT_WIDTH), F32),
                   jax.ShapeDtypeStruct((n, 2 * SGU_WIDTH), F32)],
        compiler_params=_cparams(("arbitrary",)),
        name="inproj",
    )(xa, g.reshape(1, d), sh, sc, w_in_bf)


def _s5_operators(a_re, a_im, log_dt, b_re, b_im, c_re, c_im, d_skip, t_len, n_steps):
    lam = lax.complex(a_re.astype(F32), a_im.astype(F32))
    dt = jnp.exp(log_dt.astype(F32))[..., None]
    ldt = lam * dt
    tau = jnp.arange(t_len + 1, dtype=F32)
    pw = jnp.exp(ldt[:, :, None, :] * tau[None, None, :, None])
    a_bar = jnp.exp(ldt)
    bb = ((a_bar - 1.0) / lam)[..., None] * lax.complex(b_re.astype(F32), b_im.astype(F32))
    cc = lax.complex(c_re.astype(F32), c_im.astype(F32))
    k = jnp.real(jnp.einsum('dghp,dgtp,dgpk->dgthk', cc, pw[:, :, :t_len], bb))
    ii = jnp.arange(t_len)
    diff = ii[None, :] - ii[:, None]
    kf = k[0][:, jnp.clip(diff, 0, t_len - 1)]
    kb = k[1][:, jnp.clip(-diff, 0, t_len - 1)]
    m = jnp.where((diff >= 0)[None, :, :, None, None], kf, 0.0) + \
        jnp.where((diff <= 0)[None, :, :, None, None], kb, 0.0)
    eye_t = jnp.eye(t_len, dtype=F32)
    eye_h = jnp.eye(S5_GROUP, dtype=F32)
    dd = d_skip.astype(F32).reshape(S5_GROUPS, S5_GROUP)
    m = m + eye_t[None, :, :, None, None] * (eye_h[None] * dd[:, :, None])[:, None, None, :, :]
    g = S5_GROUPS
    tw = t_len * S5_GROUP
    m = m.transpose(0, 1, 4, 2, 3).reshape(g, tw, tw)
    wf = pw[0][:, ::-1][:, 1:, :, None] * bb[0][:, None, :, :]
    wb = pw[1][:, :t_len, :, None] * bb[1][:, None, :, :]

    def state_cols(w):
        w = w.transpose(0, 1, 3, 2)
        return jnp.concatenate([jnp.real(w), jnp.imag(w)], axis=-1).reshape(g, tw, 2 * S5_STATE)

    mop = jnp.concatenate([m, state_cols(wf), state_cols(wb)], axis=-1).astype(BF16)
    gf = cc[0][:, None, :, :] * pw[0][:, 1:, None, :]
    gb = cc[1][:, None, :, :] * pw[1][:, ::-1][:, :t_len, None, :]

    def state_rows(gm):
        gm = gm.transpose(0, 3, 1, 2).reshape(g, S5_STATE, tw)
        return jnp.concatenate([jnp.real(gm), -jnp.imag(gm)], axis=1)

    vop = jnp.concatenate([state_rows(gf), state_rows(gb)], axis=1).astype(BF16)
    steps = (t_len * (2.0 ** jnp.arange(n_steps, dtype=F32)))
    am = jnp.exp(ldt[:, :, None, :] * steps[None, None, :, None])
    pvec = jnp.concatenate([jnp.real(am), jnp.real(am)], axis=-1)
    qvec = jnp.concatenate([-jnp.imag(am), jnp.imag(am)], axis=-1)
    pq = jnp.stack([pvec, qvec], axis=3)
    pq = pq.transpose(1, 0, 2, 3, 4).reshape(g, 4 * n_steps, 2 * S5_STATE)
    return mop, vop, pq


def _s5_kernel(u_ref, mop_ref, vop_ref, pq_ref, y_ref, *, lat_chunks, n_steps, tw):
    nc = u_ref.shape[0]
    sw = 2 * S5_STATE
    p = jnp.dot(u_ref[...], mop_ref[...], preferred_element_type=F32)
    row = lax.broadcasted_iota(I32, (nc, sw), 0)
    pos_f = jnp.where(row >= lat_chunks, row - lat_chunks, row + (nc - lat_chunks))
    pos_b = (nc - 1) - row

    def chain(xl, pos, base, forward):
        x = jnp.where(pos >= 1, pltpu.roll(xl, 1 if forward else nc - 1, 0), 0.0)
        for m in range(n_steps):
            k = 1 << m
            s = jnp.where(pos >= k, pltpu.roll(x, k if forward else nc - k, 0), 0.0)
            pv = pq_ref[base + 2 * m:base + 2 * m + 1, :]
            qv = pq_ref[base + 2 * m + 1:base + 2 * m + 2, :]
            x = x + pv * s + qv * pltpu.roll(s, S5_STATE, 1)
        return x

    xf = chain(p[:, tw:tw + sw], pos_f, 0, True)
    xb = chain(p[:, tw + sw:tw + 2 * sw], pos_b, 2 * n_steps, False)
    xin = jnp.concatenate([xf, xb], axis=1).astype(BF16)
    y_ref[...] = p[:, :tw] + jnp.dot(xin, vop_ref[...], preferred_element_type=F32)


def _s5_mix(z_s5, n_lat, mop, vop, pq, t_len, n_steps):
    n = z_s5.shape[0]
    nc = n // t_len
    g = S5_GROUPS
    tw = t_len * S5_GROUP
    u = z_s5.reshape(nc, t_len, g, S5_GROUP).transpose(2, 0, 1, 3).reshape(g, nc, tw).astype(BF16)
    y = pl.pallas_call(
        functools.partial(_s5_kernel, lat_chunks=n_lat // t_len, n_steps=n_steps, tw=tw),
        grid=(g,),
        in_specs=[pl.BlockSpec((None, nc, tw), lambda i: (i, 0, 0)),
                  pl.BlockSpec((None, tw, tw + 4 * S5_STATE), lambda i: (i, 0, 0)),
                  pl.BlockSpec((None, 4 * S5_STATE, tw), lambda i: (i, 0, 0)),
                  pl.BlockSpec((None, 4 * n_steps, 2 * S5_STATE), lambda i: (i, 0, 0))],
        out_specs=pl.BlockSpec((None, nc, tw), lambda i: (i, 0, 0)),
        out_shape=jax.ShapeDtypeStruct((g, nc, tw), F32),
        compiler_params=_cparams(("arbitrary",)),
        name="s5_mix",
    )(u, mop, vop, pq)
    return y.reshape(g, nc, t_len, S5_GROUP).transpose(1, 2, 0, 3).reshape(n, S5_WIDTH)


def _dft_tables(n_pos):
    half = n_pos // 2
    blk = min(256, half)
    kk = jnp.arange(half, dtype=I32)[:, None]
    ang = lambda prod: (2.0 * math.pi / n_pos) * (prod % n_pos).astype(F32)
    base = ang(kk * jnp.arange(blk, dtype=I32)[None, :])
    phase = ang(kk * (blk * jnp.arange(half // blk, dtype=I32))[None, :])
    bc, bs = jnp.cos(base)[:, None, :], jnp.sin(base)[:, None, :]
    pc, ps = jnp.cos(phase)[:, :, None], jnp.sin(phase)[:, :, None]
    cm = (pc * bc - ps * bs).reshape(half, half).astype(BF16)
    sm = (ps * bc + pc * bs).reshape(half, half).astype(BF16)
    return cm, sm


def _channel_tables():
    q = jnp.arange(FFT_GROUP, dtype=I32)
    ang = (2.0 * math.pi / FFT_GROUP) * ((q[:, None] * q[None, :]) % FFT_GROUP).astype(F32)
    return jnp.cos(ang).astype(BF16), jnp.sin(ang).astype(BF16)


def _fold_kernel(z_ref, zr_ref, cc_ref, sc_ref, e_ref, o_ref):
    tr = z_ref.shape[0]
    z = z_ref[...]
    zr = zr_ref[...]
    grow = lax.broadcasted_iota(I32, z.shape, 0) + pl.program_id(0) * tr
    e = ((z + zr) * jnp.where(grow == 0, 0.5, 1.0)).astype(BF16)
    o = (z - zr).astype(BF16)
    for g in range(FFT_GROUPS):
        sl = slice(g * FFT_GROUP, (g + 1) * FFT_GROUP)
        e_ref[:, sl] = jnp.dot(e[:, sl], cc_ref[...], preferred_element_type=F32).astype(BF16)
        o_ref[:, sl] = jnp.dot(o[:, sl], sc_ref[...], preferred_element_type=F32).astype(BF16)


def _posdft_kernel(cm_ref, sm_ref, e_ref, o_ref, zh_ref, cc_ref, alt_ref, a_ref, b_ref, *, scale):
    tk = cm_ref.shape[0]
    i = pl.program_id(0)
    csum = jnp.dot(cm_ref[...], e_ref[...], preferred_element_type=F32)
    ssum = jnp.dot(sm_ref[...], o_ref[...], preferred_element_type=F32)
    zh = zh_ref[...].astype(BF16)
    zch = jnp.concatenate(
        [jnp.dot(zh[:, g * FFT_GROUP:(g + 1) * FFT_GROUP], cc_ref[...], preferred_element_type=F32)
         for g in range(FFT_GROUPS)], axis=1)[0:1, :]
    krow = lax.broadcasted_iota(I32, csum.shape, 0) + i * tk
    sign = jnp.where((krow & 1) == 0, 1.0, -1.0)
    csum = csum + sign * zch
    a_ref[...] = (csum - ssum) * scale
    b_ref[...] = (csum + ssum) * scale

    @pl.when(i == 0)
    def _():
        nyq = jnp.dot(alt_ref[...], e_ref[...], preferred_element_type=F32)[0:1, :] + zch
        b_ref[0:1, :] = nyq * scale


def _ctx_dft_kernel(z_ref, cc_ref, sc_ref, cm_ref, sm_ref, o_ref, *, scale):
    z = z_ref[...].astype(BF16)
    for g in range(FFT_GROUPS):
        sl = slice(g * FFT_GROUP, (g + 1) * FFT_GROUP)
        zc = jnp.dot(z[:, sl], cc_ref[...], preferred_element_type=F32).astype(BF16)
        zs = jnp.dot(z[:, sl], sc_ref[...], preferred_element_type=F32).astype(BF16)
        o_ref[:, sl] = (jnp.dot(cm_ref[...], zc, preferred_element_type=F32)
                        - jnp.dot(sm_ref[...], zs, preferred_element_type=F32)) * scale


def _fourier_latent(z, cm, sm, cc, sc):
    n, w = z.shape
    half = n // 2
    zr = jnp.roll(jnp.flip(z, axis=0), 1, axis=0)
    tr = min(512, half)
    e, o = pl.pallas_call(
        _fold_kernel,
        grid=(half // tr,),
        in_specs=[pl.BlockSpec((tr, w), lambda i: (i, 0)),
                  pl.BlockSpec((tr, w), lambda i: (i, 0)),
                  pl.BlockSpec((FFT_GROUP, FFT_GROUP), lambda i: (0, 0)),
                  pl.BlockSpec((FFT_GROUP, FFT_GROUP), lambda i: (0, 0))],
        out_specs=[pl.BlockSpec((tr, w), lambda i: (i, 0))] * 2,
        out_shape=[jax.ShapeDtypeStruct((half, w), BF16)] * 2,
        compiler_params=_cparams(("arbitrary",)),
        name="fft_fold",
    )(z, zr, cc, sc)
    tk = min(FFT_KTILE, half)
    zh = lax.dynamic_slice_in_dim(z, half, 8, axis=0) if n > half + 8 else jnp.pad(z[half:], ((0, 8 - (n - half)), (0, 0)))
    alt = jnp.broadcast_to(jnp.where(jnp.arange(half) % 2 == 0, 1.0, -1.0).astype(BF16)[None, :], (8, half))
    scale = 1.0 / math.sqrt(n * FFT_GROUP)
    a, b = pl.pallas_call(
        functools.partial(_posdft_kernel, scale=scale),
        grid=(half // tk,),
        in_specs=[pl.BlockSpec((tk, half), lambda i: (i, 0)),
                  pl.BlockSpec((tk, half), lambda i: (i, 0)),
                  pl.BlockSpec((half, w), lambda i: (0, 0)),
                  pl.BlockSpec((half, w), lambda i: (0, 0)),
                  pl.BlockSpec((8, w), lambda i: (0, 0)),
                  pl.BlockSpec((FFT_GROUP, FFT_GROUP), lambda i: (0, 0)),
                  pl.BlockSpec((8, half), lambda i: (0, 0))],
        out_specs=[pl.BlockSpec((tk, w), lambda i: (i, 0))] * 2,
        out_shape=[jax.ShapeDtypeStruct((half, w), F32)] * 2,
        compiler_params=_cparams(("arbitrary",)),
        name="fft_posdft",
    )(cm, sm, e, o, zh, cc, alt)
    return jnp.concatenate([a, jnp.roll(jnp.flip(b, axis=0), 1, axis=0)], axis=0)


def _fourier_ctx(z, cc, sc):
    n, w = z.shape
    q = jnp.arange(n, dtype=I32)
    ang = (2.0 * math.pi / n) * ((q[:, None] * q[None, :]) % n).astype(F32)
    cm, sm = jnp.cos(ang).astype(BF16), jnp.sin(ang).astype(BF16)
    full = lambda s: pl.BlockSpec(s, lambda i: (0,) * len(s))
    return pl.pallas_call(
        functools.partial(_ctx_dft_kernel, scale=1.0 / math.sqrt(n * FFT_GROUP)),
        grid=(1,),
        in_specs=[full((n, w)), full((FFT_GROUP, FFT_GROUP)), full((FFT_GROUP, FFT_GROUP)),
                  full((n, n)), full((n, n))],
        out_specs=full((n, w)),
        out_shape=jax.ShapeDtypeStruct((n, w), F32),
        compiler_params=_cparams(("arbitrary",)),
        name="fft_ctx",
    )(z, cc, sc, cm, sm)


def _merge_kernel(h_ref, zg_ref, ys_ref, yf_ref, wglu_ref, gv_ref, sw_ref, sb_ref,
                  wg0_ref, wg1_ref, wg2_ref, bg0_ref, bg1_ref, bg2_ref,
                  wb0_ref, wb1_ref, wb2_ref, m_ref, feats):
    j = pl.program_id(1)
    tm = h_ref.shape[0]

    @pl.when(j == 0)
    def _():
        y = _gelu(ys_ref[...])
        feats[0] = (y * _sigmoid(jnp.dot(y.astype(BF16), wglu_ref[...],
                                         preferred_element_type=F32))).astype(BF16)
        feats[1] = yf_ref[...].astype(BF16)
        gz = _gelu(zg_ref[...])
        u = gz[:, :SGU_WIDTH]
        v = gz[:, SGU_WIDTH:]
        v = v * lax.rsqrt(jnp.mean(v * v, axis=-1, keepdims=True) + EPS) * gv_ref[...]
        vb = v.astype(BF16)
        for c in range(tm // CHUNK):
            rs = slice(c * CHUNK, (c + 1) * CHUNK)
            for hd in range(SGU_HEADS):
                cs = slice(hd * SGU_HEAD, (hd + 1) * SGU_HEAD)
                s = jnp.dot(sw_ref[hd], vb[rs, cs], preferred_element_type=F32) + sb_ref[hd]
                feats[2, rs, cs] = (u[rs, cs] * s).astype(BF16)

    h = h_ref[...]
    acc = None
    for k, (wg, bg, wb) in enumerate(((wg0_ref, bg0_ref, wb0_ref), (wg1_ref, bg1_ref, wb1_ref),
                                      (wg2_ref, bg2_ref, wb2_ref))):
        gate = _sigmoid(jnp.dot(h, wg[...], preferred_element_type=F32) + bg[...])
        br = jnp.dot(feats[k], wb[...], preferred_element_type=F32)
        acc = gate * br if acc is None else acc + gate * br
    m_ref[...] = acc.astype(BF16)


def _merge(h, zg, ys, yf, wglu_bf, gv, sw_bf, sb_full, wgate_bf, bgate, wbranch_bf):
    n, d = h.shape
    tm, tn = MERGE_TILE, MERGE_COLS
    nj = d // tn
    row = lambda i, j: (i, 0)
    const2 = lambda i, j: (0, 0)
    const3 = lambda i, j: (0, 0, 0)
    gate_spec = lambda k: pl.BlockSpec((d, tn), lambda i, j: (0, k * nj + j))
    bias_spec = lambda k: pl.BlockSpec((1, tn), lambda i, j: (0, k * nj + j))
    br_spec = lambda k: pl.BlockSpec((None, S5_WIDTH, tn), lambda i, j: (k, 0, j))
    return pl.pallas_call(
        _merge_kernel,
        grid=(n // tm, nj),
        in_specs=[pl.BlockSpec((tm, d), row),
                  pl.BlockSpec((tm, 2 * SGU_WIDTH), row),
                  pl.BlockSpec((tm, S5_WIDTH), row),
                  pl.BlockSpec((tm, FFT_WIDTH), row),
                  pl.BlockSpec((S5_WIDTH, S5_WIDTH), const2),
                  pl.BlockSpec((1, SGU_WIDTH), const2),
                  pl.BlockSpec((SGU_HEADS, CHUNK, CHUNK), const3),
                  pl.BlockSpec((SGU_HEADS, CHUNK, SGU_HEAD), const3),
                  gate_spec(0), gate_spec(1), gate_spec(2),
                  bias_spec(0), bias_spec(1), bias_spec(2),
                  br_spec(0), br_spec(1), br_spec(2)],
        out_specs=pl.BlockSpec((tm, tn), lambda i, j: (i, j)),
        out_shape=jax.ShapeDtypeStruct((n, d), BF16),
        scratch_shapes=[pltpu.VMEM((N_BRANCH, tm, S5_WIDTH), BF16)],
        compiler_params=_cparams(("arbitrary", "arbitrary")),
        name="gated_merge",
    )(h, zg, ys, yf, wglu_bf, gv.reshape(1, SGU_WIDTH), sw_bf, sb_full,
      wgate_bf, wgate_bf, wgate_bf, bgate, bgate, bgate, wbranch_bf, wbranch_bf, wbranch_bf)


def _outproj_kernel(x_ref, m_ref, wo_ref, g1_ref, g2_ref, sh_ref, sc_ref, rw_ref, rb_ref,
                    x1_ref, h2_ref, ti_ref, tw_ref):
    x1 = x_ref[...] + g1_ref[...] * jnp.dot(m_ref[...], wo_ref[...], preferred_element_type=F32)
    x1_ref[...] = x1
    y = x1 * lax.rsqrt(jnp.mean(x1 * x1, axis=-1, keepdims=True) + EPS) * g2_ref[...]
    h2 = y * (1.0 + sc_ref[...]) + sh_ref[...]
    h2_ref[...] = h2
    logits = jnp.dot(h2, rw_ref[...], preferred_element_type=F32,
                     precision=lax.Precision.HIGHEST) + rb_ref[...]
    lane = lax.broadcasted_iota(I32, logits.shape, 1)
    vals, idxs = [], []
    for _ in range(TOP_K):
        mx = jnp.max(logits, axis=-1, keepdims=True)
        am = jnp.min(jnp.where(logits == mx, lane, LANES), axis=-1, keepdims=True)
        vals.append(mx)
        idxs.append(am)
        logits = jnp.where(lane == am, -jnp.inf, logits)
    ex = [jnp.exp(v - vals[0]) for v in vals]
    den = ex[0] + ex[1] + ex[2] + ex[3]
    ti = jnp.zeros(lane.shape, I32)
    tw = jnp.zeros(lane.shape, F32)
    for k in range(TOP_K):
        ti = jnp.where(lane == k, idxs[k], ti)
        tw = jnp.where(lane == k, ex[k] / den, tw)
    ti_ref[...] = ti
    tw_ref[...] = tw


def _outproj(xa, m, n_lat, wo_bf, g1, g2n, sh2, sc2, rw_pad, rb_pad):
    n, d = xa.shape
    tm = ROW_TILE
    lat_tiles = n_lat // tm
    row = lambda i: (i, 0)
    kind = lambda i: (jnp.where(i >= lat_tiles, 1, 0), 0, 0)
    const = lambda i: (0, 0)
    return pl.pallas_call(
        _outproj_kernel,
        grid=(n // tm,),
        in_specs=[pl.BlockSpec((tm, d), row),
                  pl.BlockSpec((tm, d), row),
                  pl.BlockSpec((d, d), const),
                  pl.BlockSpec((None, 1, d), kind),
                  pl.BlockSpec((1, d), const),
                  pl.BlockSpec((None, 1, d), kind),
                  pl.BlockSpec((None, 1, d), kind),
                  pl.BlockSpec((d, LANES), const),
                  pl.BlockSpec((1, LANES), const)],
        out_specs=[pl.BlockSpec((tm, d), row), pl.BlockSpec((tm, d), row),
                   pl.BlockSpec((tm, LANES), row), pl.BlockSpec((tm, LANES), row)],
        out_shape=[jax.ShapeDtypeStruct((n, d), F32), jax.ShapeDtypeStruct((n, d), F32),
                   jax.ShapeDtypeStruct((n, LANES), I32), jax.ShapeDtypeStruct((n, LANES), F32)],
        compiler_params=_cparams(("arbitrary",)),
        name="outproj_router",
    )(xa, m, wo_bf, g1, g2n.reshape(1, d), sh2, sc2, rw_pad, rb_pad)


def _routing_tables(top_idx, n_items):
    flat_e = top_idx.reshape(-1)
    onehot = (flat_e[:, None] == jnp.arange(N_EXPERTS, dtype=I32)[None, :]).astype(I32)
    csum = jnp.cumsum(onehot, axis=0)
    rank = jnp.sum((csum - onehot) * onehot, axis=1)
    counts = csum[-1]
    padded = (counts + MOE_CH - 1) // MOE_CH * MOE_CH
    pend = jnp.cumsum(padded)
    pstart = pend - padded
    dest = jnp.sum(onehot * pstart[None, :], axis=1) + rank
    per_e = (padded + MOE_RMAX - 1) // MOE_RMAX
    iend = jnp.cumsum(per_e)
    istart = iend - per_e
    total = iend[-1]
    t = jnp.arange(n_items, dtype=I32)
    valid = t < total
    tc = jnp.minimum(t, total - 1)
    e_of = jnp.minimum(jnp.sum((tc[:, None] >= iend[None, :]).astype(I32), axis=1), N_EXPERTS - 1)
    local = tc - istart[e_of]
    row0 = pstart[e_of] + local * MOE_RMAX
    nch = jnp.clip((padded[e_of] - local * MOE_RMAX) // MOE_CH, 0, MOE_RMAX // MOE_CH)
    nch = jnp.where(valid, nch, 0)
    return (dest.astype(I32), counts.astype(I32), pstart.astype(I32), padded.astype(I32),
            e_of.astype(I32), row0.astype(I32), nch.astype(I32))


def _dispatch_kernel(cnt_ref, pst_ref, pad_ref, dest_ref, h_ref, xs_ref, zrow, sem, zsem):
    i = pl.program_id(0)
    tq = h_ref.shape[0]

    @pl.when(i == 0)
    def _():
        zrow[...] = jnp.zeros(zrow.shape, zrow.dtype)

        def per_expert(e, carry):
            first = pst_ref[e] + cnt_ref[e]
            npad = pad_ref[e] - cnt_ref[e]

            def start(r, c):
                pltpu.make_async_copy(zrow.at[pl.ds(0, 1)], xs_ref.at[pl.ds(first + r, 1)], zsem).start()
                return c

            def wait(r, c):
                pltpu.make_async_copy(zrow.at[pl.ds(0, 1)], xs_ref.at[pl.ds(first + r, 1)], zsem).wait()
                return c

            lax.fori_loop(0, npad, start, 0)
            lax.fori_loop(0, npad, wait, 0)
            return carry

        lax.fori_loop(0, N_EXPERTS, per_expert, 0)

    def row_copy(a):
        t = lax.shift_right_logical(a, 2)
        return pltpu.make_async_copy(h_ref.at[pl.ds(t, 1)], xs_ref.at[pl.ds(dest_ref[0, a], 1)], sem)

    def start(a, c):
        row_copy(a).start()
        return c

    def wait(a, c):
        row_copy(a).wait()
        return c

    lax.fori_loop(0, tq * TOP_K, start, 0)
    lax.fori_loop(0, tq * TOP_K, wait, 0)


def _dispatch(h2, dest, counts, pstart, padded, n_rows):
    n, d = h2.shape
    tq = DISP_TILE
    nt = n // tq
    dest3 = dest.reshape(nt, 1, tq * TOP_K)
    return pl.pallas_call(
        _dispatch_kernel,
        grid_spec=pltpu.PrefetchScalarGridSpec(
            num_scalar_prefetch=3,
            grid=(nt,),
            in_specs=[pl.BlockSpec((None, 1, tq * TOP_K), lambda i, *_: (i, 0, 0), memory_space=pltpu.SMEM),
                      pl.BlockSpec((tq, d), lambda i, *_: (i, 0))],
            out_specs=pl.BlockSpec(memory_space=pl.ANY),
            scratch_shapes=[pltpu.VMEM((8, d), F32), pltpu.SemaphoreType.DMA(()), pltpu.SemaphoreType.DMA(())]),
        out_shape=jax.ShapeDtypeStruct((n_rows, d), F32),
        compiler_params=_cparams(("arbitrary",)),
        name="moe_dispatch",
    )(counts, pstart, padded, dest3, h2)


def _experts_kernel(ie_ref, ir_ref, in_ref, xs_ref, wg_ref, wu_ref, wd_ref, bg_ref, bu_ref, bd_ref,
                    ys_ref, xstage, xb, acc, ystage, wgb, wub, wdb, sem_in, sem_out):
    t = pl.program_id(0)
    j = pl.program_id(1)
    nf = pl.num_programs(1)
    nch = in_ref[t]
    row0 = ir_ref[t]
    ch = MOE_CH

    def hbm_rows(c):
        return pl.ds(pl.multiple_of(row0 + c * ch, ch), ch)

    def in_copy(c, slot):
        return pltpu.make_async_copy(xs_ref.at[hbm_rows(c)], xstage.at[slot], sem_in.at[slot])

    def out_copy(c, slot):
        return pltpu.make_async_copy(ystage.at[slot], ys_ref.at[hbm_rows(c)], sem_out.at[slot])

    @pl.when(jnp.logical_and(nch > 0, j == 0))
    def _():
        in_copy(0, 0).start()

        def body(c, carry):
            slot = c & 1

            @pl.when(c + 1 < nch)
            def _():
                in_copy(c + 1, 1 - slot).start()

            in_copy(c, slot).wait()
            xb[pl.ds(pl.multiple_of(c * ch, ch), ch), :] = xstage[slot].astype(BF16)
            return carry

        lax.fori_loop(0, nch, body, 0)

    @pl.when(nch > 0)
    def _():
        wgb[...] = wg_ref[...].astype(BF16)
        wub[...] = wu_ref[...].astype(BF16)
        wdb[...] = wd_ref[...].astype(BF16)

        def partial_out(c):
            rows = pl.ds(pl.multiple_of(c * ch, ch), ch)
            x = xb[rows, :]
            gate = jnp.dot(x, wgb[...], preferred_element_type=F32) + bg_ref[...]
            up = jnp.dot(x, wub[...], preferred_element_type=F32) + bu_ref[...]
            gate = jnp.minimum(gate, SWIGLU_LIMIT)
            up = jnp.clip(up, -SWIGLU_LIMIT, SWIGLU_LIMIT)
            act = gate * _sigmoid(SWIGLU_ALPHA * gate) * (up + 1.0)
            return rows, jnp.dot(act.astype(BF16), wdb[...], preferred_element_type=F32)

        @pl.when(j == 0)
        def _():
            def body(c, carry):
                rows, part = partial_out(c)
                acc[rows, :] = part
                return carry
            lax.fori_loop(0, nch, body, 0)

        @pl.when(jnp.logical_and(j > 0, j < nf - 1))
        def _():
            def body(c, carry):
                rows, part = partial_out(c)
                acc[rows, :] += part
                return carry
            lax.fori_loop(0, nch, body, 0)

        @pl.when(j == nf - 1)
        def _():
            def body(c, carry):
                slot = c & 1
                rows, part = partial_out(c)

                @pl.when(c >= 2)
                def _():
                    out_copy(c - 2, slot).wait()

                ystage[slot] = acc[rows, :] + part + bd_ref[...]
                out_copy(c, slot).start()
                return carry

            lax.fori_loop(0, nch, body, 0)

            @pl.when(nch >= 2)
            def _():
                out_copy(nch - 2, nch & 1).wait()

            out_copy(nch - 1, (nch - 1) & 1).wait()


def _experts(xs, layer, w_up, b_up, w_down, b_down, item_e, item_row0, item_nch):
    n_rows, d = xs.shape
    n_layers, n_exp, _, two_f = w_up.shape
    f = two_f // 2
    tf = MOE_TF
    nf = f // tf
    n_items = item_e.shape[0]
    b_up4 = b_up.reshape(n_layers, n_exp, 1, two_f)
    b_down4 = b_down.reshape(n_layers, n_exp, 1, d)

    def jj(t, j, ie, ir, inn):
        return jnp.where(inn[t] > 0, j, nf - 1)

    return pl.pallas_call(
        _experts_kernel,
        grid_spec=pltpu.PrefetchScalarGridSpec(
            num_scalar_prefetch=3,
            grid=(n_items, nf),
            in_specs=[pl.BlockSpec(memory_space=pl.ANY),
                      pl.BlockSpec((None, None, d, tf), lambda t, j, ie, ir, inn: (layer, ie[t], 0, jj(t, j, ie, ir, inn))),
                      pl.BlockSpec((None, None, d, tf), lambda t, j, ie, ir, inn: (layer, ie[t], 0, nf + jj(t, j, ie, ir, inn))),
                      pl.BlockSpec((None, None, tf, d), lambda t, j, ie, ir, inn: (layer, ie[t], jj(t, j, ie, ir, inn), 0)),
                      pl.BlockSpec((None, None, 1, tf), lambda t, j, ie, ir, inn: (layer, ie[t], 0, jj(t, j, ie, ir, inn))),
                      pl.BlockSpec((None, None, 1, tf), lambda t, j, ie, ir, inn: (layer, ie[t], 0, nf + jj(t, j, ie, ir, inn))),
                      pl.BlockSpec((None, None, 1, d), lambda t, j, ie, ir, inn: (layer, ie[t], 0, 0))],
            out_specs=pl.BlockSpec(memory_space=pl.ANY),
            scratch_shapes=[pltpu.VMEM((2, MOE_CH, d), F32),
                            pltpu.VMEM((MOE_RMAX, d), BF16),
                            pltpu.VMEM((MOE_RMAX, d), F32),
                            pltpu.VMEM((2, MOE_CH, d), F32),
                            pltpu.VMEM((d, tf), BF16),
                            pltpu.VMEM((d, tf), BF16),
                            pltpu.VMEM((tf, d), BF16),
                            pltpu.SemaphoreType.DMA((2,)),
                            pltpu.SemaphoreType.DMA((2,))]),
        out_shape=jax.ShapeDtypeStruct((n_rows, d), F32),
        compiler_params=_cparams(("arbitrary", "arbitrary")),
        name="moe_experts",
    )(item_e, item_row0, item_nch, xs, w_up, w_up, w_down, b_up4, b_up4, b_down4)


def _combine_kernel(dc_ref, dn_ref, tw_ref, x1_ref, g2_ref, fg_ref, ys_ref, o_ref, gbuf, sem, *, final):
    i = pl.program_id(0)
    n = pl.num_programs(0)
    tq = x1_ref.shape[0]

    def row_copy(dref, a, slot):
        t = lax.shift_right_logical(a, 2)
        k = a & (TOP_K - 1)
        return pltpu.make_async_copy(ys_ref.at[pl.ds(dref[0, a], 1)], gbuf.at[slot, k, pl.ds(t, 1)], sem.at[slot])

    def issue(dref, slot):
        def body(a, c):
            row_copy(dref, a, slot).start()
            return c
        lax.fori_loop(0, tq * TOP_K, body, 0)

    slot = i & 1

    @pl.when(i == 0)
    def _():
        issue(dc_ref, 0)

    @pl.when(i + 1 < n)
    def _():
        issue(dn_ref, 1 - slot)

    def wait_body(a, c):
        row_copy(dc_ref, a, slot).wait()
        return c

    lax.fori_loop(0, tq * TOP_K, wait_body, 0)
    tw = tw_ref[...]
    y = tw[:, 0:1] * gbuf[slot, 0]
    for k in range(1, TOP_K):
        y = y + tw[:, k:k + 1] * gbuf[slot, k]
    x2 = x1_ref[...] + g2_ref[...] * y
    if final:
        x2 = x2 * lax.rsqrt(jnp.mean(x2 * x2, axis=-1, keepdims=True) + EPS) * fg_ref[...]
    o_ref[...] = x2


def _combine(ys, dest, top_w, x1, n_lat, g2, final_g, final):
    n, d = x1.shape
    tq = DISP_TILE
    nt = n // tq
    lat_tiles = n_lat // tq
    dest3 = dest.reshape(nt, 1, tq * TOP_K)
    kind = lambda i: (jnp.where(i >= lat_tiles, 1, 0), 0, 0)
    return pl.pallas_call(
        functools.partial(_combine_kernel, final=final),
        grid=(nt,),
        in_specs=[pl.BlockSpec((None, 1, tq * TOP_K), lambda i: (i, 0, 0), memory_space=pltpu.SMEM),
                  pl.BlockSpec((None, 1, tq * TOP_K), lambda i: (jnp.minimum(i + 1, nt - 1), 0, 0),
                               memory_space=pltpu.SMEM),
                  pl.BlockSpec((tq, LANES), lambda i: (i, 0)),
                  pl.BlockSpec((tq, d), lambda i: (i, 0)),
                  pl.BlockSpec((None, 1, d), kind),
                  pl.BlockSpec((1, d), lambda i: (0, 0)),
                  pl.BlockSpec(memory_space=pl.ANY)],
        out_specs=pl.BlockSpec((tq, d), lambda i: (i, 0)),
        out_shape=jax.ShapeDtypeStruct((n, d), F32),
        scratch_shapes=[pltpu.VMEM((2, TOP_K, tq, d), F32), pltpu.SemaphoreType.DMA((2,))],
        compiler_params=_cparams(("arbitrary",)),
        name="moe_combine",
    )(dest3, dest3, top_w, x1, g2, final_g.reshape(1, d), ys)


def kernel(x, c, ctx, c_ctx, w_mod, b_mod, norm1_g, norm2_g, w_in, s5_a_re, s5_a_im, s5_log_dt,
           s5_b_re, s5_b_im, s5_c_re, s5_c_im, s5_d, s5_w_glu, sgu_norm_g, sgu_w, sgu_b, w_branch,
           w_gate, b_gate, w_out, router_w, router_b, moe_w_up, moe_b_up, moe_w_down, moe_b_down,
           final_g):
    bsz, n_lat, d = x.shape
    n_ctx = ctx.shape[1]
    assert bsz == 1
    n_layers = w_mod.shape[0]
    n = n_lat + n_ctx
    xa = jnp.concatenate([x[0], ctx[0]], axis=0).astype(F32)

    cond8 = jnp.zeros((8, d), F32).at[0].set(c[0]).at[1].set(c_ctx)
    mods = _adaln(cond8, w_mod, b_mod)[:, :2, :].reshape(n_layers, 2, N_MOD, 1, d)

    n_chunks = n // S5_T
    n_steps = max(1, (n_chunks - 1).bit_length())
    cm, sm = _dft_tables(n_lat)
    cc, sc = _channel_tables()
    n_assign = n * TOP_K
    n_rows = n_assign + N_EXPERTS * MOE_CH
    n_items = -(-n_assign // MOE_RMAX) + N_EXPERTS

    for i in range(n_layers):
        last = i == n_layers - 1
        sh1, sc1, g1, sh2, sc2, g2 = [mods[i, :, k] for k in range(N_MOD)]

        h, z_s5, z_fft, z_sgu = _inproj(xa, n_lat, norm1_g[i], sh1, sc1, w_in[i].astype(BF16))
        mop, vop, pq = _s5_operators(s5_a_re[i], s5_a_im[i], s5_log_dt[i], s5_b_re[i], s5_b_im[i],
                                     s5_c_re[i], s5_c_im[i], s5_d[i], S5_T, n_steps)
        y_s5 = _s5_mix(z_s5, n_lat, mop, vop, pq, S5_T, n_steps)
        y_fft_lat = _fourier_latent(z_fft[:n_lat], cm, sm, cc, sc)
        if last:
            y_fft_ctx = jnp.zeros((n_ctx, FFT_WIDTH), F32)
        else:
            y_fft_ctx = _fourier_ctx(z_fft[n_lat:], cc, sc)
        y_fft = jnp.concatenate([y_fft_lat, y_fft_ctx], axis=0)
        sb_full = jnp.broadcast_to(sgu_b[i].astype(F32)[:, :, None], (SGU_HEADS, CHUNK, SGU_HEAD))
        m = _merge(h, z_sgu, y_s5, y_fft, s5_w_glu[i].astype(BF16), sgu_norm_g[i].astype(F32),
                   sgu_w[i].astype(BF16), sb_full, w_gate[i].astype(BF16),
                   b_gate[i].astype(F32).reshape(1, N_BRANCH * d), w_branch[i].astype(BF16))

        rw_pad = jnp.zeros((d, LANES), F32).at[:, :N_EXPERTS].set(router_w[i].astype(F32))
        rb_pad = jnp.full((1, LANES), -1e30, F32).at[0, :N_EXPERTS].set(router_b[i].astype(F32))
        x1, h2, top_idx, top_w = _outproj(xa, m, n_lat, w_out[i].astype(BF16), g1, norm2_g[i].astype(F32),
                                          sh2, sc2, rw_pad, rb_pad)
        dest, counts, pstart, padded, item_e, item_row0, item_nch = _routing_tables(
            top_idx[:, :TOP_K], n_items)
        xs = _dispatch(h2, dest, counts, pstart, padded, n_rows)
        ys = _experts(xs, i, moe_w_up, moe_b_up, moe_w_down, moe_b_down, item_e, item_row0, item_nch)
        xa = _combine(ys, dest, top_w, x1, n_lat, g2, final_g.astype(F32), last)

    return xa[:n_lat].reshape(bsz, n_lat, d).astype(x.dtype)
```

```python
import functools
import math

import jax
import jax.numpy as jnp
from jax import lax
from jax.experimental import pallas as pl
from jax.experimental.pallas import tpu as pltpu

F32 = jnp.float32
BF16 = jnp.bfloat16
I32 = jnp.int32

EPS = 1e-6
S5_WIDTH = 512
S5_GROUP = 16
S5_GROUPS = 32
S5_STATE = 64
FFT_WIDTH = 512
FFT_GROUPS = 4
FFT_GROUP = 128
SGU_WIDTH = 512
SGU_HEADS = 4
SGU_HEAD = 128
CHUNK = 128
N_BRANCH = 3
N_EXPERTS = 32
TOP_K = 4
SWIGLU_LIMIT = 7.0
SWIGLU_ALPHA = 1.702
N_MOD = 6

LANES = 128
VMEM_LIMIT_BYTES = 56 * 1024 * 1024

S5_T = 16
ROW_TILE = 256
MERGE_TILE = 384
MERGE_COLS = 512
FFT_KTILE = 256
MOE_CH = 256
MOE_RMAX = 2048
MOE_TF = 256
DISP_TILE = 256


def _cparams(sem, vmem=VMEM_LIMIT_BYTES):
    return pltpu.CompilerParams(dimension_semantics=sem, vmem_limit_bytes=vmem)


def _gelu(x):
    return 0.5 * x * (1.0 + jnp.tanh(math.sqrt(2.0 / math.pi) * (x + 0.044715 * x * x * x)))


def _sigmoid(x):
    return 1.0 / (1.0 + jnp.exp(-x))


def _adaln_kernel(c_ref, w_ref, b_ref, o_ref):
    c = c_ref[...]
    s = c * _sigmoid(c)
    o_ref[...] = jnp.dot(s, w_ref[...], preferred_element_type=F32,
                         precision=lax.Precision.HIGHEST) + b_ref[...]


def _adaln(cond8, w_mod, b_mod):
    n_layers, d, nm = w_mod.shape
    tn = 1024
    return pl.pallas_call(
        _adaln_kernel,
        grid=(n_layers, nm // tn),
        in_specs=[pl.BlockSpec((8, d), lambda l, j: (0, 0)),
                  pl.BlockSpec((None, d, tn), lambda l, j: (l, 0, j)),
                  pl.BlockSpec((None, 1, tn), lambda l, j: (l, 0, j))],
        out_specs=pl.BlockSpec((None, 8, tn), lambda l, j: (l, 0, j)),
        out_shape=jax.ShapeDtypeStruct((n_layers, 8, nm), F32),
        compiler_params=_cparams(("arbitrary", "arbitrary")),
        name="adaln",
    )(cond8, w_mod, b_mod.reshape(n_layers, 1, nm))


def _inproj_kernel(x_ref, g_ref, sh_ref, sc_ref, w_ref, h_ref, zs_ref, zf_ref, zg_ref):
    x = x_ref[...]
    y = x * lax.rsqrt(jnp.mean(x * x, axis=-1, keepdims=True) + EPS) * g_ref[...]
    h = (y * (1.0 + sc_ref[...]) + sh_ref[...]).astype(BF16)
    h_ref[...] = h
    z = jnp.dot(h, w_ref[...], preferred_element_type=F32)
    zs_ref[...] = z[:, :S5_WIDTH]
    zf_ref[...] = z[:, S5_WIDTH:S5_WIDTH + FFT_WIDTH]
    zg_ref[...] = z[:, S5_WIDTH + FFT_WIDTH:]


def _inproj(xa, n_lat, g, sh, sc, w_in_bf):
    n, d = xa.shape
    tm = ROW_TILE
    lat_tiles = n_lat // tm
    row = lambda i: (i, 0)
    kind = lambda i: (jnp.where(i >= lat_tiles, 1, 0), 0, 0)
    return pl.pallas_call(
        _inproj_kernel,
        grid=(n // tm,),
        in_specs=[pl.BlockSpec((tm, d), row),
                  pl.BlockSpec((1, d), lambda i: (0, 0)),
                  pl.BlockSpec((None, 1, d), kind),
                  pl.BlockSpec((None, 1, d), kind),
                  pl.BlockSpec((d, w_in_bf.shape[1]), lambda i: (0, 0))],
        out_specs=[pl.BlockSpec((tm, d), row),
                   pl.BlockSpec((tm, S5_WIDTH), row),
                   pl.BlockSpec((tm, FFT_WIDTH), row),
                   pl.BlockSpec((tm, 2 * SGU_WIDTH), row)],
        out_shape=[jax.ShapeDtypeStruct((n, d), BF16),
                   jax.ShapeDtypeStruct((n, S5_WIDTH), F32),
                   jax.ShapeDtypeStruct((n, FFT_WIDTH), F32),
                   jax.ShapeDtypeStruct((n, 2 * SGU_WIDTH), F32)],
        compiler_params=_cparams(("arbitrary",)),
        name="inproj",
    )(xa, g.reshape(1, d), sh, sc, w_in_bf)


def _s5_operators(a_re, a_im, log_dt, b_re, b_im, c_re, c_im, d_skip, t_len, n_steps):
    lam = lax.complex(a_re.astype(F32), a_im.astype(F32))
    dt = jnp.exp(log_dt.astype(F32))[..., None]
    ldt = lam * dt
    tau = jnp.arange(t_len + 1, dtype=F32)
    pw = jnp.exp(ldt[:, :, None, :] * tau[None, None, :, None])
    a_bar = jnp.exp(ldt)
    bb = ((a_bar - 1.0) / lam)[..., None] * lax.complex(b_re.astype(F32), b_im.astype(F32))
    cc = lax.complex(c_re.astype(F32), c_im.astype(F32))
    k = jnp.real(jnp.einsum('dghp,dgtp,dgpk->dgthk', cc, pw[:, :, :t_len], bb))
    ii = jnp.arange(t_len)
    diff = ii[None, :] - ii[:, None]
    kf = k[0][:, jnp.clip(diff, 0, t_len - 1)]
    kb = k[1][:, jnp.clip(-diff, 0, t_len - 1)]
    m = jnp.where((diff >= 0)[None, :, :, None, None], kf, 0.0) + \
        jnp.where((diff <= 0)[None, :, :, None, None], kb, 0.0)
    eye_t = jnp.eye(t_len, dtype=F32)
    eye_h = jnp.eye(S5_GROUP, dtype=F32)
    dd = d_skip.astype(F32).reshape(S5_GROUPS, S5_GROUP)
    m = m + eye_t[None, :, :, None, None] * (eye_h[None] * dd[:, :, None])[:, None, None, :, :]
    g = S5_GROUPS
    tw = t_len * S5_GROUP
    m = m.transpose(0, 1, 4, 2, 3).reshape(g, tw, tw)
    wf = pw[0][:, ::-1][:, 1:, :, None] * bb[0][:, None, :, :]
    wb = pw[1][:, :t_len, :, None] * bb[1][:, None, :, :]

    def state_cols(w):
        w = w.transpose(0, 1, 3, 2)
        return jnp.concatenate([jnp.real(w), jnp.imag(w)], axis=-1).reshape(g, tw, 2 * S5_STATE)

    mop = jnp.concatenate([m, state_cols(wf), state_cols(wb)], axis=-1).astype(BF16)
    gf = cc[0][:, None, :, :] * pw[0][:, 1:, None, :]
    gb = cc[1][:, None, :, :] * pw[1][:, ::-1][:, :t_len, None, :]

    def state_rows(gm):
        gm = gm.transpose(0, 3, 1, 2).reshape(g, S5_STATE, tw)
        return jnp.concatenate([jnp.real(gm), -jnp.imag(gm)], axis=1)

    vop = jnp.concatenate([state_rows(gf), state_rows(gb)], axis=1).astype(BF16)
    steps = (t_len * (2.0 ** jnp.arange(n_steps, dtype=F32)))
    am = jnp.exp(ldt[:, :, None, :] * steps[None, None, :, None])
    pvec = jnp.concatenate([jnp.real(am), jnp.real(am)], axis=-1)
    qvec = jnp.concatenate([-jnp.imag(am), jnp.imag(am)], axis=-1)
    pq = jnp.stack([pvec, qvec], axis=3)
    pq = pq.transpose(1, 0, 2, 3, 4).reshape(g, 4 * n_steps, 2 * S5_STATE)
    return mop, vop, pq


def _s5_kernel(u_ref, mop_ref, vop_ref, pq_ref, y_ref, *, lat_chunks, n_steps, tw):
    nc = u_ref.shape[0]
    sw = 2 * S5_STATE
    p = jnp.dot(u_ref[...], mop_ref[...], preferred_element_type=F32)
    row = lax.broadcasted_iota(I32, (nc, sw), 0)
    pos_f = jnp.where(row >= lat_chunks, row - lat_chunks, row + (nc - lat_chunks))
    pos_b = (nc - 1) - row

    def chain(xl, pos, base, forward):
        x = jnp.where(pos >= 1, pltpu.roll(xl, 1 if forward else nc - 1, 0), 0.0)
        for m in range(n_steps):
            k = 1 << m
            s = jnp.where(pos >= k, pltpu.roll(x, k if forward else nc - k, 0), 0.0)
            pv = pq_ref[base + 2 * m:base + 2 * m + 1, :]
            qv = pq_ref[base + 2 * m + 1:base + 2 * m + 2, :]
            x = x + pv * s + qv * pltpu.roll(s, S5_STATE, 1)
        return x

    xf = chain(p[:, tw:tw + sw], pos_f, 0, True)
    xb = chain(p[:, tw + sw:tw + 2 * sw], pos_b, 2 * n_steps, False)
    xin = jnp.concatenate([xf, xb], axis=1).astype(BF16)
    y_ref[...] = p[:, :tw] + jnp.dot(xin, vop_ref[...], preferred_element_type=F32)


def _s5_mix(z_s5, n_lat, mop, vop, pq, t_len, n_steps):
    n = z_s5.shape[0]
    nc = n // t_len
    g = S5_GROUPS
    tw = t_len * S5_GROUP
    u = z_s5.reshape(nc, t_len, g, S5_GROUP).transpose(2, 0, 1, 3).reshape(g, nc, tw).astype(BF16)
    y = pl.pallas_call(
        functools.partial(_s5_kernel, lat_chunks=n_lat // t_len, n_steps=n_steps, tw=tw),
        grid=(g,),
        in_specs=[pl.BlockSpec((None, nc, tw), lambda i: (i, 0, 0)),
                  pl.BlockSpec((None, tw, tw + 4 * S5_STATE), lambda i: (i, 0, 0)),
                  pl.BlockSpec((None, 4 * S5_STATE, tw), lambda i: (i, 0, 0)),
                  pl.BlockSpec((None, 4 * n_steps, 2 * S5_STATE), lambda i: (i, 0, 0))],
        out_specs=pl.BlockSpec((None, nc, tw), lambda i: (i, 0, 0)),
        out_shape=jax.ShapeDtypeStruct((g, nc, tw), F32),
        compiler_params=_cparams(("arbitrary",)),
        name="s5_mix",
    )(u, mop, vop, pq)
    return y.reshape(g, nc, t_len, S5_GROUP).transpose(1, 2, 0, 3).reshape(n, S5_WIDTH)


def _dft_tables(n_pos):
    half = n_pos // 2
    blk = min(256, half)
    kk = jnp.arange(half, dtype=I32)[:, None]
    ang = lambda prod: (2.0 * math.pi / n_pos) * (prod % n_pos).astype(F32)
    base = ang(kk * jnp.arange(blk, dtype=I32)[None, :])
    phase = ang(kk * (blk * jnp.arange(half // blk, dtype=I32))[None, :])
    bc, bs = jnp.cos(base)[:, None, :], jnp.sin(base)[:, None, :]
    pc, ps = jnp.cos(phase)[:, :, None], jnp.sin(phase)[:, :, None]
    cm = (pc * bc - ps * bs).reshape(half, half).astype(BF16)
    sm = (ps * bc + pc * bs).reshape(half, half).astype(BF16)
    return cm, sm


def _channel_tables():
    q = jnp.arange(FFT_GROUP, dtype=I32)
    ang = (2.0 * math.pi / FFT_GROUP) * ((q[:, None] * q[None, :]) % FFT_GROUP).astype(F32)
    return jnp.cos(ang).astype(BF16), jnp.sin(ang).astype(BF16)


def _chan_dft_kernel(zl_ref, zh_ref, cc_ref, sc_ref, rc_ref, rs_ref):
    w = zl_ref.shape[1]
    for hf, zref in enumerate((zl_ref, zh_ref)):
        z = zref[...].astype(BF16)
        for g in range(FFT_GROUPS):
            sl = slice(g * FFT_GROUP, (g + 1) * FFT_GROUP)
            osl = slice(hf * w + g * FFT_GROUP, hf * w + (g + 1) * FFT_GROUP)
            rc_ref[:, osl] = jnp.dot(z[:, sl], cc_ref[...], preferred_element_type=F32).astype(BF16)
            rs_ref[:, osl] = jnp.dot(z[:, sl], sc_ref[...], preferred_element_type=F32).astype(BF16)


def _posdft_kernel(cm_ref, sm_ref, alt_ref, rc_ref, rs_ref, o_ref, *, scale):
    tk = cm_ref.shape[0]
    w = o_ref.shape[2]
    cm = cm_ref[...]
    sm = sm_ref[...]
    alt = alt_ref[...]
    krow = lax.broadcasted_iota(I32, (tk, w), 0) + pl.program_id(0) * tk
    sign = jnp.where((krow & 1) == 0, 1.0, -1.0)

    def half_out(c, s):
        u = (jnp.dot(c, rc_ref[...], preferred_element_type=F32)
             - jnp.dot(s, rs_ref[...], preferred_element_type=F32))
        return (u[:, :w] + sign * u[:, w:]) * scale

    o_ref[0] = half_out(cm, sm)
    o_ref[1] = half_out(cm * alt, sm * alt)


def _ctx_dft_kernel(z_ref, cc_ref, sc_ref, cm_ref, sm_ref, o_ref, *, scale):
    z = z_ref[...].astype(BF16)
    for g in range(FFT_GROUPS):
        sl = slice(g * FFT_GROUP, (g + 1) * FFT_GROUP)
        zc = jnp.dot(z[:, sl], cc_ref[...], preferred_element_type=F32).astype(BF16)
        zs = jnp.dot(z[:, sl], sc_ref[...], preferred_element_type=F32).astype(BF16)
        o_ref[:, sl] = (jnp.dot(cm_ref[...], zc, preferred_element_type=F32)
                        - jnp.dot(sm_ref[...], zs, preferred_element_type=F32)) * scale


def _fourier_latent(z_all, n, cm, sm, cc, sc):
    w = z_all.shape[1]
    half = n // 2
    assert half % 2 == 0
    tr = min(512, half)
    nb = half // tr
    const = lambda i: (0, 0)
    rc, rs = pl.pallas_call(
        _chan_dft_kernel,
        grid=(nb,),
        in_specs=[pl.BlockSpec((tr, w), lambda i: (i, 0)),
                  pl.BlockSpec((tr, w), lambda i: (i + nb, 0)),
                  pl.BlockSpec((FFT_GROUP, FFT_GROUP), const),
                  pl.BlockSpec((FFT_GROUP, FFT_GROUP), const)],
        out_specs=[pl.BlockSpec((tr, 2 * w), lambda i: (i, 0))] * 2,
        out_shape=[jax.ShapeDtypeStruct((half, 2 * w), BF16)] * 2,
        compiler_params=_cparams(("arbitrary",)),
        name="fft_chan",
    )(z_all, z_all, cc, sc)
    tk = min(FFT_KTILE, half)
    alt = jnp.where(jnp.arange(half) % 2 == 0, 1.0, -1.0).astype(BF16).reshape(1, half)
    out = pl.pallas_call(
        functools.partial(_posdft_kernel, scale=1.0 / math.sqrt(n * FFT_GROUP)),
        grid=(half // tk,),
        in_specs=[pl.BlockSpec((tk, half), lambda i: (i, 0)),
                  pl.BlockSpec((tk, half), lambda i: (i, 0)),
                  pl.BlockSpec((1, half), const),
                  pl.BlockSpec((half, 2 * w), const, pipeline_mode=pl.Buffered(1)),
                  pl.BlockSpec((half, 2 * w), const, pipeline_mode=pl.Buffered(1))],
        out_specs=pl.BlockSpec((2, tk, w), lambda i: (0, i, 0)),
        out_shape=jax.ShapeDtypeStruct((2, half, w), F32),
        compiler_params=_cparams(("arbitrary",)),
        name="fft_posdft",
    )(cm, sm, alt, rc, rs)
    return out.reshape(n, w)


def _fourier_ctx(z, cc, sc):
    n, w = z.shape
    q = jnp.arange(n, dtype=I32)
    ang = (2.0 * math.pi / n) * ((q[:, None] * q[None, :]) % n).astype(F32)
    cm, sm = jnp.cos(ang).astype(BF16), jnp.sin(ang).astype(BF16)
    full = lambda s: pl.BlockSpec(s, lambda i: (0,) * len(s))
    return pl.pallas_call(
        functools.partial(_ctx_dft_kernel, scale=1.0 / math.sqrt(n * FFT_GROUP)),
        grid=(1,),
        in_specs=[full((n, w)), full((FFT_GROUP, FFT_GROUP)), full((FFT_GROUP, FFT_GROUP)),
                  full((n, n)), full((n, n))],
        out_specs=full((n, w)),
        out_shape=jax.ShapeDtypeStruct((n, w), F32),
        compiler_params=_cparams(("arbitrary",)),
        name="fft_ctx",
    )(z, cc, sc, cm, sm)


def _merge_kernel(h_ref, zg_ref, ys_ref, yf_ref, wglu_ref, gv_ref, sw_ref, sb_ref,
                  wg0_ref, wg1_ref, wg2_ref, bg0_ref, bg1_ref, bg2_ref,
                  wb0_ref, wb1_ref, wb2_ref, m_ref, feats):
    j = pl.program_id(1)
    tm = h_ref.shape[0]

    @pl.when(j == 0)
    def _():
        y = _gelu(ys_ref[...])
        feats[0] = (y * _sigmoid(jnp.dot(y.astype(BF16), wglu_ref[...],
                                         preferred_element_type=F32))).astype(BF16)
        feats[1] = yf_ref[...].astype(BF16)
        gz = _gelu(zg_ref[...])
        u = gz[:, :SGU_WIDTH]
        v = gz[:, SGU_WIDTH:]
        v = v * lax.rsqrt(jnp.mean(v * v, axis=-1, keepdims=True) + EPS) * gv_ref[...]
        vb = v.astype(BF16)
        for c in range(tm // CHUNK):
            rs = slice(c * CHUNK, (c + 1) * CHUNK)
            for hd in range(SGU_HEADS):
                cs = slice(hd * SGU_HEAD, (hd + 1) * SGU_HEAD)
                s = jnp.dot(sw_ref[hd], vb[rs, cs], preferred_element_type=F32) + sb_ref[hd]
                feats[2, rs, cs] = (u[rs, cs] * s).astype(BF16)

    h = h_ref[...]
    acc = None
    for k, (wg, bg, wb) in enumerate(((wg0_ref, bg0_ref, wb0_ref), (wg1_ref, bg1_ref, wb1_ref),
                                      (wg2_ref, bg2_ref, wb2_ref))):
        gate = _sigmoid(jnp.dot(h, wg[...], preferred_element_type=F32) + bg[...])
        br = jnp.dot(feats[k], wb[...], preferred_element_type=F32)
        acc = gate * br if acc is None else acc + gate * br
    m_ref[...] = acc.astype(BF16)


def _merge(h, zg, ys, yf, wglu_bf, gv, sw_bf, sb_full, wgate_bf, bgate, wbranch_bf):
    n, d = h.shape
    tm, tn = MERGE_TILE, MERGE_COLS
    nj = d // tn
    row = lambda i, j: (i, 0)
    const2 = lambda i, j: (0, 0)
    const3 = lambda i, j: (0, 0, 0)
    gate_spec = lambda k: pl.BlockSpec((d, tn), lambda i, j: (0, k * nj + j))
    bias_spec = lambda k: pl.BlockSpec((1, tn), lambda i, j: (0, k * nj + j))
    br_spec = lambda k: pl.BlockSpec((None, S5_WIDTH, tn), lambda i, j: (k, 0, j))
    return pl.pallas_call(
        _merge_kernel,
        grid=(n // tm, nj),
        in_specs=[pl.BlockSpec((tm, d), row),
                  pl.BlockSpec((tm, 2 * SGU_WIDTH), row),
                  pl.BlockSpec((tm, S5_WIDTH), row),
                  pl.BlockSpec((tm, FFT_WIDTH), row),
                  pl.BlockSpec((S5_WIDTH, S5_WIDTH), const2),
                  pl.BlockSpec((1, SGU_WIDTH), const2),
                  pl.BlockSpec((SGU_HEADS, CHUNK, CHUNK), const3),
                  pl.BlockSpec((SGU_HEADS, CHUNK, SGU_HEAD), const3),
                  gate_spec(0), gate_spec(1), gate_spec(2),
                  bias_spec(0), bias_spec(1), bias_spec(2),
                  br_spec(0), br_spec(1), br_spec(2)],
        out_specs=pl.BlockSpec((tm, tn), lambda i, j: (i, j)),
        out_shape=jax.ShapeDtypeStruct((n, d), BF16),
        scratch_shapes=[pltpu.VMEM((N_BRANCH, tm, S5_WIDTH), BF16)],
        compiler_params=_cparams(("arbitrary", "arbitrary")),
        name="gated_merge",
    )(h, zg, ys, yf, wglu_bf, gv.reshape(1, SGU_WIDTH), sw_bf, sb_full,
      wgate_bf, wgate_bf, wgate_bf, bgate, bgate, bgate, wbranch_bf, wbranch_bf, wbranch_bf)


def _outproj_kernel(x_ref, m_ref, wo_ref, g1_ref, g2_ref, sh_ref, sc_ref, rw_ref, rb_ref,
                    x1_ref, h2_ref, ti_ref, tw_ref):
    x1 = x_ref[...] + g1_ref[...] * jnp.dot(m_ref[...], wo_ref[...], preferred_element_type=F32)
    x1_ref[...] = x1
    y = x1 * lax.rsqrt(jnp.mean(x1 * x1, axis=-1, keepdims=True) + EPS) * g2_ref[...]
    h2 = y * (1.0 + sc_ref[...]) + sh_ref[...]
    h2_ref[...] = h2
    h_hi = h2.astype(BF16)
    h_lo = (h2 - h_hi.astype(F32)).astype(BF16)
    p_hi = jnp.dot(h_hi, rw_ref[...], preferred_element_type=F32)
    p_lo = jnp.dot(h_lo, rw_ref[:, :LANES], preferred_element_type=F32)
    logits = p_hi[:, :LANES] + p_hi[:, LANES:] + p_lo + rb_ref[...]
    lane = lax.broadcasted_iota(I32, logits.shape, 1)
    vals, idxs = [], []
    for _ in range(TOP_K):
        mx = jnp.max(logits, axis=-1, keepdims=True)
        am = jnp.min(jnp.where(logits == mx, lane, LANES), axis=-1, keepdims=True)
        vals.append(mx)
        idxs.append(am)
        logits = jnp.where(lane == am, -jnp.inf, logits)
    ex = [jnp.exp(v - vals[0]) for v in vals]
    den = ex[0] + ex[1] + ex[2] + ex[3]
    ti = jnp.zeros(lane.shape, I32)
    tw = jnp.zeros(lane.shape, F32)
    for k in range(TOP_K):
        ti = jnp.where(lane == k, idxs[k], ti)
        tw = jnp.where(lane == k, ex[k] / den, tw)
    ti_ref[...] = ti
    tw_ref[...] = tw


def _outproj(xa, m, n_lat, wo_bf, g1, g2n, sh2, sc2, rw_pad, rb_pad):
    n, d = xa.shape
    tm = ROW_TILE
    lat_tiles = n_lat // tm
    row = lambda i: (i, 0)
    kind = lambda i: (jnp.where(i >= lat_tiles, 1, 0), 0, 0)
    const = lambda i: (0, 0)
    return pl.pallas_call(
        _outproj_kernel,
        grid=(n // tm,),
        in_specs=[pl.BlockSpec((tm, d), row),
                  pl.BlockSpec((tm, d), row),
                  pl.BlockSpec((d, d), const),
                  pl.BlockSpec((None, 1, d), kind),
                  pl.BlockSpec((1, d), const),
                  pl.BlockSpec((None, 1, d), kind),
                  pl.BlockSpec((None, 1, d), kind),
                  pl.BlockSpec((d, 2 * LANES), const),
                  pl.BlockSpec((1, LANES), const)],
        out_specs=[pl.BlockSpec((tm, d), row), pl.BlockSpec((tm, d), row),
                   pl.BlockSpec((tm, LANES), row), pl.BlockSpec((tm, LANES), row)],
        out_shape=[jax.ShapeDtypeStruct((n, d), F32), jax.ShapeDtypeStruct((n, d), F32),
                   jax.ShapeDtypeStruct((n, LANES), I32), jax.ShapeDtypeStruct((n, LANES), F32)],
        compiler_params=_cparams(("arbitrary",)),
        name="outproj_router",
    )(xa, m, wo_bf, g1, g2n.reshape(1, d), sh2, sc2, rw_pad, rb_pad)


def _routing_tables(top_idx, n_items):
    flat_e = top_idx.reshape(-1)
    onehot = (flat_e[:, None] == jnp.arange(N_EXPERTS, dtype=I32)[None, :]).astype(I32)
    csum = jnp.cumsum(onehot, axis=0)
    rank = jnp.sum((csum - onehot) * onehot, axis=1)
    counts = csum[-1]
    padded = (counts + MOE_CH - 1) // MOE_CH * MOE_CH
    pend = jnp.cumsum(padded)
    pstart = pend - padded
    dest = jnp.sum(onehot * pstart[None, :], axis=1) + rank
    per_e = (padded + MOE_RMAX - 1) // MOE_RMAX
    iend = jnp.cumsum(per_e)
    istart = iend - per_e
    total = iend[-1]
    t = jnp.arange(n_items, dtype=I32)
    valid = t < total
    tc = jnp.minimum(t, total - 1)
    e_of = jnp.minimum(jnp.sum((tc[:, None] >= iend[None, :]).astype(I32), axis=1), N_EXPERTS - 1)
    local = tc - istart[e_of]
    row0 = pstart[e_of] + local * MOE_RMAX
    nch = jnp.clip((padded[e_of] - local * MOE_RMAX) // MOE_CH, 0, MOE_RMAX // MOE_CH)
    nch = jnp.where(valid, nch, 0)
    return (dest.astype(I32), counts.astype(I32), pstart.astype(I32), padded.astype(I32),
            e_of.astype(I32), row0.astype(I32), nch.astype(I32))


def _dispatch_kernel(cnt_ref, pst_ref, pad_ref, dest_ref, h_ref, xs_ref, zrow, sem, zsem):
    i = pl.program_id(0)
    tq = h_ref.shape[0]

    @pl.when(i == 0)
    def _():
        zrow[...] = jnp.zeros(zrow.shape, zrow.dtype)

        def per_expert(e, carry):
            first = pst_ref[e] + cnt_ref[e]
            npad = pad_ref[e] - cnt_ref[e]

            def start(r, c):
                pltpu.make_async_copy(zrow.at[pl.ds(0, 1)], xs_ref.at[pl.ds(first + r, 1)], zsem).start()
                return c

            def wait(r, c):
                pltpu.make_async_copy(zrow.at[pl.ds(0, 1)], xs_ref.at[pl.ds(first + r, 1)], zsem).wait()
                return c

            lax.fori_loop(0, npad, start, 0)
            lax.fori_loop(0, npad, wait, 0)
            return carry

        lax.fori_loop(0, N_EXPERTS, per_expert, 0)

    def start(a, c):
        t = lax.shift_right_logical(a, 2)
        pltpu.make_async_copy(h_ref.at[pl.ds(t, 1)], xs_ref.at[pl.ds(dest_ref[0, a], 1)], sem).start()
        return c

    lax.fori_loop(0, tq * TOP_K, start, 0, unroll=8)
    for _ in range(TOP_K):
        pltpu.make_async_copy(h_ref, xs_ref.at[pl.ds(0, tq)], sem).wait()


def _dispatch(h2, dest, counts, pstart, padded, n_rows):
    n, d = h2.shape
    tq = DISP_TILE
    nt = n // tq
    dest3 = dest.reshape(nt, 1, tq * TOP_K)
    return pl.pallas_call(
        _dispatch_kernel,
        grid_spec=pltpu.PrefetchScalarGridSpec(
            num_scalar_prefetch=3,
            grid=(nt,),
            in_specs=[pl.BlockSpec((None, 1, tq * TOP_K), lambda i, *_: (i, 0, 0), memory_space=pltpu.SMEM),
                      pl.BlockSpec((tq, d), lambda i, *_: (i, 0))],
            out_specs=pl.BlockSpec(memory_space=pl.ANY),
            scratch_shapes=[pltpu.VMEM((8, d), F32), pltpu.SemaphoreType.DMA(()), pltpu.SemaphoreType.DMA(())]),
        out_shape=jax.ShapeDtypeStruct((n_rows, d), F32),
        compiler_params=_cparams(("arbitrary",)),
        name="moe_dispatch",
    )(counts, pstart, padded, dest3, h2)


def _experts_kernel(ie_ref, ir_ref, in_ref, xs_ref, wg_ref, wu_ref, wd_ref, bg_ref, bu_ref, bd_ref,
                    ys_ref, xstage, xb, acc, ystage, wgb, wub, wdb, sem_in, sem_out):
    t = pl.program_id(0)
    j = pl.program_id(1)
    nf = pl.num_programs(1)
    nch = in_ref[t]
    row0 = ir_ref[t]
    ch = MOE_CH

    def hbm_rows(c):
        return pl.ds(pl.multiple_of(row0 + c * ch, ch), ch)

    def in_copy(c, slot):
        return pltpu.make_async_copy(xs_ref.at[hbm_rows(c)], xstage.at[slot], sem_in.at[slot])

    def out_copy(c, slot):
        return pltpu.make_async_copy(ystage.at[slot], ys_ref.at[hbm_rows(c)], sem_out.at[slot])

    @pl.when(jnp.logical_and(nch > 0, j == 0))
    def _():
        in_copy(0, 0).start()

        def body(c, carry):
            slot = c & 1

            @pl.when(c + 1 < nch)
            def _():
                in_copy(c + 1, 1 - slot).start()

            in_copy(c, slot).wait()
            xb[pl.ds(pl.multiple_of(c * ch, ch), ch), :] = xstage[slot].astype(BF16)
            return carry

        lax.fori_loop(0, nch, body, 0)

    @pl.when(nch > 0)
    def _():
        wgb[...] = wg_ref[...].astype(BF16)
        wub[...] = wu_ref[...].astype(BF16)
        wdb[...] = wd_ref[...].astype(BF16)

        def partial_out(c, size=ch):
            rows = pl.ds(pl.multiple_of(c * ch, ch), size)
            x = xb[rows, :]
            gate = jnp.dot(x, wgb[...], preferred_element_type=F32) + bg_ref[...]
            up = jnp.dot(x, wub[...], preferred_element_type=F32) + bu_ref[...]
            gate = jnp.minimum(gate, SWIGLU_LIMIT)
            up = jnp.clip(up, -SWIGLU_LIMIT, SWIGLU_LIMIT)
            act = gate * _sigmoid(SWIGLU_ALPHA * gate) * (up + 1.0)
            return rows, jnp.dot(act.astype(BF16), wdb[...], preferred_element_type=F32)

        def accumulate(first):
            def store(rows, part):
                if first:
                    acc[rows, :] = part
                else:
                    acc[rows, :] += part

            def body(c2, carry):
                store(*partial_out(2 * c2, 2 * ch))
                return carry

            lax.fori_loop(0, lax.shift_right_logical(nch, 1), body, 0)

            @pl.when((nch & 1) == 1)
            def _():
                store(*partial_out(nch - 1))

        @pl.when(j == 0)
        def _():
            accumulate(True)

        @pl.when(jnp.logical_and(j > 0, j < nf - 1))
        def _():
            accumulate(False)

        @pl.when(j == nf - 1)
        def _():
            def body(c, carry):
                slot = c & 1
                rows, part = partial_out(c)

                @pl.when(c >= 2)
                def _():
                    out_copy(c - 2, slot).wait()

                ystage[slot] = acc[rows, :] + part + bd_ref[...]
                out_copy(c, slot).start()
                return carry

            lax.fori_loop(0, nch, body, 0)

            @pl.when(nch >= 2)
            def _():
                out_copy(nch - 2, nch & 1).wait()

            out_copy(nch - 1, (nch - 1) & 1).wait()


def _experts(xs, layer, w_up, b_up, w_down, b_down, item_e, item_row0, item_nch):
    n_rows, d = xs.shape
    n_layers, n_exp, _, two_f = w_up.shape
    f = two_f // 2
    tf = MOE_TF
    nf = f // tf
    n_items = item_e.shape[0]
    b_up4 = b_up.reshape(n_layers, n_exp, 1, two_f)
    b_down4 = b_down.reshape(n_layers, n_exp, 1, d)

    def jj(t, j, ie, ir, inn):
        return jnp.where(inn[t] > 0, j, nf - 1)

    return pl.pallas_call(
        _experts_kernel,
        grid_spec=pltpu.PrefetchScalarGridSpec(
            num_scalar_prefetch=3,
            grid=(n_items, nf),
            in_specs=[pl.BlockSpec(memory_space=pl.ANY),
                      pl.BlockSpec((None, None, d, tf), lambda t, j, ie, ir, inn: (layer, ie[t], 0, jj(t, j, ie, ir, inn))),
                      pl.BlockSpec((None, None, d, tf), lambda t, j, ie, ir, inn: (layer, ie[t], 0, nf + jj(t, j, ie, ir, inn))),
                      pl.BlockSpec((None, None, tf, d), lambda t, j, ie, ir, inn: (layer, ie[t], jj(t, j, ie, ir, inn), 0)),
                      pl.BlockSpec((None, None, 1, tf), lambda t, j, ie, ir, inn: (layer, ie[t], 0, jj(t, j, ie, ir, inn))),
                      pl.BlockSpec((None, None, 1, tf), lambda t, j, ie, ir, inn: (layer, ie[t], 0, nf + jj(t, j, ie, ir, inn))),
                      pl.BlockSpec((None, None, 1, d), lambda t, j, ie, ir, inn: (layer, ie[t], 0, 0))],
            out_specs=pl.BlockSpec(memory_space=pl.ANY),
            scratch_shapes=[pltpu.VMEM((2, MOE_CH, d), F32),
                            pltpu.VMEM((MOE_RMAX, d), BF16),
                            pltpu.VMEM((MOE_RMAX, d), F32),
                            pltpu.VMEM((2, MOE_CH, d), F32),
                            pltpu.VMEM((d, tf), BF16),
                            pltpu.VMEM((d, tf), BF16),
                            pltpu.VMEM((tf, d), BF16),
                            pltpu.SemaphoreType.DMA((2,)),
                            pltpu.SemaphoreType.DMA((2,))]),
        out_shape=jax.ShapeDtypeStruct((n_rows, d), F32),
        compiler_params=_cparams(("arbitrary", "arbitrary")),
        name="moe_experts",
    )(item_e, item_row0, item_nch, xs, w_up, w_up, w_down, b_up4, b_up4, b_down4)


def _combine_kernel(dc_ref, dn_ref, tw_ref, x1_ref, g2_ref, fg_ref, ys_ref, o_ref, gbuf, sem, *, final):
    i = pl.program_id(0)
    n = pl.num_programs(0)
    tq = x1_ref.shape[0]

    def issue(dref, slot):
        def body(a, c):
            t = lax.shift_right_logical(a, 2)
            k = a & (TOP_K - 1)
            pltpu.make_async_copy(ys_ref.at[pl.ds(dref[0, a], 1)], gbuf.at[slot, k, pl.ds(t, 1)],
                                  sem.at[slot]).start()
            return c
        lax.fori_loop(0, tq * TOP_K, body, 0, unroll=8)

    slot = i & 1

    @pl.when(i == 0)
    def _():
        issue(dc_ref, 0)

    @pl.when(i + 1 < n)
    def _():
        issue(dn_ref, 1 - slot)

    for k in range(TOP_K):
        pltpu.make_async_copy(ys_ref.at[pl.ds(0, tq)], gbuf.at[slot, k], sem.at[slot]).wait()
    tw = tw_ref[...]
    y = tw[:, 0:1] * gbuf[slot, 0]
    for k in range(1, TOP_K):
        y = y + tw[:, k:k + 1] * gbuf[slot, k]
    x2 = x1_ref[...] + g2_ref[...] * y
    if final:
        x2 = x2 * lax.rsqrt(jnp.mean(x2 * x2, axis=-1, keepdims=True) + EPS) * fg_ref[...]
    o_ref[...] = x2


def _combine(ys, dest, top_w, x1, n_lat, g2, final_g, final):
    n, d = x1.shape
    tq = DISP_TILE
    lat_tiles = n_lat // tq
    dest3 = dest.reshape(n // tq, 1, tq * TOP_K)
    if final:
        n = n_lat
    nt = n // tq
    kind = lambda i: (jnp.where(i >= lat_tiles, 1, 0), 0, 0)
    return pl.pallas_call(
        functools.partial(_combine_kernel, final=final),
        grid=(nt,),
        in_specs=[pl.BlockSpec((None, 1, tq * TOP_K), lambda i: (i, 0, 0), memory_space=pltpu.SMEM),
                  pl.BlockSpec((None, 1, tq * TOP_K), lambda i: (jnp.minimum(i + 1, nt - 1), 0, 0),
                               memory_space=pltpu.SMEM),
                  pl.BlockSpec((tq, LANES), lambda i: (i, 0)),
                  pl.BlockSpec((tq, d), lambda i: (i, 0)),
                  pl.BlockSpec((None, 1, d), kind),
                  pl.BlockSpec((1, d), lambda i: (0, 0)),
                  pl.BlockSpec(memory_space=pl.ANY)],
        out_specs=pl.BlockSpec((tq, d), lambda i: (i, 0)),
        out_shape=jax.ShapeDtypeStruct((n, d), F32),
        scratch_shapes=[pltpu.VMEM((2, TOP_K, tq, d), F32), pltpu.SemaphoreType.DMA((2,))],
        compiler_params=_cparams(("arbitrary",)),
        name="moe_combine",
    )(dest3, dest3, top_w, x1, g2, final_g.reshape(1, d), ys)


def kernel(x, c, ctx, c_ctx, w_mod, b_mod, norm1_g, norm2_g, w_in, s5_a_re, s5_a_im, s5_log_dt,
           s5_b_re, s5_b_im, s5_c_re, s5_c_im, s5_d, s5_w_glu, sgu_norm_g, sgu_w, sgu_b, w_branch,
           w_gate, b_gate, w_out, router_w, router_b, moe_w_up, moe_b_up, moe_w_down, moe_b_down,
           final_g):
    bsz, n_lat, d = x.shape
    n_ctx = ctx.shape[1]
    assert bsz == 1
    n_layers = w_mod.shape[0]
    n = n_lat + n_ctx
    xa = jnp.concatenate([x[0], ctx[0]], axis=0).astype(F32)

    cond8 = jnp.zeros((8, d), F32).at[0].set(c[0]).at[1].set(c_ctx)
    mods = _adaln(cond8, w_mod, b_mod)[:, :2, :].reshape(n_layers, 2, N_MOD, 1, d)

    n_chunks = n // S5_T
    n_steps = max(1, (n_chunks - 1).bit_length())
    cm, sm = _dft_tables(n_lat)
    cc, sc = _channel_tables()
    n_assign = n * TOP_K
    n_rows = n_assign + N_EXPERTS * MOE_CH
    n_items = -(-n_assign // MOE_RMAX) + N_EXPERTS

    for i in range(n_layers):
        last = i == n_layers - 1
        sh1, sc1, g1, sh2, sc2, g2 = [mods[i, :, k] for k in range(N_MOD)]

        h, z_s5, z_fft, z_sgu = _inproj(xa, n_lat, norm1_g[i], sh1, sc1, w_in[i].astype(BF16))
        mop, vop, pq = _s5_operators(s5_a_re[i], s5_a_im[i], s5_log_dt[i], s5_b_re[i], s5_b_im[i],
                                     s5_c_re[i], s5_c_im[i], s5_d[i], S5_T, n_steps)
        y_s5 = _s5_mix(z_s5, n_lat, mop, vop, pq, S5_T, n_steps)
        y_fft_lat = _fourier_latent(z_fft, n_lat, cm, sm, cc, sc)
        if last:
            y_fft_ctx = jnp.zeros((n_ctx, FFT_WIDTH), F32)
        else:
            y_fft_ctx = _fourier_ctx(z_fft[n_lat:], cc, sc)
        y_fft = jnp.concatenate([y_fft_lat, y_fft_ctx], axis=0)
        sb_full = jnp.broadcast_to(sgu_b[i].astype(F32)[:, :, None], (SGU_HEADS, CHUNK, SGU_HEAD))
        m = _merge(h, z_sgu, y_s5, y_fft, s5_w_glu[i].astype(BF16), sgu_norm_g[i].astype(F32),
                   sgu_w[i].astype(BF16), sb_full, w_gate[i].astype(BF16),
                   b_gate[i].astype(F32).reshape(1, N_BRANCH * d), w_branch[i].astype(BF16))

        rw32 = jnp.zeros((d, LANES), F32).at[:, :N_EXPERTS].set(router_w[i].astype(F32))
        rw_hi = rw32.astype(BF16)
        rw_pad = jnp.concatenate([rw_hi, (rw32 - rw_hi.astype(F32)).astype(BF16)], axis=1)
        rb_pad = jnp.full((1, LANES), -1e30, F32).at[0, :N_EXPERTS].set(router_b[i].astype(F32))
        x1, h2, top_idx, top_w = _outproj(xa, m, n_lat, w_out[i].astype(BF16), g1, norm2_g[i].astype(F32),
                                          sh2, sc2, rw_pad, rb_pad)
        dest, counts, pstart, padded, item_e, item_row0, item_nch = _routing_tables(
            top_idx[:, :TOP_K], n_items)
        xs = _dispatch(h2, dest, counts, pstart, padded, n_rows)
        ys = _experts(xs, i, moe_w_up, moe_b_up, moe_w_down, moe_b_down, item_e, item_row0, item_nch)
        xa = _combine(ys, dest, top_w, x1, n_lat, g2, final_g.astype(F32), last)

    return xa.reshape(bsz, n_lat, d).astype(x.dtype)
```

```python
import functools
import math

import jax
import jax.numpy as jnp
from jax import lax
from jax.experimental import pallas as pl
from jax.experimental.pallas import tpu as pltpu

F32 = jnp.float32
BF16 = jnp.bfloat16
I32 = jnp.int32

EPS = 1e-6
S5_WIDTH = 512
S5_GROUP = 16
S5_GROUPS = 32
S5_STATE = 64
FFT_WIDTH = 512
FFT_GROUPS = 4
FFT_GROUP = 128
SGU_WIDTH = 512
SGU_HEADS = 4
SGU_HEAD = 128
CHUNK = 128
N_BRANCH = 3
N_EXPERTS = 32
TOP_K = 4
SWIGLU_LIMIT = 7.0
SWIGLU_ALPHA = 1.702
N_MOD = 6

LANES = 128
S5_PER_BLOCK = LANES // S5_GROUP
VMEM_LIMIT_BYTES = 56 * 1024 * 1024

S5_T = 16
ROW_TILE = 256
MERGE_TILE = 384
MERGE_COLS = 512
FFT_KTILE = 256
MOE_CH = 128
MOE_RMAX = 2048
MOE_TF = 256
DISP_TILE = 256


def _cparams(sem, vmem=VMEM_LIMIT_BYTES):
    return pltpu.CompilerParams(dimension_semantics=sem, vmem_limit_bytes=vmem)


def _gelu(x):
    return 0.5 * x * (1.0 + jnp.tanh(math.sqrt(2.0 / math.pi) * (x + 0.044715 * x * x * x)))


def _sigmoid(x):
    return 1.0 / (1.0 + jnp.exp(-x))


def _adaln_kernel(c_ref, w_ref, b_ref, o_ref):
    c = c_ref[...]
    s = c * _sigmoid(c)
    o_ref[...] = jnp.dot(s, w_ref[...], preferred_element_type=F32,
                         precision=lax.Precision.HIGHEST) + b_ref[...]


def _adaln(cond8, w_mod, b_mod):
    n_layers, d, nm = w_mod.shape
    tn = 1024
    return pl.pallas_call(
        _adaln_kernel,
        grid=(n_layers, nm // tn),
        in_specs=[pl.BlockSpec((8, d), lambda l, j: (0, 0)),
                  pl.BlockSpec((None, d, tn), lambda l, j: (l, 0, j)),
                  pl.BlockSpec((None, 1, tn), lambda l, j: (l, 0, j))],
        out_specs=pl.BlockSpec((None, 8, tn), lambda l, j: (l, 0, j)),
        out_shape=jax.ShapeDtypeStruct((n_layers, 8, nm), F32),
        compiler_params=_cparams(("arbitrary", "arbitrary")),
        name="adaln",
    )(cond8, w_mod, b_mod.reshape(n_layers, 1, nm))


def _inproj_kernel(x_ref, g_ref, sh_ref, sc_ref, w_ref, h_ref, zs_ref, zf_ref, zg_ref, zscr):
    x = x_ref[...]
    y = x * lax.rsqrt(jnp.mean(x * x, axis=-1, keepdims=True) + EPS) * g_ref[...]
    h = (y * (1.0 + sc_ref[...]) + sh_ref[...]).astype(BF16)
    h_ref[...] = h
    z = jnp.dot(h, w_ref[...], preferred_element_type=F32)
    zf_ref[...] = z[:, S5_WIDTH:S5_WIDTH + FFT_WIDTH]
    zg_ref[...] = z[:, S5_WIDTH + FFT_WIDTH:]
    n_chunks = zs_ref.shape[1]
    for b in range(S5_WIDTH // LANES):
        zscr[b] = z[:, b * LANES:(b + 1) * LANES]
        for j in range(S5_T):
            zs_ref[b, :, j * LANES:(j + 1) * LANES] = \
                zscr[b, pl.ds(j, n_chunks, stride=S5_T), :].astype(BF16)


def _inproj(xa, n_lat, g, sh, sc, w_in_bf):
    n, d = xa.shape
    tm = ROW_TILE
    lat_tiles = n_lat // tm
    row = lambda i: (i, 0)
    kind = lambda i: (jnp.where(i >= lat_tiles, 1, 0), 0, 0)
    return pl.pallas_call(
        _inproj_kernel,
        grid=(n // tm,),
        in_specs=[pl.BlockSpec((tm, d), row),
                  pl.BlockSpec((1, d), lambda i: (0, 0)),
                  pl.BlockSpec((None, 1, d), kind),
                  pl.BlockSpec((None, 1, d), kind),
                  pl.BlockSpec((d, w_in_bf.shape[1]), lambda i: (0, 0))],
        out_specs=[pl.BlockSpec((tm, d), row),
                   pl.BlockSpec((S5_WIDTH // LANES, tm // S5_T, S5_T * LANES), lambda i: (0, i, 0)),
                   pl.BlockSpec((tm, FFT_WIDTH), row),
                   pl.BlockSpec((tm, 2 * SGU_WIDTH), row)],
        out_shape=[jax.ShapeDtypeStruct((n, d), BF16),
                   jax.ShapeDtypeStruct((S5_WIDTH // LANES, n // S5_T, S5_T * LANES), BF16),
                   jax.ShapeDtypeStruct((n, FFT_WIDTH), F32),
                   jax.ShapeDtypeStruct((n, 2 * SGU_WIDTH), F32)],
        scratch_shapes=[pltpu.VMEM((S5_WIDTH // LANES, tm, LANES), F32)],
        compiler_params=_cparams(("arbitrary",)),
        name="inproj",
    )(xa, g.reshape(1, d), sh, sc, w_in_bf)


def _s5_operators(a_re, a_im, log_dt, b_re, b_im, c_re, c_im, d_skip, t_len, n_steps):
    lam = lax.complex(a_re.astype(F32), a_im.astype(F32))
    dt = jnp.exp(log_dt.astype(F32))[..., None]
    ldt = lam * dt
    tau = jnp.arange(t_len + 1, dtype=F32)
    pw = jnp.exp(ldt[:, :, None, :] * tau[None, None, :, None])
    a_bar = jnp.exp(ldt)
    bb = ((a_bar - 1.0) / lam)[..., None] * lax.complex(b_re.astype(F32), b_im.astype(F32))
    cc = lax.complex(c_re.astype(F32), c_im.astype(F32))
    k = jnp.real(jnp.einsum('dghp,dgtp,dgpk->dgthk', cc, pw[:, :, :t_len], bb))
    ii = jnp.arange(t_len)
    diff = ii[None, :] - ii[:, None]
    kf = k[0][:, jnp.clip(diff, 0, t_len - 1)]
    kb = k[1][:, jnp.clip(-diff, 0, t_len - 1)]
    m = jnp.where((diff >= 0)[None, :, :, None, None], kf, 0.0) + \
        jnp.where((diff <= 0)[None, :, :, None, None], kb, 0.0)
    eye_t = jnp.eye(t_len, dtype=F32)
    eye_h = jnp.eye(S5_GROUP, dtype=F32)
    dd = d_skip.astype(F32).reshape(S5_GROUPS, S5_GROUP)
    m = m + eye_t[None, :, :, None, None] * (eye_h[None] * dd[:, :, None])[:, None, None, :, :]
    g = S5_GROUPS
    tw = t_len * S5_GROUP
    m = m.transpose(0, 1, 4, 2, 3).reshape(g, tw, tw)
    wf = pw[0][:, ::-1][:, 1:, :, None] * bb[0][:, None, :, :]
    wb = pw[1][:, :t_len, :, None] * bb[1][:, None, :, :]

    def state_cols(w):
        w = w.transpose(0, 1, 3, 2)
        return jnp.concatenate([jnp.real(w), jnp.imag(w)], axis=-1).reshape(g, tw, 2 * S5_STATE)

    mop = jnp.concatenate([m, state_cols(wf), state_cols(wb)], axis=-1).astype(BF16)
    gf = cc[0][:, None, :, :] * pw[0][:, 1:, None, :]
    gb = cc[1][:, None, :, :] * pw[1][:, ::-1][:, :t_len, None, :]

    def state_rows(gm):
        gm = gm.transpose(0, 3, 1, 2).reshape(g, S5_STATE, tw)
        return jnp.concatenate([jnp.real(gm), -jnp.imag(gm)], axis=1)

    vop = jnp.concatenate([state_rows(gf), state_rows(gb)], axis=1).astype(BF16)
    steps = (t_len * (2.0 ** jnp.arange(n_steps, dtype=F32)))
    am = jnp.exp(ldt[:, :, None, :] * steps[None, None, :, None])
    pvec = jnp.concatenate([jnp.real(am), jnp.real(am)], axis=-1)
    qvec = jnp.concatenate([-jnp.imag(am), jnp.imag(am)], axis=-1)
    pq = jnp.stack([pvec, qvec], axis=3)
    pq = pq.transpose(1, 0, 2, 3, 4).reshape(g, 4 * n_steps, 2 * S5_STATE)
    return mop, vop, pq


def _s5_kernel(zs_ref, sel_ref, selt_ref, mop_ref, vop_ref, pq_ref, y_ref, *, lat_chunks, n_steps, tw):
    nc = zs_ref.shape[0]
    sw = 2 * S5_STATE
    u = jnp.dot(zs_ref[...], sel_ref[...], preferred_element_type=F32).astype(BF16)
    p = jnp.dot(u, mop_ref[...], preferred_element_type=F32)
    row = lax.broadcasted_iota(I32, (nc, sw), 0)
    pos_f = jnp.where(row >= lat_chunks, row - lat_chunks, row + (nc - lat_chunks))
    pos_b = (nc - 1) - row

    def chain(xl, pos, base, forward):
        x = jnp.where(pos >= 1, pltpu.roll(xl, 1 if forward else nc - 1, 0), 0.0)
        for m in range(n_steps):
            k = 1 << m
            s = jnp.where(pos >= k, pltpu.roll(x, k if forward else nc - k, 0), 0.0)
            pv = pq_ref[base + 2 * m:base + 2 * m + 1, :]
            qv = pq_ref[base + 2 * m + 1:base + 2 * m + 2, :]
            x = x + pv * s + qv * pltpu.roll(s, S5_STATE, 1)
        return x

    xf = chain(p[:, tw:tw + sw], pos_f, 0, True)
    xb = chain(p[:, tw + sw:tw + 2 * sw], pos_b, 2 * n_steps, False)
    xin = jnp.concatenate([xf, xb], axis=1).astype(BF16)
    y = p[:, :tw] + jnp.dot(xin, vop_ref[...], preferred_element_type=F32)
    contrib = jnp.dot(y.astype(BF16), selt_ref[...], preferred_element_type=F32)

    @pl.when(pl.program_id(0) % S5_PER_BLOCK == 0)
    def _():
        y_ref[...] = contrib

    @pl.when(pl.program_id(0) % S5_PER_BLOCK != 0)
    def _():
        y_ref[...] += contrib


def _s5_select_tables(t_len):
    r = jnp.arange(t_len * LANES, dtype=I32)[None, :, None]
    c = jnp.arange(t_len * S5_GROUP, dtype=I32)[None, None, :]
    gl = jnp.arange(S5_PER_BLOCK, dtype=I32)[:, None, None]
    sel = (r == (c // S5_GROUP) * LANES + gl * S5_GROUP + c % S5_GROUP).astype(BF16)
    return sel, sel.transpose(0, 2, 1)


def _s5_mix(zs, n_lat, sel, selt, mop, vop, pq, t_len, n_steps):
    nb, nc, bw = zs.shape
    g = S5_GROUPS
    tw = t_len * S5_GROUP
    blk = lambda i: (i // S5_PER_BLOCK, 0, 0)
    loc = lambda i: (i % S5_PER_BLOCK, 0, 0)
    return pl.pallas_call(
        functools.partial(_s5_kernel, lat_chunks=n_lat // t_len, n_steps=n_steps, tw=tw),
        grid=(g,),
        in_specs=[pl.BlockSpec((None, nc, bw), blk),
                  pl.BlockSpec((None, bw, tw), loc),
                  pl.BlockSpec((None, tw, bw), loc),
                  pl.BlockSpec((None, tw, tw + 4 * S5_STATE), lambda i: (i, 0, 0)),
                  pl.BlockSpec((None, 4 * S5_STATE, tw), lambda i: (i, 0, 0)),
                  pl.BlockSpec((None, 4 * n_steps, 2 * S5_STATE), lambda i: (i, 0, 0))],
        out_specs=pl.BlockSpec((None, nc, bw), blk),
        out_shape=jax.ShapeDtypeStruct((nb, nc, bw), F32),
        compiler_params=_cparams(("arbitrary",)),
        name="s5_mix",
    )(zs, sel, selt, mop, vop, pq)


def _dft_tables(n_pos):
    half = n_pos // 2
    blk = min(256, half)
    kk = jnp.arange(half, dtype=I32)[:, None]
    ang = lambda prod: (2.0 * math.pi / n_pos) * (prod % n_pos).astype(F32)
    base = ang(kk * jnp.arange(blk, dtype=I32)[None, :])
    phase = ang(kk * (blk * jnp.arange(half // blk, dtype=I32))[None, :])
    bc, bs = jnp.cos(base)[:, None, :], jnp.sin(base)[:, None, :]
    pc, ps = jnp.cos(phase)[:, :, None], jnp.sin(phase)[:, :, None]
    cm = (pc * bc - ps * bs).reshape(half, half).astype(BF16)
    sm = (ps * bc + pc * bs).reshape(half, half).astype(BF16)
    return cm, sm


def _channel_tables():
    q = jnp.arange(FFT_GROUP, dtype=I32)
    ang = (2.0 * math.pi / FFT_GROUP) * ((q[:, None] * q[None, :]) % FFT_GROUP).astype(F32)
    return jnp.cos(ang).astype(BF16), jnp.sin(ang).astype(BF16)


def _chan_dft_kernel(zl_ref, zh_ref, cc_ref, sc_ref, rc_ref, rs_ref):
    w = zl_ref.shape[1]
    for hf, zref in enumerate((zl_ref, zh_ref)):
        z = zref[...].astype(BF16)
        for g in range(FFT_GROUPS):
            sl = slice(g * FFT_GROUP, (g + 1) * FFT_GROUP)
            osl = slice(hf * w + g * FFT_GROUP, hf * w + (g + 1) * FFT_GROUP)
            rc_ref[:, osl] = jnp.dot(z[:, sl], cc_ref[...], preferred_element_type=F32).astype(BF16)
            rs_ref[:, osl] = jnp.dot(z[:, sl], sc_ref[...], preferred_element_type=F32).astype(BF16)


def _posdft_kernel(cm_ref, sm_ref, alt_ref, rc_ref, rs_ref, o_ref, *, scale):
    tk = cm_ref.shape[0]
    w = o_ref.shape[2]
    cm = cm_ref[...]
    sm = sm_ref[...]
    alt = alt_ref[...]
    krow = lax.broadcasted_iota(I32, (tk, w), 0) + pl.program_id(0) * tk
    sign = jnp.where((krow & 1) == 0, 1.0, -1.0)

    def half_out(c, s):
        u = (jnp.dot(c, rc_ref[...], preferred_element_type=F32)
             - jnp.dot(s, rs_ref[...], preferred_element_type=F32))
        return (u[:, :w] + sign * u[:, w:]) * scale

    o_ref[0] = half_out(cm, sm)
    o_ref[1] = half_out(cm * alt, sm * alt)


def _ctx_dft_kernel(z_ref, cc_ref, sc_ref, cm_ref, sm_ref, o_ref, *, scale):
    z = z_ref[...].astype(BF16)
    for g in range(FFT_GROUPS):
        sl = slice(g * FFT_GROUP, (g + 1) * FFT_GROUP)
        zc = jnp.dot(z[:, sl], cc_ref[...], preferred_element_type=F32).astype(BF16)
        zs = jnp.dot(z[:, sl], sc_ref[...], preferred_element_type=F32).astype(BF16)
        o_ref[:, sl] = (jnp.dot(cm_ref[...], zc, preferred_element_type=F32)
                        - jnp.dot(sm_ref[...], zs, preferred_element_type=F32)) * scale


def _fourier_latent(z_all, n, cm, sm, cc, sc):
    w = z_all.shape[1]
    half = n // 2
    assert half % 2 == 0
    tr = min(512, half)
    nb = half // tr
    const = lambda i: (0, 0)
    rc, rs = pl.pallas_call(
        _chan_dft_kernel,
        grid=(nb,),
        in_specs=[pl.BlockSpec((tr, w), lambda i: (i, 0)),
                  pl.BlockSpec((tr, w), lambda i: (i + nb, 0)),
                  pl.BlockSpec((FFT_GROUP, FFT_GROUP), const),
                  pl.BlockSpec((FFT_GROUP, FFT_GROUP), const)],
        out_specs=[pl.BlockSpec((tr, 2 * w), lambda i: (i, 0))] * 2,
        out_shape=[jax.ShapeDtypeStruct((half, 2 * w), BF16)] * 2,
        compiler_params=_cparams(("arbitrary",)),
        name="fft_chan",
    )(z_all, z_all, cc, sc)
    tk = min(FFT_KTILE, half)
    alt = jnp.where(jnp.arange(half) % 2 == 0, 1.0, -1.0).astype(BF16).reshape(1, half)
    out = pl.pallas_call(
        functools.partial(_posdft_kernel, scale=1.0 / math.sqrt(n * FFT_GROUP)),
        grid=(half // tk,),
        in_specs=[pl.BlockSpec((tk, half), lambda i: (i, 0)),
                  pl.BlockSpec((tk, half), lambda i: (i, 0)),
                  pl.BlockSpec((1, half), const),
                  pl.BlockSpec((half, 2 * w), const, pipeline_mode=pl.Buffered(1)),
                  pl.BlockSpec((half, 2 * w), const, pipeline_mode=pl.Buffered(1))],
        out_specs=pl.BlockSpec((2, tk, w), lambda i: (0, i, 0)),
        out_shape=jax.ShapeDtypeStruct((2, half, w), F32),
        compiler_params=_cparams(("arbitrary",)),
        name="fft_posdft",
    )(cm, sm, alt, rc, rs)
    return out.reshape(n, w)


def _fourier_ctx(z, cc, sc):
    n, w = z.shape
    q = jnp.arange(n, dtype=I32)
    ang = (2.0 * math.pi / n) * ((q[:, None] * q[None, :]) % n).astype(F32)
    cm, sm = jnp.cos(ang).astype(BF16), jnp.sin(ang).astype(BF16)
    full = lambda s: pl.BlockSpec(s, lambda i: (0,) * len(s))
    return pl.pallas_call(
        functools.partial(_ctx_dft_kernel, scale=1.0 / math.sqrt(n * FFT_GROUP)),
        grid=(1,),
        in_specs=[full((n, w)), full((FFT_GROUP, FFT_GROUP)), full((FFT_GROUP, FFT_GROUP)),
                  full((n, n)), full((n, n))],
        out_specs=full((n, w)),
        out_shape=jax.ShapeDtypeStruct((n, w), F32),
        compiler_params=_cparams(("arbitrary",)),
        name="fft_ctx",
    )(z, cc, sc, cm, sm)


def _merge_kernel(h_ref, zg_ref, ys_ref, yf_ref, wglu_ref, gv_ref, sw_ref, sb_ref,
                  wg0_ref, wg1_ref, wg2_ref, bg0_ref, bg1_ref, bg2_ref,
                  wb0_ref, wb1_ref, wb2_ref, m_ref, feats, yscr):
    j = pl.program_id(1)
    tm = h_ref.shape[0]

    @pl.when(j == 0)
    def _():
        n_chunks = ys_ref.shape[1]
        for b in range(S5_WIDTH // LANES):
            for t in range(S5_T):
                yscr[b, pl.ds(t, n_chunks, stride=S5_T), :] = ys_ref[b, :, t * LANES:(t + 1) * LANES]
        y = _gelu(jnp.concatenate([yscr[b] for b in range(S5_WIDTH // LANES)], axis=1))
        feats[0] = (y * _sigmoid(jnp.dot(y.astype(BF16), wglu_ref[...],
                                         preferred_element_type=F32))).astype(BF16)
        feats[1] = yf_ref[...].astype(BF16)
        gz = _gelu(zg_ref[...])
        u = gz[:, :SGU_WIDTH]
        v = gz[:, SGU_WIDTH:]
        v = v * lax.rsqrt(jnp.mean(v * v, axis=-1, keepdims=True) + EPS) * gv_ref[...]
        vb = v.astype(BF16)
        for c in range(tm // CHUNK):
            rs = slice(c * CHUNK, (c + 1) * CHUNK)
            for hd in range(SGU_HEADS):
                cs = slice(hd * SGU_HEAD, (hd + 1) * SGU_HEAD)
                s = jnp.dot(sw_ref[hd], vb[rs, cs], preferred_element_type=F32) + sb_ref[hd]
                feats[2, rs, cs] = (u[rs, cs] * s).astype(BF16)

    h = h_ref[...]
    acc = None
    for k, (wg, bg, wb) in enumerate(((wg0_ref, bg0_ref, wb0_ref), (wg1_ref, bg1_ref, wb1_ref),
                                      (wg2_ref, bg2_ref, wb2_ref))):
        gate = _sigmoid(jnp.dot(h, wg[...], preferred_element_type=F32) + bg[...])
        br = jnp.dot(feats[k], wb[...], preferred_element_type=F32)
        acc = gate * br if acc is None else acc + gate * br
    m_ref[...] = acc.astype(BF16)


def _merge(h, zg, ys, yf, wglu_bf, gv, sw_bf, sb_full, wgate_bf, bgate, wbranch_bf):
    n, d = h.shape
    tm, tn = MERGE_TILE, MERGE_COLS
    nj = d // tn
    row = lambda i, j: (i, 0)
    const2 = lambda i, j: (0, 0)
    const3 = lambda i, j: (0, 0, 0)
    gate_spec = lambda k: pl.BlockSpec((d, tn), lambda i, j: (0, k * nj + j))
    bias_spec = lambda k: pl.BlockSpec((1, tn), lambda i, j: (0, k * nj + j))
    br_spec = lambda k: pl.BlockSpec((None, S5_WIDTH, tn), lambda i, j: (k, 0, j))
    return pl.pallas_call(
        _merge_kernel,
        grid=(n // tm, nj),
        in_specs=[pl.BlockSpec((tm, d), row),
                  pl.BlockSpec((tm, 2 * SGU_WIDTH), row),
                  pl.BlockSpec((S5_WIDTH // LANES, tm // S5_T, S5_T * LANES), lambda i, j: (0, i, 0)),
                  pl.BlockSpec((tm, FFT_WIDTH), row),
                  pl.BlockSpec((S5_WIDTH, S5_WIDTH), const2),
                  pl.BlockSpec((1, SGU_WIDTH), const2),
                  pl.BlockSpec((SGU_HEADS, CHUNK, CHUNK), const3),
                  pl.BlockSpec((SGU_HEADS, CHUNK, SGU_HEAD), const3),
                  gate_spec(0), gate_spec(1), gate_spec(2),
                  bias_spec(0), bias_spec(1), bias_spec(2),
                  br_spec(0), br_spec(1), br_spec(2)],
        out_specs=pl.BlockSpec((tm, tn), lambda i, j: (i, j)),
        out_shape=jax.ShapeDtypeStruct((n, d), BF16),
        scratch_shapes=[pltpu.VMEM((N_BRANCH, tm, S5_WIDTH), BF16),
                        pltpu.VMEM((S5_WIDTH // LANES, tm, LANES), F32)],
        compiler_params=_cparams(("arbitrary", "arbitrary")),
        name="gated_merge",
    )(h, zg, ys, yf, wglu_bf, gv.reshape(1, SGU_WIDTH), sw_bf, sb_full,
      wgate_bf, wgate_bf, wgate_bf, bgate, bgate, bgate, wbranch_bf, wbranch_bf, wbranch_bf)


def _outproj_kernel(x_ref, m_ref, wo_ref, g1_ref, g2_ref, sh_ref, sc_ref, rw_ref, rb_ref,
                    x1_ref, h2_ref, ti_ref, tw_ref):
    x1 = x_ref[...] + g1_ref[...] * jnp.dot(m_ref[...], wo_ref[...], preferred_element_type=F32)
    x1_ref[...] = x1
    y = x1 * lax.rsqrt(jnp.mean(x1 * x1, axis=-1, keepdims=True) + EPS) * g2_ref[...]
    h2 = y * (1.0 + sc_ref[...]) + sh_ref[...]
    h2_ref[...] = h2
    h_hi = h2.astype(BF16)
    h_lo = (h2 - h_hi.astype(F32)).astype(BF16)
    p_hi = jnp.dot(h_hi, rw_ref[...], preferred_element_type=F32)
    p_lo = jnp.dot(h_lo, rw_ref[:, :LANES], preferred_element_type=F32)
    logits = p_hi[:, :LANES] + p_hi[:, LANES:] + p_lo + rb_ref[...]
    lane = lax.broadcasted_iota(I32, logits.shape, 1)
    vals, idxs = [], []
    for _ in range(TOP_K):
        mx = jnp.max(logits, axis=-1, keepdims=True)
        am = jnp.min(jnp.where(logits == mx, lane, LANES), axis=-1, keepdims=True)
        vals.append(mx)
        idxs.append(am)
        logits = jnp.where(lane == am, -jnp.inf, logits)
    ex = [jnp.exp(v - vals[0]) for v in vals]
    den = ex[0] + ex[1] + ex[2] + ex[3]
    ti = jnp.zeros(lane.shape, I32)
    tw = jnp.zeros(lane.shape, F32)
    for k in range(TOP_K):
        ti = jnp.where(lane == k, idxs[k], ti)
        tw = jnp.where(lane == k, ex[k] / den, tw)
    ti_ref[...] = ti
    tw_ref[...] = tw


def _outproj(xa, m, n_lat, wo_bf, g1, g2n, sh2, sc2, rw_pad, rb_pad):
    n, d = xa.shape
    tm = ROW_TILE
    lat_tiles = n_lat // tm
    row = lambda i: (i, 0)
    kind = lambda i: (jnp.where(i >= lat_tiles, 1, 0), 0, 0)
    const = lambda i: (0, 0)
    return pl.pallas_call(
        _outproj_kernel,
        grid=(n // tm,),
        in_specs=[pl.BlockSpec((tm, d), row),
                  pl.BlockSpec((tm, d), row),
                  pl.BlockSpec((d, d), const),
                  pl.BlockSpec((None, 1, d), kind),
                  pl.BlockSpec((1, d), const),
                  pl.BlockSpec((None, 1, d), kind),
                  pl.BlockSpec((None, 1, d), kind),
                  pl.BlockSpec((d, 2 * LANES), const),
                  pl.BlockSpec((1, LANES), const)],
        out_specs=[pl.BlockSpec((tm, d), row), pl.BlockSpec((tm, d), row),
                   pl.BlockSpec((tm, LANES), row), pl.BlockSpec((tm, LANES), row)],
        out_shape=[jax.ShapeDtypeStruct((n, d), F32), jax.ShapeDtypeStruct((n, d), F32),
                   jax.ShapeDtypeStruct((n, LANES), I32), jax.ShapeDtypeStruct((n, LANES), F32)],
        compiler_params=_cparams(("arbitrary",)),
        name="outproj_router",
    )(xa, m, wo_bf, g1, g2n.reshape(1, d), sh2, sc2, rw_pad, rb_pad)


def _routing_tables(top_idx, n_items):
    flat_e = top_idx.reshape(-1)
    onehot = (flat_e[:, None] == jnp.arange(N_EXPERTS, dtype=I32)[None, :]).astype(I32)
    csum = jnp.cumsum(onehot, axis=0)
    rank = jnp.sum((csum - onehot) * onehot, axis=1)
    counts = csum[-1]
    padded = (counts + MOE_CH - 1) // MOE_CH * MOE_CH
    pend = jnp.cumsum(padded)
    pstart = pend - padded
    dest = jnp.sum(onehot * pstart[None, :], axis=1) + rank
    per_e = (padded + MOE_RMAX - 1) // MOE_RMAX
    iend = jnp.cumsum(per_e)
    istart = iend - per_e
    total = iend[-1]
    t = jnp.arange(n_items, dtype=I32)
    valid = t < total
    tc = jnp.minimum(t, total - 1)
    e_of = jnp.minimum(jnp.sum((tc[:, None] >= iend[None, :]).astype(I32), axis=1), N_EXPERTS - 1)
    local = tc - istart[e_of]
    row0 = pstart[e_of] + local * MOE_RMAX
    nch = jnp.clip((padded[e_of] - local * MOE_RMAX) // MOE_CH, 0, MOE_RMAX // MOE_CH)
    nch = jnp.where(valid, nch, 0)
    return (dest.astype(I32), counts.astype(I32), pstart.astype(I32), padded.astype(I32),
            e_of.astype(I32), row0.astype(I32), nch.astype(I32))


def _dispatch_kernel(cnt_ref, pst_ref, pad_ref, dest_ref, h_ref, xs_ref, zrow, sem, zsem):
    i = pl.program_id(0)
    tq = h_ref.shape[0]

    @pl.when(i == 0)
    def _():
        zrow[...] = jnp.zeros(zrow.shape, zrow.dtype)

        def per_expert(e, carry):
            first = pst_ref[e] + cnt_ref[e]
            npad = pad_ref[e] - cnt_ref[e]

            def start(r, c):
                pltpu.make_async_copy(zrow.at[pl.ds(0, 1)], xs_ref.at[pl.ds(first + r, 1)], zsem).start()
                return c

            def wait(r, c):
                pltpu.make_async_copy(zrow.at[pl.ds(0, 1)], xs_ref.at[pl.ds(first + r, 1)], zsem).wait()
                return c

            lax.fori_loop(0, npad, start, 0)
            lax.fori_loop(0, npad, wait, 0)
            return carry

        lax.fori_loop(0, N_EXPERTS, per_expert, 0)

    def start(t, c):
        src = h_ref.at[pl.ds(t, 1)]
        for k in range(TOP_K):
            pltpu.make_async_copy(src, xs_ref.at[pl.ds(dest_ref[0, t * TOP_K + k], 1)], sem).start()
        return c

    lax.fori_loop(0, tq, start, 0, unroll=2)
    for _ in range(TOP_K):
        pltpu.make_async_copy(h_ref, xs_ref.at[pl.ds(0, tq)], sem).wait()


def _dispatch(h2, dest, counts, pstart, padded, n_rows):
    n, d = h2.shape
    tq = DISP_TILE
    nt = n // tq
    dest3 = dest.reshape(nt, 1, tq * TOP_K)
    return pl.pallas_call(
        _dispatch_kernel,
        grid_spec=pltpu.PrefetchScalarGridSpec(
            num_scalar_prefetch=3,
            grid=(nt,),
            in_specs=[pl.BlockSpec((None, 1, tq * TOP_K), lambda i, *_: (i, 0, 0), memory_space=pltpu.SMEM),
                      pl.BlockSpec((tq, d), lambda i, *_: (i, 0))],
            out_specs=pl.BlockSpec(memory_space=pl.ANY),
            scratch_shapes=[pltpu.VMEM((8, d), F32), pltpu.SemaphoreType.DMA(()), pltpu.SemaphoreType.DMA(())]),
        out_shape=jax.ShapeDtypeStruct((n_rows, d), F32),
        compiler_params=_cparams(("arbitrary",)),
        name="moe_dispatch",
    )(counts, pstart, padded, dest3, h2)


def _experts_kernel(ie_ref, ir_ref, in_ref, xs_ref, wg_ref, wu_ref, wd_ref, bg_ref, bu_ref,
                    ys_ref, xstage, xb, acc, pending, sem_in, sem_out):
    t = pl.program_id(0)
    j = pl.program_id(1)
    nf = pl.num_programs(1)
    nch = in_ref[t]
    row0 = ir_ref[t]
    ch = MOE_CH

    def hbm_rows(c):
        return pl.ds(pl.multiple_of(row0 + c * ch, ch), ch)

    def vmem_rows(c, size=ch):
        return pl.ds(pl.multiple_of(c * ch, ch), size)

    def in_copy(c, slot):
        return pltpu.make_async_copy(xs_ref.at[hbm_rows(c)], xstage.at[slot], sem_in.at[slot])

    def out_copy(c):
        return pltpu.make_async_copy(acc.at[vmem_rows(c)], ys_ref.at[hbm_rows(c)], sem_out)

    def drain():
        def body(c, carry):
            out_copy(0).wait()
            return carry
        lax.fori_loop(0, pending[0], body, 0)
        pending[0] = 0

    @pl.when(jnp.logical_and(t == 0, j == 0))
    def _():
        pending[0] = 0

    @pl.when(jnp.logical_and(nch > 0, j == 0))
    def _():
        in_copy(0, 0).start()

        def body(c, carry):
            slot = c & 1

            @pl.when(c + 1 < nch)
            def _():
                in_copy(c + 1, 1 - slot).start()

            in_copy(c, slot).wait()
            xb[vmem_rows(c), :] = xstage[slot].astype(BF16)
            return carry

        lax.fori_loop(0, nch, body, 0)

    @pl.when(nch > 0)
    def _():
        def partial_out(c, size):
            rows = vmem_rows(c, size)
            x = xb[rows, :]
            gate = jnp.dot(x, wg_ref[...].astype(BF16), preferred_element_type=F32) + bg_ref[...]
            up = jnp.dot(x, wu_ref[...].astype(BF16), preferred_element_type=F32) + bu_ref[...]
            gate = jnp.minimum(gate, SWIGLU_LIMIT)
            up = jnp.clip(up, -SWIGLU_LIMIT, SWIGLU_LIMIT)
            act = gate * _sigmoid(SWIGLU_ALPHA * gate) * (up + 1.0)
            return rows, jnp.dot(act.astype(BF16), wd_ref[...].astype(BF16), preferred_element_type=F32)

        def accumulate(first):
            def store(rows, part):
                if first:
                    acc[rows, :] = part
                else:
                    acc[rows, :] += part

            def body(c4, carry):
                store(*partial_out(4 * c4, 4 * ch))
                return carry

            quads = lax.shift_right_logical(nch, 2)
            lax.fori_loop(0, quads, body, 0)

            @pl.when((nch & 2) != 0)
            def _():
                store(*partial_out(4 * quads, 2 * ch))

            @pl.when((nch & 1) != 0)
            def _():
                store(*partial_out(nch - 1, ch))

        @pl.when(j == 0)
        def _():
            drain()
            accumulate(True)

        @pl.when(j > 0)
        def _():
            accumulate(False)

        @pl.when(j == nf - 1)
        def _():
            def body(c, carry):
                out_copy(c).start()
                return carry
            lax.fori_loop(0, nch, body, 0)
            pending[0] = nch

    @pl.when(jnp.logical_and(t == pl.num_programs(0) - 1, j == nf - 1))
    def _():
        drain()


def _experts(xs, layer, w_up, b_up, w_down, item_e, item_row0, item_nch):
    n_rows, d = xs.shape
    n_layers, n_exp, _, two_f = w_up.shape
    f = two_f // 2
    tf = MOE_TF
    nf = f // tf
    assert nf >= 2
    n_items = item_e.shape[0]
    b_up4 = b_up.reshape(n_layers, n_exp, 1, two_f)

    def jj(t, j, ie, ir, inn):
        return jnp.where(inn[t] > 0, j, nf - 1)

    return pl.pallas_call(
        _experts_kernel,
        grid_spec=pltpu.PrefetchScalarGridSpec(
            num_scalar_prefetch=3,
            grid=(n_items, nf),
            in_specs=[pl.BlockSpec(memory_space=pl.ANY),
                      pl.BlockSpec((None, None, d, tf), lambda t, j, ie, ir, inn: (layer, ie[t], 0, jj(t, j, ie, ir, inn))),
                      pl.BlockSpec((None, None, d, tf), lambda t, j, ie, ir, inn: (layer, ie[t], 0, nf + jj(t, j, ie, ir, inn))),
                      pl.BlockSpec((None, None, tf, d), lambda t, j, ie, ir, inn: (layer, ie[t], jj(t, j, ie, ir, inn), 0)),
                      pl.BlockSpec((None, None, 1, tf), lambda t, j, ie, ir, inn: (layer, ie[t], 0, jj(t, j, ie, ir, inn))),
                      pl.BlockSpec((None, None, 1, tf), lambda t, j, ie, ir, inn: (layer, ie[t], 0, nf + jj(t, j, ie, ir, inn)))],
            out_specs=pl.BlockSpec(memory_space=pl.ANY),
            scratch_shapes=[pltpu.VMEM((2, MOE_CH, d), F32),
                            pltpu.VMEM((MOE_RMAX, d), BF16),
                            pltpu.VMEM((MOE_RMAX, d), F32),
                            pltpu.SMEM((1,), I32),
                            pltpu.SemaphoreType.DMA((2,)),
                            pltpu.SemaphoreType.DMA(())]),
        out_shape=jax.ShapeDtypeStruct((n_rows, d), F32),
        compiler_params=_cparams(("arbitrary", "arbitrary")),
        name="moe_experts",
    )(item_e, item_row0, item_nch, xs, w_up, w_up, w_down, b_up4, b_up4)


def _combine_kernel(dc_ref, dn_ref, tw_ref, ti_ref, x1_ref, g2_ref, fg_ref, bd_ref, ys_ref, o_ref,
                    gbuf, sem, *, final):
    i = pl.program_id(0)
    n = pl.num_programs(0)
    tq = x1_ref.shape[0]

    def issue(dref, slot):
        def body(t, c):
            for k in range(TOP_K):
                pltpu.make_async_copy(ys_ref.at[pl.ds(dref[0, t * TOP_K + k], 1)],
                                      gbuf.at[slot, k, pl.ds(t, 1)], sem.at[slot]).start()
            return c
        lax.fori_loop(0, tq, body, 0, unroll=2)

    slot = i & 1

    @pl.when(i == 0)
    def _():
        issue(dc_ref, 0)

    @pl.when(i + 1 < n)
    def _():
        issue(dn_ref, 1 - slot)

    for k in range(TOP_K):
        pltpu.make_async_copy(ys_ref.at[pl.ds(0, tq)], gbuf.at[slot, k], sem.at[slot]).wait()
    tw = tw_ref[...]
    ti = ti_ref[...]
    lane = lax.broadcasted_iota(I32, tw.shape, 1)
    ew = jnp.zeros(tw.shape, F32)
    for k in range(TOP_K):
        ew = ew + jnp.where(lane == ti[:, k:k + 1], tw[:, k:k + 1], 0.0)
    y = jnp.dot(ew.astype(BF16), bd_ref[...], preferred_element_type=F32)
    for k in range(TOP_K):
        y = y + tw[:, k:k + 1] * gbuf[slot, k]
    x2 = x1_ref[...] + g2_ref[...] * y
    if final:
        x2 = x2 * lax.rsqrt(jnp.mean(x2 * x2, axis=-1, keepdims=True) + EPS) * fg_ref[...]
    o_ref[...] = x2


def _combine(ys, dest, top_w, top_idx, b_down, x1, n_lat, g2, final_g, final):
    n, d = x1.shape
    bd_pad = jnp.zeros((LANES, d), BF16).at[:N_EXPERTS].set(b_down.astype(BF16))
    tq = DISP_TILE
    lat_tiles = n_lat // tq
    dest3 = dest.reshape(n // tq, 1, tq * TOP_K)
    if final:
        n = n_lat
    nt = n // tq
    kind = lambda i: (jnp.where(i >= lat_tiles, 1, 0), 0, 0)
    return pl.pallas_call(
        functools.partial(_combine_kernel, final=final),
        grid=(nt,),
        in_specs=[pl.BlockSpec((None, 1, tq * TOP_K), lambda i: (i, 0, 0), memory_space=pltpu.SMEM),
                  pl.BlockSpec((None, 1, tq * TOP_K), lambda i: (jnp.minimum(i + 1, nt - 1), 0, 0),
                               memory_space=pltpu.SMEM),
                  pl.BlockSpec((tq, LANES), lambda i: (i, 0)),
                  pl.BlockSpec((tq, LANES), lambda i: (i, 0)),
                  pl.BlockSpec((tq, d), lambda i: (i, 0)),
                  pl.BlockSpec((None, 1, d), kind),
                  pl.BlockSpec((1, d), lambda i: (0, 0)),
                  pl.BlockSpec((LANES, d), lambda i: (0, 0)),
                  pl.BlockSpec(memory_space=pl.ANY)],
        out_specs=pl.BlockSpec((tq, d), lambda i: (i, 0)),
        out_shape=jax.ShapeDtypeStruct((n, d), F32),
        scratch_shapes=[pltpu.VMEM((2, TOP_K, tq, d), F32), pltpu.SemaphoreType.DMA((2,))],
        compiler_params=_cparams(("arbitrary",)),
        name="moe_combine",
    )(dest3, dest3, top_w, top_idx, x1, g2, final_g.reshape(1, d), bd_pad, ys)


def kernel(x, c, ctx, c_ctx, w_mod, b_mod, norm1_g, norm2_g, w_in, s5_a_re, s5_a_im, s5_log_dt,
           s5_b_re, s5_b_im, s5_c_re, s5_c_im, s5_d, s5_w_glu, sgu_norm_g, sgu_w, sgu_b, w_branch,
           w_gate, b_gate, w_out, router_w, router_b, moe_w_up, moe_b_up, moe_w_down, moe_b_down,
           final_g):
    bsz, n_lat, d = x.shape
    n_ctx = ctx.shape[1]
    assert bsz == 1
    n_layers = w_mod.shape[0]
    n = n_lat + n_ctx
    xa = jnp.concatenate([x[0], ctx[0]], axis=0).astype(F32)

    cond8 = jnp.zeros((8, d), F32).at[0].set(c[0]).at[1].set(c_ctx)
    mods = _adaln(cond8, w_mod, b_mod)[:, :2, :].reshape(n_layers, 2, N_MOD, 1, d)

    n_chunks = n // S5_T
    n_steps = max(1, (n_chunks - 1).bit_length())
    cm, sm = _dft_tables(n_lat)
    cc, sc = _channel_tables()
    sel, selt = _s5_select_tables(S5_T)
    n_assign = n * TOP_K
    n_rows = n_assign + N_EXPERTS * MOE_CH
    n_items = -(-n_assign // MOE_RMAX) + N_EXPERTS

    for i in range(n_layers):
        last = i == n_layers - 1
        sh1, sc1, g1, sh2, sc2, g2 = [mods[i, :, k] for k in range(N_MOD)]

        h, z_s5, z_fft, z_sgu = _inproj(xa, n_lat, norm1_g[i], sh1, sc1, w_in[i].astype(BF16))
        mop, vop, pq = _s5_operators(s5_a_re[i], s5_a_im[i], s5_log_dt[i], s5_b_re[i], s5_b_im[i],
                                     s5_c_re[i], s5_c_im[i], s5_d[i], S5_T, n_steps)
        y_s5 = _s5_mix(z_s5, n_lat, sel, selt, mop, vop, pq, S5_T, n_steps)
        y_fft_lat = _fourier_latent(z_fft, n_lat, cm, sm, cc, sc)
        if last:
            y_fft_ctx = jnp.zeros((n_ctx, FFT_WIDTH), F32)
        else:
            y_fft_ctx = _fourier_ctx(z_fft[n_lat:], cc, sc)
        y_fft = jnp.concatenate([y_fft_lat, y_fft_ctx], axis=0)
        sb_full = jnp.broadcast_to(sgu_b[i].astype(F32)[:, :, None], (SGU_HEADS, CHUNK, SGU_HEAD))
        m = _merge(h, z_sgu, y_s5, y_fft, s5_w_glu[i].astype(BF16), sgu_norm_g[i].astype(F32),
                   sgu_w[i].astype(BF16), sb_full, w_gate[i].astype(BF16),
                   b_gate[i].astype(F32).reshape(1, N_BRANCH * d), w_branch[i].astype(BF16))

        rw32 = jnp.zeros((d, LANES), F32).at[:, :N_EXPERTS].set(router_w[i].astype(F32))
        rw_hi = rw32.astype(BF16)
        rw_pad = jnp.concatenate([rw_hi, (rw32 - rw_hi.astype(F32)).astype(BF16)], axis=1)
        rb_pad = jnp.full((1, LANES), -1e30, F32).at[0, :N_EXPERTS].set(router_b[i].astype(F32))
        x1, h2, top_idx, top_w = _outproj(xa, m, n_lat, w_out[i].astype(BF16), g1, norm2_g[i].astype(F32),
                                          sh2, sc2, rw_pad, rb_pad)
        dest, counts, pstart, padded, item_e, item_row0, item_nch = _routing_tables(
            top_idx[:, :TOP_K], n_items)
        xs = _dispatch(h2, dest, counts, pstart, padded, n_rows)
        ys = _experts(xs, i, moe_w_up, moe_b_up, moe_w_down, item_e, item_row0, item_nch)
        xa = _combine(ys, dest, top_w, top_idx, moe_b_down[i], x1, n_lat, g2, final_g.astype(F32), last)

    return xa.reshape(bsz, n_lat, d).astype(x.dtype)
```

```python
import functools
import math

import jax
import jax.numpy as jnp
from jax import lax
from jax.experimental import pallas as pl
from jax.experimental.pallas import tpu as pltpu

F32 = jnp.float32
BF16 = jnp.bfloat16
I32 = jnp.int32

EPS = 1e-6
S5_WIDTH = 512
S5_GROUP = 16
S5_GROUPS = 32
S5_STATE = 64
FFT_WIDTH = 512
FFT_GROUPS = 4
FFT_GROUP = 128
SGU_WIDTH = 512
SGU_HEADS = 4
SGU_HEAD = 128
CHUNK = 128
N_BRANCH = 3
N_EXPERTS = 32
TOP_K = 4
SWIGLU_LIMIT = 7.0
SWIGLU_ALPHA = 1.702
N_MOD = 6

LANES = 128
S5_PER_BLOCK = LANES // S5_GROUP
VMEM_LIMIT_BYTES = 56 * 1024 * 1024

S5_T = 16
ROW_TILE = 256
MERGE_TILE = 384
MERGE_COLS = 512
FFT_KTILE = 256
DFT_BLK = 256
MOE_CH = 128
MOE_RMAX = 1536
MOE_TF = 512
DISP_TILE = 256


def _cparams(sem, vmem=VMEM_LIMIT_BYTES):
    return pltpu.CompilerParams(dimension_semantics=sem, vmem_limit_bytes=vmem)


def _gelu(x):
    return 0.5 * x * (1.0 + jnp.tanh(math.sqrt(2.0 / math.pi) * (x + 0.044715 * x * x * x)))


def _sigmoid(x):
    return 1.0 / (1.0 + jnp.exp(-x))


def _adaln_kernel(c_ref, w_ref, b_ref, o_ref):
    c = c_ref[...]
    s = c * _sigmoid(c)
    o_ref[...] = jnp.dot(s, w_ref[...], preferred_element_type=F32,
                         precision=lax.Precision.HIGHEST) + b_ref[...]


def _adaln(cond8, w_mod, b_mod):
    n_layers, d, nm = w_mod.shape
    tn = 1024
    return pl.pallas_call(
        _adaln_kernel,
        grid=(n_layers, nm // tn),
        in_specs=[pl.BlockSpec((8, d), lambda l, j: (0, 0)),
                  pl.BlockSpec((None, d, tn), lambda l, j: (l, 0, j)),
                  pl.BlockSpec((None, 1, tn), lambda l, j: (l, 0, j))],
        out_specs=pl.BlockSpec((None, 8, tn), lambda l, j: (l, 0, j)),
        out_shape=jax.ShapeDtypeStruct((n_layers, 8, nm), F32),
        compiler_params=_cparams(("arbitrary", "arbitrary")),
        name="adaln",
    )(cond8, w_mod, b_mod.reshape(n_layers, 1, nm))


def _inproj_kernel(x_ref, g_ref, sh_ref, sc_ref, w_ref, h_ref, zs_ref, zf_ref, zg_ref, zscr):
    x = x_ref[...]
    y = x * lax.rsqrt(jnp.mean(x * x, axis=-1, keepdims=True) + EPS) * g_ref[...]
    h = (y * (1.0 + sc_ref[...]) + sh_ref[...]).astype(BF16)
    h_ref[...] = h
    z = jnp.dot(h, w_ref[...], preferred_element_type=F32)
    zf_ref[...] = z[:, S5_WIDTH:S5_WIDTH + FFT_WIDTH]
    zg_ref[...] = z[:, S5_WIDTH + FFT_WIDTH:]
    n_chunks = zs_ref.shape[1]
    for b in range(S5_WIDTH // LANES):
        zscr[b] = z[:, b * LANES:(b + 1) * LANES]
        for j in range(S5_T):
            zs_ref[b, :, j * LANES:(j + 1) * LANES] = \
                zscr[b, pl.ds(j, n_chunks, stride=S5_T), :].astype(BF16)


def _inproj(xa, n_lat, g, sh, sc, w_in_bf):
    n, d = xa.shape
    tm = ROW_TILE
    lat_tiles = n_lat // tm
    row = lambda i: (i, 0)
    kind = lambda i: (jnp.where(i >= lat_tiles, 1, 0), 0, 0)
    return pl.pallas_call(
        _inproj_kernel,
        grid=(n // tm,),
        in_specs=[pl.BlockSpec((tm, d), row),
                  pl.BlockSpec((1, d), lambda i: (0, 0)),
                  pl.BlockSpec((None, 1, d), kind),
                  pl.BlockSpec((None, 1, d), kind),
                  pl.BlockSpec((d, w_in_bf.shape[1]), lambda i: (0, 0))],
        out_specs=[pl.BlockSpec((tm, d), row),
                   pl.BlockSpec((S5_WIDTH // LANES, tm // S5_T, S5_T * LANES), lambda i: (0, i, 0)),
                   pl.BlockSpec((tm, FFT_WIDTH), row),
                   pl.BlockSpec((tm, 2 * SGU_WIDTH), row)],
        out_shape=[jax.ShapeDtypeStruct((n, d), BF16),
                   jax.ShapeDtypeStruct((S5_WIDTH // LANES, n // S5_T, S5_T * LANES), BF16),
                   jax.ShapeDtypeStruct((n, FFT_WIDTH), F32),
                   jax.ShapeDtypeStruct((n, 2 * SGU_WIDTH), F32)],
        scratch_shapes=[pltpu.VMEM((S5_WIDTH // LANES, tm, LANES), F32)],
        compiler_params=_cparams(("arbitrary",)),
        name="inproj",
    )(xa, g.reshape(1, d), sh, sc, w_in_bf)


def _s5_operators(a_re, a_im, log_dt, b_re, b_im, c_re, c_im, d_skip, t_len, n_steps):
    lam = lax.complex(a_re.astype(F32), a_im.astype(F32))
    dt = jnp.exp(log_dt.astype(F32))[..., None]
    ldt = lam * dt
    tau = jnp.arange(t_len + 1, dtype=F32)
    pw = jnp.exp(ldt[:, :, None, :] * tau[None, None, :, None])
    a_bar = jnp.exp(ldt)
    bb = ((a_bar - 1.0) / lam)[..., None] * lax.complex(b_re.astype(F32), b_im.astype(F32))
    cc = lax.complex(c_re.astype(F32), c_im.astype(F32))
    k = jnp.real(jnp.einsum('dghp,dgtp,dgpk->dgthk', cc, pw[:, :, :t_len], bb))
    ii = jnp.arange(t_len)
    diff = ii[None, :] - ii[:, None]
    lag = jnp.arange(t_len)[:, None, None]
    place = lambda e: jnp.einsum('gthk,tji->gjihk', e[0], e[1].astype(F32), precision=lax.Precision.HIGHEST)
    m = place((k[0], diff[None] == lag)) + place((k[1], -diff[None] == lag))
    eye_t = jnp.eye(t_len, dtype=F32)
    eye_h = jnp.eye(S5_GROUP, dtype=F32)
    dd = d_skip.astype(F32).reshape(S5_GROUPS, S5_GROUP)
    m = m + eye_t[None, :, :, None, None] * (eye_h[None] * dd[:, :, None])[:, None, None, :, :]
    g = S5_GROUPS
    tw = t_len * S5_GROUP
    m = m.transpose(0, 1, 4, 2, 3).reshape(g, tw, tw)
    wf = pw[0][:, ::-1][:, 1:, :, None] * bb[0][:, None, :, :]
    wb = pw[1][:, :t_len, :, None] * bb[1][:, None, :, :]

    def state_cols(w):
        w = w.transpose(0, 1, 3, 2)
        return jnp.concatenate([jnp.real(w), jnp.imag(w)], axis=-1).reshape(g, tw, 2 * S5_STATE)

    mop = jnp.concatenate([m, state_cols(wf), state_cols(wb)], axis=-1).astype(BF16)
    gf = cc[0][:, None, :, :] * pw[0][:, 1:, None, :]
    gb = cc[1][:, None, :, :] * pw[1][:, ::-1][:, :t_len, None, :]

    def state_rows(gm):
        gm = gm.transpose(0, 3, 1, 2).reshape(g, S5_STATE, tw)
        return jnp.concatenate([jnp.real(gm), -jnp.imag(gm)], axis=1)

    vop = jnp.concatenate([state_rows(gf), state_rows(gb)], axis=1).astype(BF16)
    steps = (t_len * (2.0 ** jnp.arange(n_steps, dtype=F32)))
    am = jnp.exp(ldt[:, :, None, :] * steps[None, None, :, None])
    pvec = jnp.concatenate([jnp.real(am), jnp.real(am)], axis=-1)
    qvec = jnp.concatenate([-jnp.imag(am), jnp.imag(am)], axis=-1)
    pq = jnp.stack([pvec, qvec], axis=3)
    pq = pq.transpose(1, 0, 2, 3, 4).reshape(g, 4 * n_steps, 2 * S5_STATE)
    return mop, vop, pq


def _s5_kernel(zs_ref, sel_ref, selt_ref, mop_ref, vop_ref, pq_ref, y_ref, *, lat_chunks, n_steps, tw):
    nc = zs_ref.shape[0]
    sw = 2 * S5_STATE
    u = jnp.dot(zs_ref[...], sel_ref[...], preferred_element_type=F32).astype(BF16)
    p = jnp.dot(u, mop_ref[...], preferred_element_type=F32)
    row = lax.broadcasted_iota(I32, (nc, sw), 0)
    pos_f = jnp.where(row >= lat_chunks, row - lat_chunks, row + (nc - lat_chunks))
    pos_b = (nc - 1) - row

    def chain(xl, pos, base, forward):
        x = jnp.where(pos >= 1, pltpu.roll(xl, 1 if forward else nc - 1, 0), 0.0)
        for m in range(n_steps):
            k = 1 << m
            s = jnp.where(pos >= k, pltpu.roll(x, k if forward else nc - k, 0), 0.0)
            pv = pq_ref[base + 2 * m:base + 2 * m + 1, :]
            qv = pq_ref[base + 2 * m + 1:base + 2 * m + 2, :]
            x = x + pv * s + qv * pltpu.roll(s, S5_STATE, 1)
        return x

    xf = chain(p[:, tw:tw + sw], pos_f, 0, True)
    xb = chain(p[:, tw + sw:tw + 2 * sw], pos_b, 2 * n_steps, False)
    xin = jnp.concatenate([xf, xb], axis=1).astype(BF16)
    y = p[:, :tw] + jnp.dot(xin, vop_ref[...], preferred_element_type=F32)
    contrib = jnp.dot(y.astype(BF16), selt_ref[...], preferred_element_type=F32)

    @pl.when(pl.program_id(0) % S5_PER_BLOCK == 0)
    def _():
        y_ref[...] = contrib

    @pl.when(pl.program_id(0) % S5_PER_BLOCK != 0)
    def _():
        y_ref[...] += contrib


def _s5_select_tables(t_len):
    r = jnp.arange(t_len * LANES, dtype=I32)[None, :, None]
    c = jnp.arange(t_len * S5_GROUP, dtype=I32)[None, None, :]
    gl = jnp.arange(S5_PER_BLOCK, dtype=I32)[:, None, None]
    sel = (r == (c // S5_GROUP) * LANES + gl * S5_GROUP + c % S5_GROUP).astype(BF16)
    return sel, sel.transpose(0, 2, 1)


def _s5_mix(zs, n_lat, sel, selt, mop, vop, pq, t_len, n_steps):
    nb, nc, bw = zs.shape
    g = S5_GROUPS
    tw = t_len * S5_GROUP
    blk = lambda i: (i // S5_PER_BLOCK, 0, 0)
    loc = lambda i: (i % S5_PER_BLOCK, 0, 0)
    return pl.pallas_call(
        functools.partial(_s5_kernel, lat_chunks=n_lat // t_len, n_steps=n_steps, tw=tw),
        grid=(g,),
        in_specs=[pl.BlockSpec((None, nc, bw), blk),
                  pl.BlockSpec((None, bw, tw), loc),
                  pl.BlockSpec((None, tw, bw), loc),
                  pl.BlockSpec((None, tw, tw + 4 * S5_STATE), lambda i: (i, 0, 0)),
                  pl.BlockSpec((None, 4 * S5_STATE, tw), lambda i: (i, 0, 0)),
                  pl.BlockSpec((None, 4 * n_steps, 2 * S5_STATE), lambda i: (i, 0, 0))],
        out_specs=pl.BlockSpec((None, nc, bw), blk),
        out_shape=jax.ShapeDtypeStruct((nb, nc, bw), F32),
        compiler_params=_cparams(("arbitrary",)),
        name="s5_mix",
    )(zs, sel, selt, mop, vop, pq)


def _dft_tables(n_pos):
    half = n_pos // 2
    blk = min(DFT_BLK, half)
    kk = jnp.arange(half, dtype=I32)[:, None]
    ang = lambda prod: (2.0 * math.pi / n_pos) * (prod % n_pos).astype(F32)
    base = ang(kk * jnp.arange(blk, dtype=I32)[None, :])
    phase = ang(kk * (blk * jnp.arange(half // blk, dtype=I32))[None, :])
    return jnp.cos(base), jnp.sin(base), jnp.cos(phase), jnp.sin(phase)


def _channel_tables():
    q = jnp.arange(FFT_GROUP, dtype=I32)
    ang = (2.0 * math.pi / FFT_GROUP) * ((q[:, None] * q[None, :]) % FFT_GROUP).astype(F32)
    return jnp.cos(ang).astype(BF16), jnp.sin(ang).astype(BF16)


def _chan_dft_kernel(zl_ref, zh_ref, cc_ref, sc_ref, rc_ref, rs_ref):
    w = zl_ref.shape[1]
    for hf, zref in enumerate((zl_ref, zh_ref)):
        z = zref[...].astype(BF16)
        for g in range(FFT_GROUPS):
            sl = slice(g * FFT_GROUP, (g + 1) * FFT_GROUP)
            osl = slice(hf * w + g * FFT_GROUP, hf * w + (g + 1) * FFT_GROUP)
            rc_ref[:, osl] = jnp.dot(z[:, sl], cc_ref[...], preferred_element_type=F32).astype(BF16)
            rs_ref[:, osl] = jnp.dot(z[:, sl], sc_ref[...], preferred_element_type=F32).astype(BF16)


def _posdft_kernel(bc_ref, bs_ref, pc_ref, ps_ref, alt_ref, rc_ref, rs_ref, o_ref, *, scale):
    tk = bc_ref.shape[0]
    w = o_ref.shape[2]
    bc = bc_ref[...]
    bs = bs_ref[...]
    cparts, sparts = [], []
    for t in range(pc_ref.shape[1]):
        pc = pc_ref[:, t:t + 1]
        ps = ps_ref[:, t:t + 1]
        cparts.append((pc * bc - ps * bs).astype(BF16))
        sparts.append((ps * bc + pc * bs).astype(BF16))
    cm = jnp.concatenate(cparts, axis=1)
    sm = jnp.concatenate(sparts, axis=1)
    alt = alt_ref[...]
    krow = lax.broadcasted_iota(I32, (tk, w), 0) + pl.program_id(0) * tk
    sign = jnp.where((krow & 1) == 0, 1.0, -1.0)

    def half_out(c, s):
        u = (jnp.dot(c, rc_ref[...], preferred_element_type=F32)
             - jnp.dot(s, rs_ref[...], preferred_element_type=F32))
        return (u[:, :w] + sign * u[:, w:]) * scale

    o_ref[0] = half_out(cm, sm)
    o_ref[1] = half_out(cm * alt, sm * alt)


def _ctx_dft_kernel(z_ref, cc_ref, sc_ref, cm_ref, sm_ref, o_ref, *, scale):
    z = z_ref[...].astype(BF16)
    for g in range(FFT_GROUPS):
        sl = slice(g * FFT_GROUP, (g + 1) * FFT_GROUP)
        zc = jnp.dot(z[:, sl], cc_ref[...], preferred_element_type=F32).astype(BF16)
        zs = jnp.dot(z[:, sl], sc_ref[...], preferred_element_type=F32).astype(BF16)
        o_ref[:, sl] = (jnp.dot(cm_ref[...], zc, preferred_element_type=F32)
                        - jnp.dot(sm_ref[...], zs, preferred_element_type=F32)) * scale


def _fourier_latent(z_all, n, tables, cc, sc):
    bcos, bsin, pcos, psin = tables
    w = z_all.shape[1]
    half = n // 2
    assert half % 2 == 0
    tr = min(512, half)
    nb = half // tr
    const = lambda i: (0, 0)
    rc, rs = pl.pallas_call(
        _chan_dft_kernel,
        grid=(nb,),
        in_specs=[pl.BlockSpec((tr, w), lambda i: (i, 0)),
                  pl.BlockSpec((tr, w), lambda i: (i + nb, 0)),
                  pl.BlockSpec((FFT_GROUP, FFT_GROUP), const),
                  pl.BlockSpec((FFT_GROUP, FFT_GROUP), const)],
        out_specs=[pl.BlockSpec((tr, 2 * w), lambda i: (i, 0))] * 2,
        out_shape=[jax.ShapeDtypeStruct((half, 2 * w), BF16)] * 2,
        compiler_params=_cparams(("arbitrary",)),
        name="fft_chan",
    )(z_all, z_all, cc, sc)
    tk = min(FFT_KTILE, half)
    alt = jnp.where(jnp.arange(half) % 2 == 0, 1.0, -1.0).astype(BF16).reshape(1, half)
    out = pl.pallas_call(
        functools.partial(_posdft_kernel, scale=1.0 / math.sqrt(n * FFT_GROUP)),
        grid=(half // tk,),
        in_specs=[pl.BlockSpec((tk, bcos.shape[1]), lambda i: (i, 0)),
                  pl.BlockSpec((tk, bcos.shape[1]), lambda i: (i, 0)),
                  pl.BlockSpec((tk, pcos.shape[1]), lambda i: (i, 0)),
                  pl.BlockSpec((tk, pcos.shape[1]), lambda i: (i, 0)),
                  pl.BlockSpec((1, half), const),
                  pl.BlockSpec((half, 2 * w), const, pipeline_mode=pl.Buffered(1)),
                  pl.BlockSpec((half, 2 * w), const, pipeline_mode=pl.Buffered(1))],
        out_specs=pl.BlockSpec((2, tk, w), lambda i: (0, i, 0)),
        out_shape=jax.ShapeDtypeStruct((2, half, w), F32),
        compiler_params=_cparams(("arbitrary",)),
        name="fft_posdft",
    )(bcos, bsin, pcos, psin, alt, rc, rs)
    return out.reshape(n, w)


def _fourier_ctx(z, cc, sc):
    n, w = z.shape
    q = jnp.arange(n, dtype=I32)
    ang = (2.0 * math.pi / n) * ((q[:, None] * q[None, :]) % n).astype(F32)
    cm, sm = jnp.cos(ang).astype(BF16), jnp.sin(ang).astype(BF16)
    full = lambda s: pl.BlockSpec(s, lambda i: (0,) * len(s))
    return pl.pallas_call(
        functools.partial(_ctx_dft_kernel, scale=1.0 / math.sqrt(n * FFT_GROUP)),
        grid=(1,),
        in_specs=[full((n, w)), full((FFT_GROUP, FFT_GROUP)), full((FFT_GROUP, FFT_GROUP)),
                  full((n, n)), full((n, n))],
        out_specs=full((n, w)),
        out_shape=jax.ShapeDtypeStruct((n, w), F32),
        compiler_params=_cparams(("arbitrary",)),
        name="fft_ctx",
    )(z, cc, sc, cm, sm)


def _merge_kernel(h_ref, zg_ref, ys_ref, yf_ref, wglu_ref, gv_ref, sw_ref, sb_ref,
                  wg0_ref, wg1_ref, wg2_ref, bg0_ref, bg1_ref, bg2_ref,
                  wb0_ref, wb1_ref, wb2_ref, m_ref, feats, yscr):
    j = pl.program_id(1)
    tm = h_ref.shape[0]

    @pl.when(j == 0)
    def _():
        n_chunks = ys_ref.shape[1]
        for b in range(S5_WIDTH // LANES):
            for t in range(S5_T):
                yscr[b, pl.ds(t, n_chunks, stride=S5_T), :] = ys_ref[b, :, t * LANES:(t + 1) * LANES]
        y = _gelu(jnp.concatenate([yscr[b] for b in range(S5_WIDTH // LANES)], axis=1))
        feats[0] = (y * _sigmoid(jnp.dot(y.astype(BF16), wglu_ref[...],
                                         preferred_element_type=F32))).astype(BF16)
        feats[1] = yf_ref[...].astype(BF16)
        gz = _gelu(zg_ref[...])
        u = gz[:, :SGU_WIDTH]
        v = gz[:, SGU_WIDTH:]
        v = v * lax.rsqrt(jnp.mean(v * v, axis=-1, keepdims=True) + EPS) * gv_ref[...]
        vb = v.astype(BF16)
        for c in range(tm // CHUNK):
            rs = slice(c * CHUNK, (c + 1) * CHUNK)
            for hd in range(SGU_HEADS):
                cs = slice(hd * SGU_HEAD, (hd + 1) * SGU_HEAD)
                s = jnp.dot(sw_ref[hd], vb[rs, cs], preferred_element_type=F32) + sb_ref[hd]
                feats[2, rs, cs] = (u[rs, cs] * s).astype(BF16)

    h = h_ref[...]
    acc = None
    for k, (wg, bg, wb) in enumerate(((wg0_ref, bg0_ref, wb0_ref), (wg1_ref, bg1_ref, wb1_ref),
                                      (wg2_ref, bg2_ref, wb2_ref))):
        gate = _sigmoid(jnp.dot(h, wg[...], preferred_element_type=F32) + bg[...])
        br = jnp.dot(feats[k], wb[...], preferred_element_type=F32)
        acc = gate * br if acc is None else acc + gate * br
    m_ref[...] = acc.astype(BF16)


def _merge(h, zg, ys, yf, wglu_bf, gv, sw_bf, sb_full, wgate_bf, bgate, wbranch_bf):
    n, d = h.shape
    tm, tn = MERGE_TILE, MERGE_COLS
    nj = d // tn
    row = lambda i, j: (i, 0)
    const2 = lambda i, j: (0, 0)
    const3 = lambda i, j: (0, 0, 0)
    gate_spec = lambda k: pl.BlockSpec((d, tn), lambda i, j: (0, k * nj + j))
    bias_spec = lambda k: pl.BlockSpec((1, tn), lambda i, j: (0, k * nj + j))
    br_spec = lambda k: pl.BlockSpec((None, S5_WIDTH, tn), lambda i, j: (k, 0, j))
    return pl.pallas_call(
        _merge_kernel,
        grid=(n // tm, nj),
        in_specs=[pl.BlockSpec((tm, d), row),
                  pl.BlockSpec((tm, 2 * SGU_WIDTH), row),
                  pl.BlockSpec((S5_WIDTH // LANES, tm // S5_T, S5_T * LANES), lambda i, j: (0, i, 0)),
                  pl.BlockSpec((tm, FFT_WIDTH), row),
                  pl.BlockSpec((S5_WIDTH, S5_WIDTH), const2),
                  pl.BlockSpec((1, SGU_WIDTH), const2),
                  pl.BlockSpec((SGU_HEADS, CHUNK, CHUNK), const3),
                  pl.BlockSpec((SGU_HEADS, CHUNK, SGU_HEAD), const3),
                  gate_spec(0), gate_spec(1), gate_spec(2),
                  bias_spec(0), bias_spec(1), bias_spec(2),
                  br_spec(0), br_spec(1), br_spec(2)],
        out_specs=pl.BlockSpec((tm, tn), lambda i, j: (i, j)),
        out_shape=jax.ShapeDtypeStruct((n, d), BF16),
        scratch_shapes=[pltpu.VMEM((N_BRANCH, tm, S5_WIDTH), BF16),
                        pltpu.VMEM((S5_WIDTH // LANES, tm, LANES), F32)],
        compiler_params=_cparams(("arbitrary", "arbitrary")),
        name="gated_merge",
    )(h, zg, ys, yf, wglu_bf, gv.reshape(1, SGU_WIDTH), sw_bf, sb_full,
      wgate_bf, wgate_bf, wgate_bf, bgate, bgate, bgate, wbranch_bf, wbranch_bf, wbranch_bf)


def _outproj_kernel(x_ref, m_ref, wo_ref, g1_ref, g2_ref, sh_ref, sc_ref, rw_ref, rb_ref,
                    x1_ref, h2_ref, ti_ref, tw_ref):
    x1 = x_ref[...] + g1_ref[...] * jnp.dot(m_ref[...], wo_ref[...], preferred_element_type=F32)
    x1_ref[...] = x1
    y = x1 * lax.rsqrt(jnp.mean(x1 * x1, axis=-1, keepdims=True) + EPS) * g2_ref[...]
    h2 = y * (1.0 + sc_ref[...]) + sh_ref[...]
    h2_ref[...] = h2
    h_hi = h2.astype(BF16)
    h_lo = (h2 - h_hi.astype(F32)).astype(BF16)
    p_hi = jnp.dot(h_hi, rw_ref[...], preferred_element_type=F32)
    p_lo = jnp.dot(h_lo, rw_ref[:, :LANES], preferred_element_type=F32)
    logits = p_hi[:, :LANES] + p_hi[:, LANES:] + p_lo + rb_ref[...]
    lane = lax.broadcasted_iota(I32, logits.shape, 1)
    vals, idxs = [], []
    for _ in range(TOP_K):
        mx = jnp.max(logits, axis=-1, keepdims=True)
        am = jnp.min(jnp.where(logits == mx, lane, LANES), axis=-1, keepdims=True)
        vals.append(mx)
        idxs.append(am)
        logits = jnp.where(lane == am, -jnp.inf, logits)
    ex = [jnp.exp(v - vals[0]) for v in vals]
    den = ex[0] + ex[1] + ex[2] + ex[3]
    ti = jnp.zeros(lane.shape, I32)
    tw = jnp.zeros(lane.shape, F32)
    for k in range(TOP_K):
        ti = jnp.where(lane == k, idxs[k], ti)
        tw = jnp.where(lane == k, ex[k] / den, tw)
    ti_ref[...] = ti
    tw_ref[...] = tw


def _outproj(xa, m, n_lat, wo_bf, g1, g2n, sh2, sc2, rw_pad, rb_pad):
    n, d = xa.shape
    tm = ROW_TILE
    lat_tiles = n_lat // tm
    row = lambda i: (i, 0)
    kind = lambda i: (jnp.where(i >= lat_tiles, 1, 0), 0, 0)
    const = lambda i: (0, 0)
    return pl.pallas_call(
        _outproj_kernel,
        grid=(n // tm,),
        in_specs=[pl.BlockSpec((tm, d), row),
                  pl.BlockSpec((tm, d), row),
                  pl.BlockSpec((d, d), const),
                  pl.BlockSpec((None, 1, d), kind),
                  pl.BlockSpec((1, d), const),
                  pl.BlockSpec((None, 1, d), kind),
                  pl.BlockSpec((None, 1, d), kind),
                  pl.BlockSpec((d, 2 * LANES), const),
                  pl.BlockSpec((1, LANES), const)],
        out_specs=[pl.BlockSpec((tm, d), row), pl.BlockSpec((tm, d), row),
                   pl.BlockSpec((tm, LANES), row), pl.BlockSpec((tm, LANES), row)],
        out_shape=[jax.ShapeDtypeStruct((n, d), F32), jax.ShapeDtypeStruct((n, d), F32),
                   jax.ShapeDtypeStruct((n, LANES), I32), jax.ShapeDtypeStruct((n, LANES), F32)],
        compiler_params=_cparams(("arbitrary",)),
        name="outproj_router",
    )(xa, m, wo_bf, g1, g2n.reshape(1, d), sh2, sc2, rw_pad, rb_pad)


def _routing_tables(top_idx, n_items):
    flat_e = top_idx.reshape(-1)
    onehot = (flat_e[:, None] == jnp.arange(N_EXPERTS, dtype=I32)[None, :]).astype(I32)
    csum = jnp.cumsum(onehot, axis=0)
    rank = jnp.sum((csum - onehot) * onehot, axis=1)
    counts = csum[-1]
    padded = (counts + MOE_CH - 1) // MOE_CH * MOE_CH
    pend = jnp.cumsum(padded)
    pstart = pend - padded
    dest = jnp.sum(onehot * pstart[None, :], axis=1) + rank
    per_e = (padded + MOE_RMAX - 1) // MOE_RMAX
    iend = jnp.cumsum(per_e)
    istart = iend - per_e
    total = iend[-1]
    t = jnp.arange(n_items, dtype=I32)
    valid = t < total
    tc = jnp.minimum(t, total - 1)
    e_of = jnp.minimum(jnp.sum((tc[:, None] >= iend[None, :]).astype(I32), axis=1), N_EXPERTS - 1)
    local = tc - istart[e_of]
    row0 = pstart[e_of] + local * MOE_RMAX
    nch = jnp.clip((padded[e_of] - local * MOE_RMAX) // MOE_CH, 0, MOE_RMAX // MOE_CH)
    nch = jnp.where(valid, nch, 0)
    return (dest.astype(I32), counts.astype(I32), pstart.astype(I32), padded.astype(I32),
            e_of.astype(I32), row0.astype(I32), nch.astype(I32))


def _dispatch_kernel(cnt_ref, pst_ref, pad_ref, dest_ref, h_ref, xs_ref, zrow, sem, zsem):
    i = pl.program_id(0)
    tq = h_ref.shape[0]

    @pl.when(i == 0)
    def _():
        zrow[...] = jnp.zeros(zrow.shape, zrow.dtype)

        def per_expert(e, carry):
            first = pst_ref[e] + cnt_ref[e]
            npad = pad_ref[e] - cnt_ref[e]

            def start(r, c):
                pltpu.make_async_copy(zrow.at[pl.ds(0, 1)], xs_ref.at[pl.ds(first + r, 1)], zsem).start()
                return c

            def wait(r, c):
                pltpu.make_async_copy(zrow.at[pl.ds(0, 1)], xs_ref.at[pl.ds(first + r, 1)], zsem).wait()
                return c

            lax.fori_loop(0, npad, start, 0)
            lax.fori_loop(0, npad, wait, 0)
            return carry

        lax.fori_loop(0, N_EXPERTS, per_expert, 0)

    def start(t, c):
        src = h_ref.at[pl.ds(t, 1)]
        for k in range(TOP_K):
            pltpu.make_async_copy(src, xs_ref.at[pl.ds(dest_ref[0, t * TOP_K + k], 1)], sem).start()
        return c

    lax.fori_loop(0, tq, start, 0, unroll=2)
    for _ in range(TOP_K):
        pltpu.make_async_copy(h_ref, xs_ref.at[pl.ds(0, tq)], sem).wait()


def _dispatch(h2, dest, counts, pstart, padded, n_rows):
    n, d = h2.shape
    tq = DISP_TILE
    nt = n // tq
    dest3 = dest.reshape(nt, 1, tq * TOP_K)
    return pl.pallas_call(
        _dispatch_kernel,
        grid_spec=pltpu.PrefetchScalarGridSpec(
            num_scalar_prefetch=3,
            grid=(nt,),
            in_specs=[pl.BlockSpec((None, 1, tq * TOP_K), lambda i, *_: (i, 0, 0), memory_space=pltpu.SMEM),
                      pl.BlockSpec((tq, d), lambda i, *_: (i, 0))],
            out_specs=pl.BlockSpec(memory_space=pl.ANY),
            scratch_shapes=[pltpu.VMEM((8, d), F32), pltpu.SemaphoreType.DMA(()), pltpu.SemaphoreType.DMA(())]),
        out_shape=jax.ShapeDtypeStruct((n_rows, d), F32),
        compiler_params=_cparams(("arbitrary",)),
        name="moe_dispatch",
    )(counts, pstart, padded, dest3, h2)


def _experts_kernel(ie_ref, ir_ref, in_ref, xs_ref, wg_ref, wu_ref, wd_ref, bg_ref, bu_ref,
                    ys_ref, xstage, xb, acc, pending, sem_in, sem_out, *, units_per_step):
    t = pl.program_id(0)
    j = pl.program_id(1)
    nf = pl.num_programs(1)
    nch = in_ref[t]
    row0 = ir_ref[t]
    ch = MOE_CH

    def hbm_rows(c):
        return pl.ds(pl.multiple_of(row0 + c * ch, ch), ch)

    def vmem_rows(c, size=ch):
        return pl.ds(pl.multiple_of(c * ch, ch), size)

    def in_copy(c, slot):
        return pltpu.make_async_copy(xs_ref.at[hbm_rows(c)], xstage.at[slot], sem_in.at[slot])

    def out_copy(c):
        return pltpu.make_async_copy(acc.at[vmem_rows(c)], ys_ref.at[hbm_rows(c)], sem_out)

    def drain():
        def body(c, carry):
            out_copy(0).wait()
            return carry
        lax.fori_loop(0, pending[0], body, 0)
        pending[0] = 0

    cur = t & 1
    n_items = pl.num_programs(0)
    t_next = jnp.minimum(t + 1, n_items - 1)
    nch_next = jnp.where(t + 1 < n_items, in_ref[t_next], 0)
    row0_next = ir_ref[t_next]

    def next_copy(u, k):
        src = xs_ref.at[pl.ds(pl.multiple_of(row0_next + u * ch, ch), ch)]
        return pltpu.make_async_copy(src, xstage.at[k], sem_in.at[k])

    @pl.when(jnp.logical_and(t == 0, j == 0))
    def _():
        pending[0] = 0

    @pl.when(jnp.logical_and(jnp.logical_and(t == 0, j == 0), nch > 0))
    def _():
        in_copy(0, 0).start()

        def body(c, carry):
            slot = c & 1

            @pl.when(c + 1 < nch)
            def _():
                in_copy(c + 1, 1 - slot).start()

            in_copy(c, slot).wait()
            xb[0, vmem_rows(c), :] = xstage[slot].astype(BF16)
            return carry

        lax.fori_loop(0, nch, body, 0)

    for k in range(units_per_step):
        @pl.when(j * units_per_step + k < nch_next)
        def _():
            next_copy(j * units_per_step + k, k).start()

    @pl.when(nch > 0)
    def _():
        def partial_out(c, size):
            rows = vmem_rows(c, size)
            x = xb[cur, rows, :]
            gate = jnp.dot(x, wg_ref[...].astype(BF16), preferred_element_type=F32) + bg_ref[...]
            up = jnp.dot(x, wu_ref[...].astype(BF16), preferred_element_type=F32) + bu_ref[...]
            gate = jnp.minimum(gate, SWIGLU_LIMIT)
            up = jnp.clip(up, -SWIGLU_LIMIT, SWIGLU_LIMIT)
            act = gate * _sigmoid(SWIGLU_ALPHA * gate) * (up + 1.0)
            return rows, jnp.dot(act.astype(BF16), wd_ref[...].astype(BF16), preferred_element_type=F32)

        def accumulate(first):
            def store(rows, part):
                if first:
                    acc[rows, :] = part
                else:
                    acc[rows, :] += part

            def body(c4, carry):
                store(*partial_out(4 * c4, 4 * ch))
                return carry

            quads = lax.shift_right_logical(nch, 2)
            lax.fori_loop(0, quads, body, 0)

            @pl.when((nch & 2) != 0)
            def _():
                store(*partial_out(4 * quads, 2 * ch))

            @pl.when((nch & 1) != 0)
            def _():
                store(*partial_out(nch - 1, ch))

        @pl.when(j == 0)
        def _():
            drain()
            accumulate(True)

        @pl.when(j > 0)
        def _():
            accumulate(False)

        @pl.when(j == nf - 1)
        def _():
            def body(c, carry):
                out_copy(c).start()
                return carry
            lax.fori_loop(0, nch, body, 0)
            pending[0] = nch

    for k in range(units_per_step):
        @pl.when(j * units_per_step + k < nch_next)
        def _():
            next_copy(j * units_per_step + k, k).wait()
            xb[1 - cur, vmem_rows(j * units_per_step + k), :] = xstage[k].astype(BF16)

    @pl.when(jnp.logical_and(t == n_items - 1, j == nf - 1))
    def _():
        drain()


def _experts(xs, layer, w_up, b_up, w_down, item_e, item_row0, item_nch):
    n_rows, d = xs.shape
    n_layers, n_exp, _, two_f = w_up.shape
    f = two_f // 2
    tf = MOE_TF
    nf = f // tf
    assert nf >= 2
    units_per_step = max(2, -(-(MOE_RMAX // MOE_CH) // nf))
    n_items = item_e.shape[0]
    b_up4 = b_up.reshape(n_layers, n_exp, 1, two_f)

    def jj(t, j, ie, ir, inn):
        return jnp.where(inn[t] > 0, j, nf - 1)

    return pl.pallas_call(
        functools.partial(_experts_kernel, units_per_step=units_per_step),
        grid_spec=pltpu.PrefetchScalarGridSpec(
            num_scalar_prefetch=3,
            grid=(n_items, nf),
            in_specs=[pl.BlockSpec(memory_space=pl.ANY),
                      pl.BlockSpec((None, None, d, tf), lambda t, j, ie, ir, inn: (layer, ie[t], 0, jj(t, j, ie, ir, inn))),
                      pl.BlockSpec((None, None, d, tf), lambda t, j, ie, ir, inn: (layer, ie[t], 0, nf + jj(t, j, ie, ir, inn))),
                      pl.BlockSpec((None, None, tf, d), lambda t, j, ie, ir, inn: (layer, ie[t], jj(t, j, ie, ir, inn), 0)),
                      pl.BlockSpec((None, None, 1, tf), lambda t, j, ie, ir, inn: (layer, ie[t], 0, jj(t, j, ie, ir, inn))),
                      pl.BlockSpec((None, None, 1, tf), lambda t, j, ie, ir, inn: (layer, ie[t], 0, nf + jj(t, j, ie, ir, inn)))],
            out_specs=pl.BlockSpec(memory_space=pl.ANY),
            scratch_shapes=[pltpu.VMEM((units_per_step, MOE_CH, d), F32),
                            pltpu.VMEM((2, MOE_RMAX, d), BF16),
                            pltpu.VMEM((MOE_RMAX, d), F32),
                            pltpu.SMEM((1,), I32),
                            pltpu.SemaphoreType.DMA((units_per_step,)),
                            pltpu.SemaphoreType.DMA(())]),
        out_shape=jax.ShapeDtypeStruct((n_rows, d), F32),
        compiler_params=_cparams(("arbitrary", "arbitrary")),
        name="moe_experts",
    )(item_e, item_row0, item_nch, xs, w_up, w_up, w_down, b_up4, b_up4)


def _combine_kernel(dc_ref, dn_ref, tw_ref, ti_ref, x1_ref, g2_ref, fg_ref, bd_ref, ys_ref, o_ref,
                    gbuf, sem, *, final):
    i = pl.program_id(0)
    n = pl.num_programs(0)
    tq = x1_ref.shape[0]

    def issue(dref, slot):
        def body(t, c):
            for k in range(TOP_K):
                pltpu.make_async_copy(ys_ref.at[pl.ds(dref[0, t * TOP_K + k], 1)],
                                      gbuf.at[slot, k, pl.ds(t, 1)], sem.at[slot]).start()
            return c
        lax.fori_loop(0, tq, body, 0, unroll=2)

    slot = i & 1

    @pl.when(i == 0)
    def _():
        issue(dc_ref, 0)

    @pl.when(i + 1 < n)
    def _():
        issue(dn_ref, 1 - slot)

    for k in range(TOP_K):
        pltpu.make_async_copy(ys_ref.at[pl.ds(0, tq)], gbuf.at[slot, k], sem.at[slot]).wait()
    tw = tw_ref[...]
    ti = ti_ref[...]
    lane = lax.broadcasted_iota(I32, tw.shape, 1)
    ew = jnp.zeros(tw.shape, F32)
    for k in range(TOP_K):
        ew = ew + jnp.where(lane == ti[:, k:k + 1], tw[:, k:k + 1], 0.0)
    y = jnp.dot(ew.astype(BF16), bd_ref[...], preferred_element_type=F32)
    for k in range(TOP_K):
        y = y + tw[:, k:k + 1] * gbuf[slot, k]
    x2 = x1_ref[...] + g2_ref[...] * y
    if final:
        x2 = x2 * lax.rsqrt(jnp.mean(x2 * x2, axis=-1, keepdims=True) + EPS) * fg_ref[...]
    o_ref[...] = x2


def _combine(ys, dest, top_w, top_idx, b_down, x1, n_lat, g2, final_g, final):
    n, d = x1.shape
    bd_pad = jnp.zeros((LANES, d), BF16).at[:N_EXPERTS].set(b_down.astype(BF16))
    tq = DISP_TILE
    lat_tiles = n_lat // tq
    dest3 = dest.reshape(n // tq, 1, tq * TOP_K)
    if final:
        n = n_lat
    nt = n // tq
    kind = lambda i: (jnp.where(i >= lat_tiles, 1, 0), 0, 0)
    return pl.pallas_call(
        functools.partial(_combine_kernel, final=final),
        grid=(nt,),
        in_specs=[pl.BlockSpec((None, 1, tq * TOP_K), lambda i: (i, 0, 0), memory_space=pltpu.SMEM),
                  pl.BlockSpec((None, 1, tq * TOP_K), lambda i: (jnp.minimum(i + 1, nt - 1), 0, 0),
                               memory_space=pltpu.SMEM),
                  pl.BlockSpec((tq, LANES), lambda i: (i, 0)),
                  pl.BlockSpec((tq, LANES), lambda i: (i, 0)),
                  pl.BlockSpec((tq, d), lambda i: (i, 0)),
                  pl.BlockSpec((None, 1, d), kind),
                  pl.BlockSpec((1, d), lambda i: (0, 0)),
                  pl.BlockSpec((LANES, d), lambda i: (0, 0)),
                  pl.BlockSpec(memory_space=pl.ANY)],
        out_specs=pl.BlockSpec((tq, d), lambda i: (i, 0)),
        out_shape=jax.ShapeDtypeStruct((n, d), F32),
        scratch_shapes=[pltpu.VMEM((2, TOP_K, tq, d), F32), pltpu.SemaphoreType.DMA((2,))],
        compiler_params=_cparams(("arbitrary",)),
        name="moe_combine",
    )(dest3, dest3, top_w, top_idx, x1, g2, final_g.reshape(1, d), bd_pad, ys)


def kernel(x, c, ctx, c_ctx, w_mod, b_mod, norm1_g, norm2_g, w_in, s5_a_re, s5_a_im, s5_log_dt,
           s5_b_re, s5_b_im, s5_c_re, s5_c_im, s5_d, s5_w_glu, sgu_norm_g, sgu_w, sgu_b, w_branch,
           w_gate, b_gate, w_out, router_w, router_b, moe_w_up, moe_b_up, moe_w_down, moe_b_down,
           final_g):
    bsz, n_lat, d = x.shape
    n_ctx = ctx.shape[1]
    assert bsz == 1
    n_layers = w_mod.shape[0]
    n = n_lat + n_ctx
    xa = jnp.concatenate([x[0], ctx[0]], axis=0).astype(F32)

    cond8 = jnp.zeros((8, d), F32).at[0].set(c[0]).at[1].set(c_ctx)
    mods = _adaln(cond8, w_mod, b_mod)[:, :2, :].reshape(n_layers, 2, N_MOD, 1, d)

    n_chunks = n // S5_T
    n_steps = max(1, (n_chunks - 1).bit_length())
    dft_tables = _dft_tables(n_lat)
    cc, sc = _channel_tables()
    sel, selt = _s5_select_tables(S5_T)
    n_assign = n * TOP_K
    n_rows = n_assign + N_EXPERTS * MOE_CH
    n_items = -(-n_assign // MOE_RMAX) + N_EXPERTS

    for i in range(n_layers):
        last = i == n_layers - 1
        sh1, sc1, g1, sh2, sc2, g2 = [mods[i, :, k] for k in range(N_MOD)]

        h, z_s5, z_fft, z_sgu = _inproj(xa, n_lat, norm1_g[i], sh1, sc1, w_in[i].astype(BF16))
        mop, vop, pq = _s5_operators(s5_a_re[i], s5_a_im[i], s5_log_dt[i], s5_b_re[i], s5_b_im[i],
                                     s5_c_re[i], s5_c_im[i], s5_d[i], S5_T, n_steps)
        y_s5 = _s5_mix(z_s5, n_lat, sel, selt, mop, vop, pq, S5_T, n_steps)
        y_fft_lat = _fourier_latent(z_fft, n_lat, dft_tables, cc, sc)
        if last:
            y_fft_ctx = jnp.zeros((n_ctx, FFT_WIDTH), F32)
        else:
            y_fft_ctx = _fourier_ctx(z_fft[n_lat:], cc, sc)
        y_fft = jnp.concatenate([y_fft_lat, y_fft_ctx], axis=0)
        sb_full = jnp.broadcast_to(sgu_b[i].astype(F32)[:, :, None], (SGU_HEADS, CHUNK, SGU_HEAD))
        m = _merge(h, z_sgu, y_s5, y_fft, s5_w_glu[i].astype(BF16), sgu_norm_g[i].astype(F32),
                   sgu_w[i].astype(BF16), sb_full, w_gate[i].astype(BF16),
                   b_gate[i].astype(F32).reshape(1, N_BRANCH * d), w_branch[i].astype(BF16))

        rw32 = jnp.zeros((d, LANES), F32).at[:, :N_EXPERTS].set(router_w[i].astype(F32))
        rw_hi = rw32.astype(BF16)
        rw_pad = jnp.concatenate([rw_hi, (rw32 - rw_hi.astype(F32)).astype(BF16)], axis=1)
        rb_pad = jnp.full((1, LANES), -1e30, F32).at[0, :N_EXPERTS].set(router_b[i].astype(F32))
        x1, h2, top_idx, top_w = _outproj(xa, m, n_lat, w_out[i].astype(BF16), g1, norm2_g[i].astype(F32),
                                          sh2, sc2, rw_pad, rb_pad)
        dest, counts, pstart, padded, item_e, item_row0, item_nch = _routing_tables(
            top_idx[:, :TOP_K], n_items)
        xs = _dispatch(h2, dest, counts, pstart, padded, n_rows)
        ys = _experts(xs, i, moe_w_up, moe_b_up, moe_w_down, item_e, item_row0, item_nch)
        xa = _combine(ys, dest, top_w, top_idx, moe_b_down[i], x1, n_lat, g2, final_g.astype(F32), last)

    return xa.reshape(bsz, n_lat, d).astype(x.dtype)
```

```python
import functools
import math

import jax
import jax.numpy as jnp
from jax import lax
from jax.experimental import pallas as pl
from jax.experimental.pallas import tpu as pltpu

F32 = jnp.float32
BF16 = jnp.bfloat16
I32 = jnp.int32

EPS = 1e-6
S5_WIDTH = 512
S5_GROUP = 16
S5_GROUPS = 32
S5_STATE = 64
FFT_WIDTH = 512
FFT_GROUPS = 4
FFT_GROUP = 128
SGU_WIDTH = 512
SGU_HEADS = 4
SGU_HEAD = 128
CHUNK = 128
N_BRANCH = 3
N_EXPERTS = 32
TOP_K = 4
SWIGLU_LIMIT = 7.0
SWIGLU_ALPHA = 1.702
N_MOD = 6

LANES = 128
S5_PER_BLOCK = LANES // S5_GROUP
VMEM_LIMIT_BYTES = 56 * 1024 * 1024

S5_T = 16
ROW_TILE = 256
MERGE_TILE = 384
MERGE_COLS = 512
FFT_KTILE = 256
DFT_BLK = 256
MOE_CH = 128
MOE_RMAX = 1536
MOE_TF = 512
DISP_TILE = 256


def _cparams(sem, vmem=VMEM_LIMIT_BYTES):
    return pltpu.CompilerParams(dimension_semantics=sem, vmem_limit_bytes=vmem)


def _gelu(x):
    return 0.5 * x * (1.0 + jnp.tanh(math.sqrt(2.0 / math.pi) * (x + 0.044715 * x * x * x)))


def _sigmoid(x):
    return 1.0 / (1.0 + jnp.exp(-x))


def _adaln_kernel(c_ref, w_ref, b_ref, o_ref):
    c = c_ref[...]
    s = c * _sigmoid(c)
    o_ref[...] = jnp.dot(s, w_ref[...], preferred_element_type=F32,
                         precision=lax.Precision.HIGHEST) + b_ref[...]


def _adaln(cond8, w_mod, b_mod):
    n_layers, d, nm = w_mod.shape
    tn = 1024
    return pl.pallas_call(
        _adaln_kernel,
        grid=(n_layers, nm // tn),
        in_specs=[pl.BlockSpec((8, d), lambda l, j: (0, 0)),
                  pl.BlockSpec((None, d, tn), lambda l, j: (l, 0, j)),
                  pl.BlockSpec((None, 1, tn), lambda l, j: (l, 0, j))],
        out_specs=pl.BlockSpec((None, 8, tn), lambda l, j: (l, 0, j)),
        out_shape=jax.ShapeDtypeStruct((n_layers, 8, nm), F32),
        compiler_params=_cparams(("arbitrary", "arbitrary")),
        name="adaln",
    )(cond8, w_mod, b_mod.reshape(n_layers, 1, nm))


def _token_rows(x_lat, x_ctx, tm):
    d = x_lat.shape[1]
    if x_ctx is None:
        return (x_lat,), [pl.BlockSpec((tm, d), lambda i: (i, 0))], lambda refs: refs[0][...]
    lat_tiles = x_lat.shape[0] // tm
    specs = [pl.BlockSpec((tm, d), lambda i: (jnp.minimum(i, lat_tiles - 1), 0)),
             pl.BlockSpec((tm, d), lambda i: (jnp.maximum(i - lat_tiles, 0), 0))]
    load = lambda refs: jnp.where(pl.program_id(0) >= lat_tiles, refs[1][...], refs[0][...])
    return (x_lat, x_ctx), specs, load


def _inproj_kernel(*refs, load_x):
    g_ref, sh_ref, sc_ref, w_ref, h_ref, zs_ref, zf_ref, zg_ref, zscr = refs[-9:]
    x = load_x(refs[:-9])
    y = x * lax.rsqrt(jnp.mean(x * x, axis=-1, keepdims=True) + EPS) * g_ref[...]
    h = (y * (1.0 + sc_ref[...]) + sh_ref[...]).astype(BF16)
    h_ref[...] = h
    z = jnp.dot(h, w_ref[...], preferred_element_type=F32)
    zf_ref[...] = z[:, S5_WIDTH:S5_WIDTH + FFT_WIDTH]
    zg_ref[...] = z[:, S5_WIDTH + FFT_WIDTH:]
    n_chunks = zs_ref.shape[1]
    for b in range(S5_WIDTH // LANES):
        zscr[b] = z[:, b * LANES:(b + 1) * LANES]
        for j in range(S5_T):
            zs_ref[b, :, j * LANES:(j + 1) * LANES] = \
                zscr[b, pl.ds(j, n_chunks, stride=S5_T), :].astype(BF16)


def _inproj(x_lat, x_ctx, n, n_lat, layer, g, sh, sc, w_in_bf):
    d = x_lat.shape[1]
    tm = ROW_TILE
    lat_tiles = n_lat // tm
    row = lambda i: (i, 0)
    kind = lambda i: (jnp.where(i >= lat_tiles, 1, 0), 0, 0)
    x_ops, x_specs, load_x = _token_rows(x_lat, x_ctx, tm)
    return pl.pallas_call(
        functools.partial(_inproj_kernel, load_x=load_x),
        grid=(n // tm,),
        in_specs=x_specs + [
                  pl.BlockSpec((1, d), lambda i: (0, 0)),
                  pl.BlockSpec((None, 1, d), kind),
                  pl.BlockSpec((None, 1, d), kind),
                  pl.BlockSpec((None, d, w_in_bf.shape[2]), lambda i: (layer, 0, 0))],
        out_specs=[pl.BlockSpec((tm, d), row),
                   pl.BlockSpec((S5_WIDTH // LANES, tm // S5_T, S5_T * LANES), lambda i: (0, i, 0)),
                   pl.BlockSpec((tm, FFT_WIDTH), row),
                   pl.BlockSpec((tm, 2 * SGU_WIDTH), row)],
        out_shape=[jax.ShapeDtypeStruct((n, d), BF16),
                   jax.ShapeDtypeStruct((S5_WIDTH // LANES, n // S5_T, S5_T * LANES), BF16),
                   jax.ShapeDtypeStruct((n, FFT_WIDTH), F32),
                   jax.ShapeDtypeStruct((n, 2 * SGU_WIDTH), F32)],
        scratch_shapes=[pltpu.VMEM((S5_WIDTH // LANES, tm, LANES), F32)],
        compiler_params=_cparams(("arbitrary",)),
        name="inproj",
    )(*x_ops, g.reshape(1, d), sh, sc, w_in_bf)


def _s5_operators(a_re, a_im, log_dt, b_re, b_im, c_re, c_im, d_skip, t_len, n_steps):
    lam = lax.complex(a_re.astype(F32), a_im.astype(F32))
    dt = jnp.exp(log_dt.astype(F32))[..., None]
    ldt = lam * dt
    tau = jnp.arange(t_len + 1, dtype=F32)
    pw = jnp.exp(ldt[:, :, None, :] * tau[None, None, :, None])
    a_bar = jnp.exp(ldt)
    bb = ((a_bar - 1.0) / lam)[..., None] * lax.complex(b_re.astype(F32), b_im.astype(F32))
    cc = lax.complex(c_re.astype(F32), c_im.astype(F32))
    k = jnp.real(jnp.einsum('dghp,dgtp,dgpk->dgthk', cc, pw[:, :, :t_len], bb))
    ii = jnp.arange(t_len)
    diff = ii[None, :] - ii[:, None]
    dmask = diff[None, :, :, None, None]
    m = jnp.zeros((S5_GROUPS, t_len, t_len, S5_GROUP, S5_GROUP), F32)
    for lag in range(t_len):
        m = m + jnp.where(dmask == lag, k[0][:, lag][:, None, None], 0.0) \
              + jnp.where(dmask == -lag, k[1][:, lag][:, None, None], 0.0)
    eye_t = jnp.eye(t_len, dtype=F32)
    eye_h = jnp.eye(S5_GROUP, dtype=F32)
    dd = d_skip.astype(F32).reshape(S5_GROUPS, S5_GROUP)
    m = m + eye_t[None, :, :, None, None] * (eye_h[None] * dd[:, :, None])[:, None, None, :, :]
    g = S5_GROUPS
    tw = t_len * S5_GROUP
    m = m.transpose(0, 1, 4, 2, 3).reshape(g, tw, tw)
    wf = pw[0][:, ::-1][:, 1:, :, None] * bb[0][:, None, :, :]
    wb = pw[1][:, :t_len, :, None] * bb[1][:, None, :, :]

    def state_cols(w):
        w = w.transpose(0, 1, 3, 2)
        return jnp.concatenate([jnp.real(w), jnp.imag(w)], axis=-1).reshape(g, tw, 2 * S5_STATE)

    mop = jnp.concatenate([m, state_cols(wf), state_cols(wb)], axis=-1).astype(BF16)
    gf = cc[0][:, None, :, :] * pw[0][:, 1:, None, :]
    gb = cc[1][:, None, :, :] * pw[1][:, ::-1][:, :t_len, None, :]

    def state_rows(gm):
        gm = gm.transpose(0, 3, 1, 2).reshape(g, S5_STATE, tw)
        return jnp.concatenate([jnp.real(gm), -jnp.imag(gm)], axis=1)

    vop = jnp.concatenate([state_rows(gf), state_rows(gb)], axis=1).astype(BF16)
    steps = (t_len * (2.0 ** jnp.arange(n_steps, dtype=F32)))
    am = jnp.exp(ldt[:, :, None, :] * steps[None, None, :, None])
    pvec = jnp.concatenate([jnp.real(am), jnp.real(am)], axis=-1)
    qvec = jnp.concatenate([-jnp.imag(am), jnp.imag(am)], axis=-1)
    pq = jnp.stack([pvec, qvec], axis=3)
    pq = pq.transpose(1, 0, 2, 3, 4).reshape(g, 4 * n_steps, 2 * S5_STATE)
    return mop, vop, pq


def _s5_kernel(zs_ref, sel_ref, selt_ref, mop_ref, vop_ref, pq_ref, y_ref, *, lat_chunks, n_steps, tw):
    nc = zs_ref.shape[0]
    sw = 2 * S5_STATE
    u = jnp.dot(zs_ref[...], sel_ref[...], preferred_element_type=F32).astype(BF16)
    p = jnp.dot(u, mop_ref[...], preferred_element_type=F32)
    row = lax.broadcasted_iota(I32, (nc, sw), 0)
    pos_f = jnp.where(row >= lat_chunks, row - lat_chunks, row + (nc - lat_chunks))
    pos_b = (nc - 1) - row

    def chain(xl, pos, base, forward):
        x = jnp.where(pos >= 1, pltpu.roll(xl, 1 if forward else nc - 1, 0), 0.0)
        for m in range(n_steps):
            k = 1 << m
            s = jnp.where(pos >= k, pltpu.roll(x, k if forward else nc - k, 0), 0.0)
            pv = pq_ref[base + 2 * m:base + 2 * m + 1, :]
            qv = pq_ref[base + 2 * m + 1:base + 2 * m + 2, :]
            x = x + pv * s + qv * pltpu.roll(s, S5_STATE, 1)
        return x

    xf = chain(p[:, tw:tw + sw], pos_f, 0, True)
    xb = chain(p[:, tw + sw:tw + 2 * sw], pos_b, 2 * n_steps, False)
    xin = jnp.concatenate([xf, xb], axis=1).astype(BF16)
    y = p[:, :tw] + jnp.dot(xin, vop_ref[...], preferred_element_type=F32)
    contrib = jnp.dot(y.astype(BF16), selt_ref[...], preferred_element_type=F32)

    @pl.when(pl.program_id(0) % S5_PER_BLOCK == 0)
    def _():
        y_ref[...] = contrib

    @pl.when(pl.program_id(0) % S5_PER_BLOCK != 0)
    def _():
        y_ref[...] += contrib


def _s5_select_tables(t_len):
    r = jnp.arange(t_len * LANES, dtype=I32)[None, :, None]
    c = jnp.arange(t_len * S5_GROUP, dtype=I32)[None, None, :]
    gl = jnp.arange(S5_PER_BLOCK, dtype=I32)[:, None, None]
    sel = (r == (c // S5_GROUP) * LANES + gl * S5_GROUP + c % S5_GROUP).astype(BF16)
    return sel, sel.transpose(0, 2, 1)


def _s5_mix(zs, n_lat, sel, selt, mop, vop, pq, t_len, n_steps):
    nb, nc, bw = zs.shape
    g = S5_GROUPS
    tw = t_len * S5_GROUP
    blk = lambda i: (i // S5_PER_BLOCK, 0, 0)
    loc = lambda i: (i % S5_PER_BLOCK, 0, 0)
    return pl.pallas_call(
        functools.partial(_s5_kernel, lat_chunks=n_lat // t_len, n_steps=n_steps, tw=tw),
        grid=(g,),
        in_specs=[pl.BlockSpec((None, nc, bw), blk),
                  pl.BlockSpec((None, bw, tw), loc),
                  pl.BlockSpec((None, tw, bw), loc),
                  pl.BlockSpec((None, tw, tw + 4 * S5_STATE), lambda i: (i, 0, 0)),
                  pl.BlockSpec((None, 4 * S5_STATE, tw), lambda i: (i, 0, 0)),
                  pl.BlockSpec((None, 4 * n_steps, 2 * S5_STATE), lambda i: (i, 0, 0))],
        out_specs=pl.BlockSpec((None, nc, bw), blk),
        out_shape=jax.ShapeDtypeStruct((nb, nc, bw), F32),
        compiler_params=_cparams(("arbitrary",)),
        name="s5_mix",
    )(zs, sel, selt, mop, vop, pq)


def _dft_tables(n_pos):
    half = n_pos // 2
    blk = min(DFT_BLK, half)
    kk = jnp.arange(half, dtype=I32)[:, None]
    ang = lambda prod: (2.0 * math.pi / n_pos) * (prod % n_pos).astype(F32)
    base = ang(kk * jnp.arange(blk, dtype=I32)[None, :])
    phase = ang(kk * (blk * jnp.arange(half // blk, dtype=I32))[None, :])
    return jnp.cos(base), jnp.sin(base), jnp.cos(phase), jnp.sin(phase)


def _channel_tables():
    q = jnp.arange(FFT_GROUP, dtype=I32)
    ang = (2.0 * math.pi / FFT_GROUP) * ((q[:, None] * q[None, :]) % FFT_GROUP).astype(F32)
    return jnp.cos(ang).astype(BF16), jnp.sin(ang).astype(BF16)


def _chan_dft_kernel(zl_ref, zh_ref, cc_ref, sc_ref, rc_ref, rs_ref):
    w = zl_ref.shape[1]
    for hf, zref in enumerate((zl_ref, zh_ref)):
        z = zref[...].astype(BF16)
        for g in range(FFT_GROUPS):
            sl = slice(g * FFT_GROUP, (g + 1) * FFT_GROUP)
            osl = slice(hf * w + g * FFT_GROUP, hf * w + (g + 1) * FFT_GROUP)
            rc_ref[:, osl] = jnp.dot(z[:, sl], cc_ref[...], preferred_element_type=F32).astype(BF16)
            rs_ref[:, osl] = jnp.dot(z[:, sl], sc_ref[...], preferred_element_type=F32).astype(BF16)


def _posdft_kernel(bc_ref, bs_ref, pc_ref, ps_ref, alt_ref, rc_ref, rs_ref, o_ref, *, scale):
    tk, w = o_ref.shape
    bc = bc_ref[...]
    bs = bs_ref[...]
    cparts, sparts = [], []
    for t in range(pc_ref.shape[1]):
        pc = pc_ref[:, t:t + 1]
        ps = ps_ref[:, t:t + 1]
        cparts.append((pc * bc - ps * bs).astype(BF16))
        sparts.append((ps * bc + pc * bs).astype(BF16))
    cm = jnp.concatenate(cparts, axis=1)
    sm = jnp.concatenate(sparts, axis=1)
    alt = jnp.where(pl.program_id(0) == 1, alt_ref[...], jnp.ones(alt_ref.shape, BF16))
    krow = lax.broadcasted_iota(I32, (tk, w), 0) + pl.program_id(1) * tk
    sign = jnp.where((krow & 1) == 0, 1.0, -1.0)
    u = (jnp.dot(cm * alt, rc_ref[...], preferred_element_type=F32)
         - jnp.dot(sm * alt, rs_ref[...], preferred_element_type=F32))
    o_ref[...] = (u[:, :w] + sign * u[:, w:]) * scale


def _ctx_dft_kernel(z_ref, cc_ref, sc_ref, cm_ref, sm_ref, lat_ref, o_ref, *, scale):
    del lat_ref
    z = z_ref[...].astype(BF16)
    for g in range(FFT_GROUPS):
        sl = slice(g * FFT_GROUP, (g + 1) * FFT_GROUP)
        zc = jnp.dot(z[:, sl], cc_ref[...], preferred_element_type=F32).astype(BF16)
        zs = jnp.dot(z[:, sl], sc_ref[...], preferred_element_type=F32).astype(BF16)
        o_ref[:, sl] = (jnp.dot(cm_ref[...], zc, preferred_element_type=F32)
                        - jnp.dot(sm_ref[...], zs, preferred_element_type=F32)) * scale


def _fourier_latent(z_all, n, tables, cc, sc):
    bcos, bsin, pcos, psin = tables
    w = z_all.shape[1]
    half = n // 2
    assert half % 2 == 0
    tr = min(512, half)
    nb = half // tr
    const = lambda i: (0, 0)
    rc, rs = pl.pallas_call(
        _chan_dft_kernel,
        grid=(nb,),
        in_specs=[pl.BlockSpec((tr, w), lambda i: (i, 0)),
                  pl.BlockSpec((tr, w), lambda i: (i + nb, 0)),
                  pl.BlockSpec((FFT_GROUP, FFT_GROUP), const),
                  pl.BlockSpec((FFT_GROUP, FFT_GROUP), const)],
        out_specs=[pl.BlockSpec((tr, 2 * w), lambda i: (i, 0))] * 2,
        out_shape=[jax.ShapeDtypeStruct((half, 2 * w), BF16)] * 2,
        compiler_params=_cparams(("arbitrary",)),
        name="fft_chan",
    )(z_all, z_all, cc, sc)
    tk = min(FFT_KTILE, half)
    nk = half // tk
    alt = jnp.where(jnp.arange(half) % 2 == 0, 1.0, -1.0).astype(BF16).reshape(1, half)
    out = pl.pallas_call(
        functools.partial(_posdft_kernel, scale=1.0 / math.sqrt(n * FFT_GROUP)),
        grid=(2, nk),
        in_specs=[pl.BlockSpec((tk, bcos.shape[1]), lambda q, i: (i, 0)),
                  pl.BlockSpec((tk, bcos.shape[1]), lambda q, i: (i, 0)),
                  pl.BlockSpec((tk, pcos.shape[1]), lambda q, i: (i, 0)),
                  pl.BlockSpec((tk, pcos.shape[1]), lambda q, i: (i, 0)),
                  pl.BlockSpec((1, half), lambda q, i: (0, 0)),
                  pl.BlockSpec((half, 2 * w), lambda q, i: (0, 0), pipeline_mode=pl.Buffered(1)),
                  pl.BlockSpec((half, 2 * w), lambda q, i: (0, 0), pipeline_mode=pl.Buffered(1))],
        out_specs=pl.BlockSpec((tk, w), lambda q, i: (q * nk + i, 0)),
        out_shape=jax.ShapeDtypeStruct((z_all.shape[0], w), F32),
        compiler_params=_cparams(("arbitrary", "arbitrary")),
        name="fft_posdft",
    )(bcos, bsin, pcos, psin, alt, rc, rs)
    return out


def _fourier_ctx(z_all, y_all, n_lat, cc, sc):
    n_all, w = z_all.shape
    n = n_all - n_lat
    assert n_lat % n == 0
    q = jnp.arange(n, dtype=I32)
    ang = (2.0 * math.pi / n) * ((q[:, None] * q[None, :]) % n).astype(F32)
    cm, sm = jnp.cos(ang).astype(BF16), jnp.sin(ang).astype(BF16)
    full = lambda s: pl.BlockSpec(s, lambda i: (0,) * len(s))
    ctx_rows = pl.BlockSpec((n, w), lambda i: (n_lat // n, 0))
    return pl.pallas_call(
        functools.partial(_ctx_dft_kernel, scale=1.0 / math.sqrt(n * FFT_GROUP)),
        grid=(1,),
        in_specs=[ctx_rows, full((FFT_GROUP, FFT_GROUP)), full((FFT_GROUP, FFT_GROUP)),
                  full((n, n)), full((n, n)), pl.BlockSpec(memory_space=pl.ANY)],
        out_specs=ctx_rows,
        out_shape=jax.ShapeDtypeStruct((n_all, w), F32),
        input_output_aliases={5: 0},
        compiler_params=_cparams(("arbitrary",)),
        name="fft_ctx",
    )(z_all, cc, sc, cm, sm, y_all)


def _merge_kernel(h_ref, zg_ref, ys_ref, yf_ref, wglu_ref, gv_ref, sw_ref, sb_ref,
                  wg0_ref, wg1_ref, wg2_ref, bg0_ref, bg1_ref, bg2_ref,
                  wb0_ref, wb1_ref, wb2_ref, m_ref, feats, yscr):
    j = pl.program_id(1)
    tm = h_ref.shape[0]

    @pl.when(j == 0)
    def _():
        n_chunks = ys_ref.shape[1]
        for b in range(S5_WIDTH // LANES):
            for t in range(S5_T):
                yscr[b, pl.ds(t, n_chunks, stride=S5_T), :] = ys_ref[b, :, t * LANES:(t + 1) * LANES]
        y = _gelu(jnp.concatenate([yscr[b] for b in range(S5_WIDTH // LANES)], axis=1))
        feats[0] = (y * _sigmoid(jnp.dot(y.astype(BF16), wglu_ref[...],
                                         preferred_element_type=F32))).astype(BF16)
        feats[1] = yf_ref[...].astype(BF16)
        gz = _gelu(zg_ref[...])
        u = gz[:, :SGU_WIDTH]
        v = gz[:, SGU_WIDTH:]
        v = v * lax.rsqrt(jnp.mean(v * v, axis=-1, keepdims=True) + EPS) * gv_ref[...]
        vb = v.astype(BF16)
        for c in range(tm // CHUNK):
            rs = slice(c * CHUNK, (c + 1) * CHUNK)
            for hd in range(SGU_HEADS):
                cs = slice(hd * SGU_HEAD, (hd + 1) * SGU_HEAD)
                s = jnp.dot(sw_ref[hd], vb[rs, cs], preferred_element_type=F32) + sb_ref[hd]
                feats[2, rs, cs] = (u[rs, cs] * s).astype(BF16)

    h = h_ref[...]
    acc = None
    for k, (wg, bg, wb) in enumerate(((wg0_ref, bg0_ref, wb0_ref), (wg1_ref, bg1_ref, wb1_ref),
                                      (wg2_ref, bg2_ref, wb2_ref))):
        gate = _sigmoid(jnp.dot(h, wg[...], preferred_element_type=F32) + bg[...])
        br = jnp.dot(feats[k], wb[...], preferred_element_type=F32)
        acc = gate * br if acc is None else acc + gate * br
    m_ref[...] = acc.astype(BF16)


def _merge(h, zg, ys, yf, layer, wglu_bf, gv, sw_bf, sb_full, wgate_bf, bgate, wbranch_bf):
    n, d = h.shape
    tm, tn = MERGE_TILE, MERGE_COLS
    nj = d // tn
    row = lambda i, j: (i, 0)
    const2 = lambda i, j: (0, 0)
    const3 = lambda i, j: (0, 0, 0)
    gate_spec = lambda k: pl.BlockSpec((None, d, tn), lambda i, j: (layer, 0, k * nj + j))
    bias_spec = lambda k: pl.BlockSpec((1, tn), lambda i, j: (0, k * nj + j))
    br_spec = lambda k: pl.BlockSpec((None, None, S5_WIDTH, tn), lambda i, j: (layer, k, 0, j))
    return pl.pallas_call(
        _merge_kernel,
        grid=(n // tm, nj),
        in_specs=[pl.BlockSpec((tm, d), row),
                  pl.BlockSpec((tm, 2 * SGU_WIDTH), row),
                  pl.BlockSpec((S5_WIDTH // LANES, tm // S5_T, S5_T * LANES), lambda i, j: (0, i, 0)),
                  pl.BlockSpec((tm, FFT_WIDTH), row),
                  pl.BlockSpec((None, S5_WIDTH, S5_WIDTH), lambda i, j: (layer, 0, 0)),
                  pl.BlockSpec((1, SGU_WIDTH), const2),
                  pl.BlockSpec((None, SGU_HEADS, CHUNK, CHUNK), lambda i, j: (layer, 0, 0, 0)),
                  pl.BlockSpec((SGU_HEADS, CHUNK, SGU_HEAD), const3),
                  gate_spec(0), gate_spec(1), gate_spec(2),
                  bias_spec(0), bias_spec(1), bias_spec(2),
                  br_spec(0), br_spec(1), br_spec(2)],
        out_specs=pl.BlockSpec((tm, tn), lambda i, j: (i, j)),
        out_shape=jax.ShapeDtypeStruct((n, d), BF16),
        scratch_shapes=[pltpu.VMEM((N_BRANCH, tm, S5_WIDTH), BF16),
                        pltpu.VMEM((S5_WIDTH // LANES, tm, LANES), F32)],
        compiler_params=_cparams(("arbitrary", "arbitrary")),
        name="gated_merge",
    )(h, zg, ys, yf, wglu_bf, gv.reshape(1, SGU_WIDTH), sw_bf, sb_full,
      wgate_bf, wgate_bf, wgate_bf, bgate, bgate, bgate, wbranch_bf, wbranch_bf, wbranch_bf)


def _outproj_kernel(*refs, load_x):
    (m_ref, wo_ref, g1_ref, g2_ref, sh_ref, sc_ref, rw_ref, rb_ref,
     x1_ref, h2_ref, ti_ref, tw_ref) = refs[-12:]
    x1 = load_x(refs[:-12]) + g1_ref[...] * jnp.dot(m_ref[...], wo_ref[...], preferred_element_type=F32)
    x1_ref[...] = x1
    y = x1 * lax.rsqrt(jnp.mean(x1 * x1, axis=-1, keepdims=True) + EPS) * g2_ref[...]
    h2 = y * (1.0 + sc_ref[...]) + sh_ref[...]
    h2_ref[...] = h2
    h_hi = h2.astype(BF16)
    h_lo = (h2 - h_hi.astype(F32)).astype(BF16)
    p_hi = jnp.dot(h_hi, rw_ref[...], preferred_element_type=F32)
    p_lo = jnp.dot(h_lo, rw_ref[:, :LANES], preferred_element_type=F32)
    logits = p_hi[:, :LANES] + p_hi[:, LANES:] + p_lo + rb_ref[...]
    lane = lax.broadcasted_iota(I32, logits.shape, 1)
    vals, idxs = [], []
    for _ in range(TOP_K):
        mx = jnp.max(logits, axis=-1, keepdims=True)
        am = jnp.min(jnp.where(logits == mx, lane, LANES), axis=-1, keepdims=True)
        vals.append(mx)
        idxs.append(am)
        logits = jnp.where(lane == am, -jnp.inf, logits)
    ex = [jnp.exp(v - vals[0]) for v in vals]
    den = ex[0] + ex[1] + ex[2] + ex[3]
    ti = jnp.zeros(lane.shape, I32)
    tw = jnp.zeros(lane.shape, F32)
    for k in range(TOP_K):
        ti = jnp.where(lane == k, idxs[k], ti)
        tw = jnp.where(lane == k, ex[k] / den, tw)
    ti_ref[...] = ti
    tw_ref[...] = tw


def _outproj(x_lat, x_ctx, m, n_lat, layer, wo_bf, g1, g2n, sh2, sc2, rw_pad, rb_pad):
    n, d = m.shape
    tm = ROW_TILE
    lat_tiles = n_lat // tm
    row = lambda i: (i, 0)
    kind = lambda i: (jnp.where(i >= lat_tiles, 1, 0), 0, 0)
    const = lambda i: (0, 0)
    x_ops, x_specs, load_x = _token_rows(x_lat, x_ctx, tm)
    return pl.pallas_call(
        functools.partial(_outproj_kernel, load_x=load_x),
        grid=(n // tm,),
        in_specs=x_specs + [
                  pl.BlockSpec((tm, d), row),
                  pl.BlockSpec((None, d, d), lambda i: (layer, 0, 0)),
                  pl.BlockSpec((None, 1, d), kind),
                  pl.BlockSpec((1, d), const),
                  pl.BlockSpec((None, 1, d), kind),
                  pl.BlockSpec((None, 1, d), kind),
                  pl.BlockSpec((d, 2 * LANES), const),
                  pl.BlockSpec((1, LANES), const)],
        out_specs=[pl.BlockSpec((tm, d), row), pl.BlockSpec((tm, d), row),
                   pl.BlockSpec((tm, LANES), row), pl.BlockSpec((tm, LANES), row)],
        out_shape=[jax.ShapeDtypeStruct((n, d), F32), jax.ShapeDtypeStruct((n, d), F32),
                   jax.ShapeDtypeStruct((n, LANES), I32), jax.ShapeDtypeStruct((n, LANES), F32)],
        compiler_params=_cparams(("arbitrary",)),
        name="outproj_router",
    )(*x_ops, m, wo_bf, g1, g2n.reshape(1, d), sh2, sc2, rw_pad, rb_pad)


def _routing_tables(top_idx, n_items):
    flat_e = top_idx.reshape(-1)
    onehot = (flat_e[:, None] == jnp.arange(N_EXPERTS, dtype=I32)[None, :]).astype(I32)
    csum = jnp.cumsum(onehot, axis=0)
    rank = jnp.sum((csum - onehot) * onehot, axis=1)
    counts = csum[-1]
    padded = (counts + MOE_CH - 1) // MOE_CH * MOE_CH
    pend = jnp.cumsum(padded)
    pstart = pend - padded
    dest = jnp.sum(onehot * pstart[None, :], axis=1) + rank
    per_e = (padded + MOE_RMAX - 1) // MOE_RMAX
    iend = jnp.cumsum(per_e)
    istart = iend - per_e
    total = iend[-1]
    t = jnp.arange(n_items, dtype=I32)
    valid = t < total
    tc = jnp.minimum(t, total - 1)
    e_of = jnp.minimum(jnp.sum((tc[:, None] >= iend[None, :]).astype(I32), axis=1), N_EXPERTS - 1)
    local = tc - istart[e_of]
    row0 = pstart[e_of] + local * MOE_RMAX
    nch = jnp.clip((padded[e_of] - local * MOE_RMAX) // MOE_CH, 0, MOE_RMAX // MOE_CH)
    nch = jnp.where(valid, nch, 0)
    return (dest.astype(I32), counts.astype(I32), pstart.astype(I32), padded.astype(I32),
            e_of.astype(I32), row0.astype(I32), nch.astype(I32))


def _dispatch_kernel(cnt_ref, pst_ref, pad_ref, dest_ref, h_ref, xs_ref, zrow, sem, zsem):
    i = pl.program_id(0)
    tq = h_ref.shape[0]

    @pl.when(i == 0)
    def _():
        zrow[...] = jnp.zeros(zrow.shape, zrow.dtype)

        def per_expert(e, carry):
            first = pst_ref[e] + cnt_ref[e]
            npad = pad_ref[e] - cnt_ref[e]

            def start(r, c):
                pltpu.make_async_copy(zrow.at[pl.ds(0, 1)], xs_ref.at[pl.ds(first + r, 1)], zsem).start()
                return c

            def wait(r, c):
                pltpu.make_async_copy(zrow.at[pl.ds(0, 1)], xs_ref.at[pl.ds(first + r, 1)], zsem).wait()
                return c

            lax.fori_loop(0, npad, start, 0)
            lax.fori_loop(0, npad, wait, 0)
            return carry

        lax.fori_loop(0, N_EXPERTS, per_expert, 0)

    def start(t, c):
        src = h_ref.at[pl.ds(t, 1)]
        for k in range(TOP_K):
            pltpu.make_async_copy(src, xs_ref.at[pl.ds(dest_ref[0, t * TOP_K + k], 1)], sem).start()
        return c

    lax.fori_loop(0, tq, start, 0, unroll=2)
    for _ in range(TOP_K):
        pltpu.make_async_copy(h_ref, xs_ref.at[pl.ds(0, tq)], sem).wait()


def _dispatch(h2, dest, counts, pstart, padded, n_rows):
    n, d = h2.shape
    tq = DISP_TILE
    nt = n // tq
    dest3 = dest.reshape(nt, 1, tq * TOP_K)
    return pl.pallas_call(
        _dispatch_kernel,
        grid_spec=pltpu.PrefetchScalarGridSpec(
            num_scalar_prefetch=3,
            grid=(nt,),
            in_specs=[pl.BlockSpec((None, 1, tq * TOP_K), lambda i, *_: (i, 0, 0), memory_space=pltpu.SMEM),
                      pl.BlockSpec((tq, d), lambda i, *_: (i, 0))],
            out_specs=pl.BlockSpec(memory_space=pl.ANY),
            scratch_shapes=[pltpu.VMEM((8, d), F32), pltpu.SemaphoreType.DMA(()), pltpu.SemaphoreType.DMA(())]),
        out_shape=jax.ShapeDtypeStruct((n_rows, d), F32),
        compiler_params=_cparams(("arbitrary",)),
        name="moe_dispatch",
    )(counts, pstart, padded, dest3, h2)


def _experts_kernel(ie_ref, ir_ref, in_ref, xs_ref, wg_ref, wu_ref, wd_ref, bg_ref, bu_ref,
                    ys_ref, xstage, xb, acc, pending, sem_in, sem_out, *, units_per_step):
    t = pl.program_id(0)
    j = pl.program_id(1)
    nf = pl.num_programs(1)
    nch = in_ref[t]
    row0 = ir_ref[t]
    ch = MOE_CH

    def hbm_rows(c):
        return pl.ds(pl.multiple_of(row0 + c * ch, ch), ch)

    def vmem_rows(c, size=ch):
        return pl.ds(pl.multiple_of(c * ch, ch), size)

    def in_copy(c, slot):
        return pltpu.make_async_copy(xs_ref.at[hbm_rows(c)], xstage.at[slot], sem_in.at[slot])

    def out_copy(c):
        return pltpu.make_async_copy(acc.at[vmem_rows(c)], ys_ref.at[hbm_rows(c)], sem_out)

    def drain():
        def body(c, carry):
            out_copy(0).wait()
            return carry
        lax.fori_loop(0, pending[0], body, 0)
        pending[0] = 0

    cur = t & 1
    n_items = pl.num_programs(0)
    t_next = jnp.minimum(t + 1, n_items - 1)
    nch_next = jnp.where(t + 1 < n_items, in_ref[t_next], 0)
    row0_next = ir_ref[t_next]

    def next_copy(u, k):
        src = xs_ref.at[pl.ds(pl.multiple_of(row0_next + u * ch, ch), ch)]
        return pltpu.make_async_copy(src, xstage.at[k], sem_in.at[k])

    @pl.when(jnp.logical_and(t == 0, j == 0))
    def _():
        pending[0] = 0

    @pl.when(jnp.logical_and(jnp.logical_and(t == 0, j == 0), nch > 0))
    def _():
        in_copy(0, 0).start()

        def body(c, carry):
            slot = c & 1

            @pl.when(c + 1 < nch)
            def _():
                in_copy(c + 1, 1 - slot).start()

            in_copy(c, slot).wait()
            xb[0, vmem_rows(c), :] = xstage[slot].astype(BF16)
            return carry

        lax.fori_loop(0, nch, body, 0)

    for k in range(units_per_step):
        @pl.when(j * units_per_step + k < nch_next)
        def _():
            next_copy(j * units_per_step + k, k).start()

    @pl.when(nch > 0)
    def _():
        def partial_out(c, size):
            rows = vmem_rows(c, size)
            x = xb[cur, rows, :]
            gate = jnp.dot(x, wg_ref[...].astype(BF16), preferred_element_type=F32) + bg_ref[...]
            up = jnp.dot(x, wu_ref[...].astype(BF16), preferred_element_type=F32) + bu_ref[...]
            gate = jnp.minimum(gate, SWIGLU_LIMIT)
            up = jnp.clip(up, -SWIGLU_LIMIT, SWIGLU_LIMIT)
            act = gate * _sigmoid(SWIGLU_ALPHA * gate) * (up + 1.0)
            return rows, jnp.dot(act.astype(BF16), wd_ref[...].astype(BF16), preferred_element_type=F32)

        def accumulate(first):
            def store(rows, part):
                if first:
                    acc[rows, :] = part
                else:
                    acc[rows, :] += part

            def body(c4, carry):
                store(*partial_out(4 * c4, 4 * ch))
                return carry

            quads = lax.shift_right_logical(nch, 2)
            lax.fori_loop(0, quads, body, 0)

            @pl.when((nch & 2) != 0)
            def _():
                store(*partial_out(4 * quads, 2 * ch))

            @pl.when((nch & 1) != 0)
            def _():
                store(*partial_out(nch - 1, ch))

        @pl.when(j == 0)
        def _():
            drain()
            accumulate(True)

        @pl.when(j > 0)
        def _():
            accumulate(False)

        @pl.when(j == nf - 1)
        def _():
            def body(c, carry):
                out_copy(c).start()
                return carry
            lax.fori_loop(0, nch, body, 0)
            pending[0] = nch

    for k in range(units_per_step):
        @pl.when(j * units_per_step + k < nch_next)
        def _():
            next_copy(j * units_per_step + k, k).wait()
            xb[1 - cur, vmem_rows(j * units_per_step + k), :] = xstage[k].astype(BF16)

    @pl.when(jnp.logical_and(t == n_items - 1, j == nf - 1))
    def _():
        drain()


def _experts(xs, layer, w_up, b_up, w_down, item_e, item_row0, item_nch):
    n_rows, d = xs.shape
    n_layers, n_exp, _, two_f = w_up.shape
    f = two_f // 2
    tf = MOE_TF
    nf = f // tf
    assert nf >= 2
    units_per_step = max(2, -(-(MOE_RMAX // MOE_CH) // nf))
    n_items = item_e.shape[0]
    b_up4 = b_up.reshape(n_layers, n_exp, 1, two_f)

    def jj(t, j, ie, ir, inn):
        return jnp.where(inn[t] > 0, j, nf - 1)

    return pl.pallas_call(
        functools.partial(_experts_kernel, units_per_step=units_per_step),
        grid_spec=pltpu.PrefetchScalarGridSpec(
            num_scalar_prefetch=3,
            grid=(n_items, nf),
            in_specs=[pl.BlockSpec(memory_space=pl.ANY),
                      pl.BlockSpec((None, None, d, tf), lambda t, j, ie, ir, inn: (layer, ie[t], 0, jj(t, j, ie, ir, inn))),
                      pl.BlockSpec((None, None, d, tf), lambda t, j, ie, ir, inn: (layer, ie[t], 0, nf + jj(t, j, ie, ir, inn))),
                      pl.BlockSpec((None, None, tf, d), lambda t, j, ie, ir, inn: (layer, ie[t], jj(t, j, ie, ir, inn), 0)),
                      pl.BlockSpec((None, None, 1, tf), lambda t, j, ie, ir, inn: (layer, ie[t], 0, jj(t, j, ie, ir, inn))),
                      pl.BlockSpec((None, None, 1, tf), lambda t, j, ie, ir, inn: (layer, ie[t], 0, nf + jj(t, j, ie, ir, inn)))],
            out_specs=pl.BlockSpec(memory_space=pl.ANY),
            scratch_shapes=[pltpu.VMEM((units_per_step, MOE_CH, d), F32),
                            pltpu.VMEM((2, MOE_RMAX, d), BF16),
                            pltpu.VMEM((MOE_RMAX, d), F32),
                            pltpu.SMEM((1,), I32),
                            pltpu.SemaphoreType.DMA((units_per_step,)),
                            pltpu.SemaphoreType.DMA(())]),
        out_shape=jax.ShapeDtypeStruct((n_rows, d), F32),
        compiler_params=_cparams(("arbitrary", "arbitrary")),
        name="moe_experts",
    )(item_e, item_row0, item_nch, xs, w_up, w_up, w_down, b_up4, b_up4)


def _combine_kernel(dc_ref, dn_ref, tw_ref, ti_ref, x1_ref, g2_ref, fg_ref, bd_ref, ys_ref, o_ref,
                    gbuf, sem, *, final):
    i = pl.program_id(0)
    n = pl.num_programs(0)
    tq = x1_ref.shape[0]

    def issue(dref, slot):
        def body(t, c):
            for k in range(TOP_K):
                pltpu.make_async_copy(ys_ref.at[pl.ds(dref[0, t * TOP_K + k], 1)],
                                      gbuf.at[slot, k, pl.ds(t, 1)], sem.at[slot]).start()
            return c
        lax.fori_loop(0, tq, body, 0, unroll=2)

    slot = i & 1

    @pl.when(i == 0)
    def _():
        issue(dc_ref, 0)

    @pl.when(i + 1 < n)
    def _():
        issue(dn_ref, 1 - slot)

    for k in range(TOP_K):
        pltpu.make_async_copy(ys_ref.at[pl.ds(0, tq)], gbuf.at[slot, k], sem.at[slot]).wait()
    tw = tw_ref[...]
    ti = ti_ref[...]
    lane = lax.broadcasted_iota(I32, tw.shape, 1)
    ew = jnp.zeros(tw.shape, F32)
    for k in range(TOP_K):
        ew = ew + jnp.where(lane == ti[:, k:k + 1], tw[:, k:k + 1], 0.0)
    y = jnp.dot(ew.astype(BF16), bd_ref[...], preferred_element_type=F32)
    for k in range(TOP_K):
        y = y + tw[:, k:k + 1] * gbuf[slot, k]
    x2 = x1_ref[...] + g2_ref[...] * y
    if final:
        x2 = x2 * lax.rsqrt(jnp.mean(x2 * x2, axis=-1, keepdims=True) + EPS) * fg_ref[...]
    o_ref[...] = x2


def _combine(ys, dest, top_w, top_idx, b_down, x1, n_lat, g2, final_g, final):
    n, d = x1.shape
    bd_pad = jnp.zeros((LANES, d), BF16).at[:N_EXPERTS].set(b_down.astype(BF16))
    tq = DISP_TILE
    lat_tiles = n_lat // tq
    dest3 = dest.reshape(n // tq, 1, tq * TOP_K)
    if final:
        n = n_lat
    nt = n // tq
    kind = lambda i: (jnp.where(i >= lat_tiles, 1, 0), 0, 0)
    return pl.pallas_call(
        functools.partial(_combine_kernel, final=final),
        grid=(nt,),
        in_specs=[pl.BlockSpec((None, 1, tq * TOP_K), lambda i: (i, 0, 0), memory_space=pltpu.SMEM),
                  pl.BlockSpec((None, 1, tq * TOP_K), lambda i: (jnp.minimum(i + 1, nt - 1), 0, 0),
                               memory_space=pltpu.SMEM),
                  pl.BlockSpec((tq, LANES), lambda i: (i, 0)),
                  pl.BlockSpec((tq, LANES), lambda i: (i, 0)),
                  pl.BlockSpec((tq, d), lambda i: (i, 0)),
                  pl.BlockSpec((None, 1, d), kind),
                  pl.BlockSpec((1, d), lambda i: (0, 0)),
                  pl.BlockSpec((LANES, d), lambda i: (0, 0)),
                  pl.BlockSpec(memory_space=pl.ANY)],
        out_specs=pl.BlockSpec((tq, d), lambda i: (i, 0)),
        out_shape=jax.ShapeDtypeStruct((n, d), F32),
        scratch_shapes=[pltpu.VMEM((2, TOP_K, tq, d), F32), pltpu.SemaphoreType.DMA((2,))],
        compiler_params=_cparams(("arbitrary",)),
        name="moe_combine",
    )(dest3, dest3, top_w, top_idx, x1, g2, final_g.reshape(1, d), bd_pad, ys)


def kernel(x, c, ctx, c_ctx, w_mod, b_mod, norm1_g, norm2_g, w_in, s5_a_re, s5_a_im, s5_log_dt,
           s5_b_re, s5_b_im, s5_c_re, s5_c_im, s5_d, s5_w_glu, sgu_norm_g, sgu_w, sgu_b, w_branch,
           w_gate, b_gate, w_out, router_w, router_b, moe_w_up, moe_b_up, moe_w_down, moe_b_down,
           final_g):
    bsz, n_lat, d = x.shape
    n_ctx = ctx.shape[1]
    assert bsz == 1
    n_layers = w_mod.shape[0]
    n = n_lat + n_ctx
    x_lat, x_ctx = x.reshape(n_lat, d).astype(F32), ctx.reshape(n_ctx, d).astype(F32)
    w_in_bf, w_out_bf, w_gate_bf, w_branch_bf = (t.astype(BF16) for t in (w_in, w_out, w_gate, w_branch))
    w_glu_bf, sgu_w_bf = s5_w_glu.astype(BF16), sgu_w.astype(BF16)

    cond8 = jnp.zeros((8, d), F32).at[0].set(c[0]).at[1].set(c_ctx)
    mods = _adaln(cond8, w_mod, b_mod)[:, :2, :].reshape(n_layers, 2, N_MOD, 1, d)

    n_chunks = n // S5_T
    n_steps = max(1, (n_chunks - 1).bit_length())
    dft_tables = _dft_tables(n_lat)
    cc, sc = _channel_tables()
    sel, selt = _s5_select_tables(S5_T)
    n_assign = n * TOP_K
    n_rows = n_assign + N_EXPERTS * MOE_CH
    n_items = -(-n_assign // MOE_RMAX) + N_EXPERTS

    for i in range(n_layers):
        last = i == n_layers - 1
        sh1, sc1, g1, sh2, sc2, g2 = [mods[i, :, k] for k in range(N_MOD)]

        h, z_s5, z_fft, z_sgu = _inproj(x_lat, x_ctx, n, n_lat, i, norm1_g[i], sh1, sc1, w_in_bf)
        mop, vop, pq = _s5_operators(s5_a_re[i], s5_a_im[i], s5_log_dt[i], s5_b_re[i], s5_b_im[i],
                                     s5_c_re[i], s5_c_im[i], s5_d[i], S5_T, n_steps)
        y_s5 = _s5_mix(z_s5, n_lat, sel, selt, mop, vop, pq, S5_T, n_steps)
        y_fft = _fourier_ctx(z_fft, _fourier_latent(z_fft, n_lat, dft_tables, cc, sc), n_lat, cc, sc)
        sb_full = jnp.broadcast_to(sgu_b[i].astype(F32)[:, :, None], (SGU_HEADS, CHUNK, SGU_HEAD))
        m = _merge(h, z_sgu, y_s5, y_fft, i, w_glu_bf, sgu_norm_g[i].astype(F32), sgu_w_bf, sb_full,
                   w_gate_bf, b_gate[i].astype(F32).reshape(1, N_BRANCH * d), w_branch_bf)

        rw32 = jnp.zeros((d, LANES), F32).at[:, :N_EXPERTS].set(router_w[i].astype(F32))
        rw_hi = rw32.astype(BF16)
        rw_pad = jnp.concatenate([rw_hi, (rw32 - rw_hi.astype(F32)).astype(BF16)], axis=1)
        rb_pad = jnp.full((1, LANES), -1e30, F32).at[0, :N_EXPERTS].set(router_b[i].astype(F32))
        x1, h2, top_idx, top_w = _outproj(x_lat, x_ctx, m, n_lat, i, w_out_bf, g1, norm2_g[i].astype(F32),
                                          sh2, sc2, rw_pad, rb_pad)
        dest, counts, pstart, padded, item_e, item_row0, item_nch = _routing_tables(
            top_idx[:, :TOP_K], n_items)
        xs = _dispatch(h2, dest, counts, pstart, padded, n_rows)
        ys = _experts(xs, i, moe_w_up, moe_b_up, moe_w_down, item_e, item_row0, item_nch)
        x_lat = _combine(ys, dest, top_w, top_idx, moe_b_down[i], x1, n_lat, g2, final_g.astype(F32), last)
        x_ctx = None

    return x_lat.reshape(bsz, n_lat, d).astype(x.dtype)
```

```python
import functools
import math

import jax
import jax.numpy as jnp
from jax import lax
from jax.experimental import pallas as pl
from jax.experimental.pallas import tpu as pltpu

F32 = jnp.float32
BF16 = jnp.bfloat16
I32 = jnp.int32

EPS = 1e-6
S5_WIDTH = 512
S5_GROUP = 16
S5_GROUPS = 32
S5_STATE = 64
FFT_WIDTH = 512
FFT_GROUPS = 4
FFT_GROUP = 128
SGU_WIDTH = 512
SGU_HEADS = 4
SGU_HEAD = 128
CHUNK = 128
N_BRANCH = 3
N_EXPERTS = 32
TOP_K = 4
SWIGLU_LIMIT = 7.0
SWIGLU_ALPHA = 1.702
N_MOD = 6

LANES = 128
S5_PER_BLOCK = LANES // S5_GROUP
VMEM_LIMIT_BYTES = 56 * 1024 * 1024

S5_T = 16
ROW_TILE = 256
MERGE_TILE = 384
MERGE_COLS = 512
FFT_KTILE = 256
DFT_BLK = 256
MOE_CH = 128
MOE_RMAX = 1536
MOE_TF = 512
DISP_TILE = 256


def _cparams(sem, vmem=VMEM_LIMIT_BYTES):
    return pltpu.CompilerParams(dimension_semantics=sem, vmem_limit_bytes=vmem)


def _gelu(x):
    return 0.5 * x * (1.0 + jnp.tanh(math.sqrt(2.0 / math.pi) * (x + 0.044715 * x * x * x)))


def _sigmoid(x):
    return 1.0 / (1.0 + jnp.exp(-x))


def _adaln_kernel(c_ref, w_ref, b_ref, o_ref):
    c = c_ref[...]
    s = c * _sigmoid(c)
    o_ref[...] = jnp.dot(s, w_ref[...], preferred_element_type=F32,
                         precision=lax.Precision.HIGHEST) + b_ref[...]


def _adaln(cond8, w_mod, b_mod):
    n_layers, d, nm = w_mod.shape
    tn = 1024
    return pl.pallas_call(
        _adaln_kernel,
        grid=(n_layers, nm // tn),
        in_specs=[pl.BlockSpec((8, d), lambda l, j: (0, 0)),
                  pl.BlockSpec((None, d, tn), lambda l, j: (l, 0, j)),
                  pl.BlockSpec((None, 1, tn), lambda l, j: (l, 0, j))],
        out_specs=pl.BlockSpec((None, 8, tn), lambda l, j: (l, 0, j)),
        out_shape=jax.ShapeDtypeStruct((n_layers, 8, nm), F32),
        compiler_params=_cparams(("arbitrary", "arbitrary")),
        name="adaln",
    )(cond8, w_mod, b_mod.reshape(n_layers, 1, nm))


def _token_rows(x_lat, x_ctx, tm):
    d = x_lat.shape[1]
    if x_ctx is None:
        return (x_lat,), [pl.BlockSpec((tm, d), lambda i: (i, 0))], lambda refs: refs[0][...]
    lat_tiles = x_lat.shape[0] // tm
    specs = [pl.BlockSpec((tm, d), lambda i: (jnp.minimum(i, lat_tiles - 1), 0)),
             pl.BlockSpec((tm, d), lambda i: (jnp.maximum(i - lat_tiles, 0), 0))]
    load = lambda refs: jnp.where(pl.program_id(0) >= lat_tiles, refs[1][...], refs[0][...])
    return (x_lat, x_ctx), specs, load


def _inproj_kernel(*refs, load_x):
    g_ref, sh_ref, sc_ref, w_ref, h_ref, zs_ref, zf_ref, zg_ref, zscr = refs[-9:]
    x = load_x(refs[:-9])
    y = x * lax.rsqrt(jnp.mean(x * x, axis=-1, keepdims=True) + EPS) * g_ref[...]
    h = (y * (1.0 + sc_ref[...]) + sh_ref[...]).astype(BF16)
    h_ref[...] = h
    z = jnp.dot(h, w_ref[...], preferred_element_type=F32)
    zf_ref[...] = z[:, S5_WIDTH:S5_WIDTH + FFT_WIDTH]
    zg_ref[...] = z[:, S5_WIDTH + FFT_WIDTH:]
    n_chunks = zs_ref.shape[1]
    for b in range(S5_WIDTH // LANES):
        zscr[b] = z[:, b * LANES:(b + 1) * LANES]
        for j in range(S5_T):
            zs_ref[b, :, j * LANES:(j + 1) * LANES] = \
                zscr[b, pl.ds(j, n_chunks, stride=S5_T), :].astype(BF16)


def _inproj(x_lat, x_ctx, n, n_lat, layer, g, sh, sc, w_in_bf):
    d = x_lat.shape[1]
    tm = ROW_TILE
    lat_tiles = n_lat // tm
    row = lambda i: (i, 0)
    kind = lambda i: (jnp.where(i >= lat_tiles, 1, 0), 0, 0)
    x_ops, x_specs, load_x = _token_rows(x_lat, x_ctx, tm)
    return pl.pallas_call(
        functools.partial(_inproj_kernel, load_x=load_x),
        grid=(n // tm,),
        in_specs=x_specs + [
                  pl.BlockSpec((1, d), lambda i: (0, 0)),
                  pl.BlockSpec((None, 1, d), kind),
                  pl.BlockSpec((None, 1, d), kind),
                  pl.BlockSpec((None, d, w_in_bf.shape[2]), lambda i: (layer, 0, 0))],
        out_specs=[pl.BlockSpec((tm, d), row),
                   pl.BlockSpec((S5_WIDTH // LANES, tm // S5_T, S5_T * LANES), lambda i: (0, i, 0)),
                   pl.BlockSpec((tm, FFT_WIDTH), row),
                   pl.BlockSpec((tm, 2 * SGU_WIDTH), row)],
        out_shape=[jax.ShapeDtypeStruct((n, d), BF16),
                   jax.ShapeDtypeStruct((S5_WIDTH // LANES, n // S5_T, S5_T * LANES), BF16),
                   jax.ShapeDtypeStruct((n, FFT_WIDTH), F32),
                   jax.ShapeDtypeStruct((n, 2 * SGU_WIDTH), F32)],
        scratch_shapes=[pltpu.VMEM((S5_WIDTH // LANES, tm, LANES), F32)],
        compiler_params=_cparams(("arbitrary",)),
        name="inproj",
    )(*x_ops, g.reshape(1, d), sh, sc, w_in_bf)


def _s5_operators(a_re, a_im, log_dt, b_re, b_im, c_re, c_im, d_skip, t_len, n_steps):
    lam = lax.complex(a_re.astype(F32), a_im.astype(F32))
    dt = jnp.exp(log_dt.astype(F32))[..., None]
    ldt = lam * dt
    tau = jnp.arange(t_len + 1, dtype=F32)
    pw = jnp.exp(ldt[:, :, None, :] * tau[None, None, :, None])
    a_bar = jnp.exp(ldt)
    bb = ((a_bar - 1.0) / lam)[..., None] * lax.complex(b_re.astype(F32), b_im.astype(F32))
    cc = lax.complex(c_re.astype(F32), c_im.astype(F32))
    k = jnp.real(jnp.einsum('dghp,dgtp,dgpk->dgthk', cc, pw[:, :, :t_len], bb))
    g = S5_GROUPS
    tw = t_len * S5_GROUP
    dd = d_skip.astype(F32).reshape(g, S5_GROUP)
    k0 = k[0][:, :1] + k[1][:, :1] + (jnp.eye(S5_GROUP, dtype=F32)[None] * dd[:, :, None])[:, None]
    kk = jnp.concatenate([k[1][:, :0:-1], k0, k[0][:, 1:]], axis=1).transpose(0, 3, 2, 1)
    m = jnp.stack([kk[..., t_len - 1 - j:2 * t_len - 1 - j] for j in range(t_len)], axis=2)
    m = m.reshape(g, tw, tw).astype(BF16)
    wf = pw[0][:, ::-1][:, 1:, :, None] * bb[0][:, None, :, :]
    wb = pw[1][:, :t_len, :, None] * bb[1][:, None, :, :]

    def state_cols(w):
        w = w.transpose(0, 3, 1, 2)
        return jnp.concatenate([jnp.real(w), jnp.imag(w)], axis=-1).reshape(g, tw, 2 * S5_STATE)

    wop = jnp.concatenate([state_cols(wf), state_cols(wb)], axis=-1).astype(BF16)
    gf = cc[0][:, None, :, :] * pw[0][:, 1:, None, :]
    gb = cc[1][:, None, :, :] * pw[1][:, ::-1][:, :t_len, None, :]

    def state_rows(gm):
        gm = gm.transpose(0, 3, 2, 1).reshape(g, S5_STATE, tw)
        return jnp.concatenate([jnp.real(gm), -jnp.imag(gm)], axis=1)

    vop = jnp.concatenate([state_rows(gf), state_rows(gb)], axis=1).astype(BF16)
    steps = (t_len * (2.0 ** jnp.arange(n_steps, dtype=F32)))
    am = jnp.exp(ldt[:, :, None, :] * steps[None, None, :, None])
    pvec = jnp.concatenate([jnp.real(am), jnp.real(am)], axis=-1)
    qvec = jnp.concatenate([-jnp.imag(am), jnp.imag(am)], axis=-1)
    pq = jnp.stack([pvec, qvec], axis=3)
    pq = pq.transpose(1, 0, 2, 3, 4).reshape(g, 4 * n_steps, 2 * S5_STATE)
    return m, wop, vop, pq


def _s5_kernel(zs_ref, sel_ref, selt_ref, m_ref, wop_ref, vop_ref, pq_ref, y_ref, *, lat_chunks, n_steps):
    nc = zs_ref.shape[0]
    sw = 2 * S5_STATE
    u = jnp.dot(zs_ref[...], sel_ref[...], preferred_element_type=F32).astype(BF16)
    xloc = jnp.dot(u, wop_ref[...], preferred_element_type=F32)
    row = lax.broadcasted_iota(I32, (nc, sw), 0)
    pos_f = jnp.where(row >= lat_chunks, row - lat_chunks, row + (nc - lat_chunks))
    pos_b = (nc - 1) - row

    def chain(xl, pos, base, forward):
        x = jnp.where(pos >= 1, pltpu.roll(xl, 1 if forward else nc - 1, 0), 0.0)
        for m in range(n_steps):
            k = 1 << m
            s = jnp.where(pos >= k, pltpu.roll(x, k if forward else nc - k, 0), 0.0)
            pv = pq_ref[base + 2 * m:base + 2 * m + 1, :]
            qv = pq_ref[base + 2 * m + 1:base + 2 * m + 2, :]
            x = x + pv * s + qv * pltpu.roll(s, S5_STATE, 1)
        return x

    xf = chain(xloc[:, :sw], pos_f, 0, True)
    xb = chain(xloc[:, sw:], pos_b, 2 * n_steps, False)
    xin = jnp.concatenate([xf, xb], axis=1).astype(BF16)
    y = (jnp.dot(u, m_ref[...], preferred_element_type=F32)
         + jnp.dot(xin, vop_ref[...], preferred_element_type=F32))
    contrib = jnp.dot(y.astype(BF16), selt_ref[...], preferred_element_type=F32)

    @pl.when(pl.program_id(0) % S5_PER_BLOCK == 0)
    def _():
        y_ref[...] = contrib

    @pl.when(pl.program_id(0) % S5_PER_BLOCK != 0)
    def _():
        y_ref[...] += contrib


def _s5_select_tables(t_len):
    r = jnp.arange(t_len * LANES, dtype=I32)[None, :, None]
    c = jnp.arange(t_len * S5_GROUP, dtype=I32)[None, None, :]
    gl = jnp.arange(S5_PER_BLOCK, dtype=I32)[:, None, None]
    sel = (r == (c % t_len) * LANES + gl * S5_GROUP + c // t_len).astype(BF16)
    return sel, sel.transpose(0, 2, 1)


def _s5_mix(zs, n_lat, sel, selt, m, wop, vop, pq, t_len, n_steps):
    nb, nc, bw = zs.shape
    g = S5_GROUPS
    tw = t_len * S5_GROUP
    blk = lambda i: (i // S5_PER_BLOCK, 0, 0)
    loc = lambda i: (i % S5_PER_BLOCK, 0, 0)
    return pl.pallas_call(
        functools.partial(_s5_kernel, lat_chunks=n_lat // t_len, n_steps=n_steps),
        grid=(g,),
        in_specs=[pl.BlockSpec((None, nc, bw), blk),
                  pl.BlockSpec((None, bw, tw), loc),
                  pl.BlockSpec((None, tw, bw), loc),
                  pl.BlockSpec((None, tw, tw), lambda i: (i, 0, 0)),
                  pl.BlockSpec((None, tw, 4 * S5_STATE), lambda i: (i, 0, 0)),
                  pl.BlockSpec((None, 4 * S5_STATE, tw), lambda i: (i, 0, 0)),
                  pl.BlockSpec((None, 4 * n_steps, 2 * S5_STATE), lambda i: (i, 0, 0))],
        out_specs=pl.BlockSpec((None, nc, bw), blk),
        out_shape=jax.ShapeDtypeStruct((nb, nc, bw), F32),
        compiler_params=_cparams(("arbitrary",)),
        name="s5_mix",
    )(zs, sel, selt, m, wop, vop, pq)


def _dft_tables(n_pos):
    half = n_pos // 2
    blk = min(DFT_BLK, half)
    kk = jnp.arange(half, dtype=I32)[:, None]
    ang = lambda prod: (2.0 * math.pi / n_pos) * (prod % n_pos).astype(F32)
    base = ang(kk * jnp.arange(blk, dtype=I32)[None, :])
    phase = ang(kk * (blk * jnp.arange(half // blk, dtype=I32))[None, :])
    return jnp.cos(base), jnp.sin(base), jnp.cos(phase), jnp.sin(phase)


def _channel_tables():
    q = jnp.arange(FFT_GROUP, dtype=I32)
    ang = (2.0 * math.pi / FFT_GROUP) * ((q[:, None] * q[None, :]) % FFT_GROUP).astype(F32)
    return jnp.cos(ang).astype(BF16), jnp.sin(ang).astype(BF16)


def _chan_dft_kernel(zl_ref, zh_ref, cc_ref, sc_ref, rc_ref, rs_ref):
    w = zl_ref.shape[1]
    for hf, zref in enumerate((zl_ref, zh_ref)):
        z = zref[...].astype(BF16)
        for g in range(FFT_GROUPS):
            sl = slice(g * FFT_GROUP, (g + 1) * FFT_GROUP)
            osl = slice(hf * w + g * FFT_GROUP, hf * w + (g + 1) * FFT_GROUP)
            rc_ref[:, osl] = jnp.dot(z[:, sl], cc_ref[...], preferred_element_type=F32).astype(BF16)
            rs_ref[:, osl] = jnp.dot(z[:, sl], sc_ref[...], preferred_element_type=F32).astype(BF16)


def _posdft_kernel(bc_ref, bs_ref, pc_ref, ps_ref, alt_ref, rc_ref, rs_ref, o_ref, *, scale):
    tk, w = o_ref.shape
    bc = bc_ref[...]
    bs = bs_ref[...]
    cparts, sparts = [], []
    for t in range(pc_ref.shape[1]):
        pc = pc_ref[:, t:t + 1]
        ps = ps_ref[:, t:t + 1]
        cparts.append((pc * bc - ps * bs).astype(BF16))
        sparts.append((ps * bc + pc * bs).astype(BF16))
    cm = jnp.concatenate(cparts, axis=1)
    sm = jnp.concatenate(sparts, axis=1)
    alt = jnp.where(pl.program_id(0) == 1, alt_ref[...], jnp.ones(alt_ref.shape, BF16))
    krow = lax.broadcasted_iota(I32, (tk, w), 0) + pl.program_id(1) * tk
    sign = jnp.where((krow & 1) == 0, 1.0, -1.0)
    u = (jnp.dot(cm * alt, rc_ref[...], preferred_element_type=F32)
         - jnp.dot(sm * alt, rs_ref[...], preferred_element_type=F32))
    o_ref[...] = (u[:, :w] + sign * u[:, w:]) * scale


def _ctx_dft_kernel(z_ref, cc_ref, sc_ref, cm_ref, sm_ref, lat_ref, o_ref, *, scale):
    del lat_ref
    z = z_ref[...].astype(BF16)
    for g in range(FFT_GROUPS):
        sl = slice(g * FFT_GROUP, (g + 1) * FFT_GROUP)
        zc = jnp.dot(z[:, sl], cc_ref[...], preferred_element_type=F32).astype(BF16)
        zs = jnp.dot(z[:, sl], sc_ref[...], preferred_element_type=F32).astype(BF16)
        o_ref[:, sl] = (jnp.dot(cm_ref[...], zc, preferred_element_type=F32)
                        - jnp.dot(sm_ref[...], zs, preferred_element_type=F32)) * scale


def _fourier_latent(z_all, n, tables, cc, sc):
    bcos, bsin, pcos, psin = tables
    w = z_all.shape[1]
    half = n // 2
    assert half % 2 == 0
    tr = min(512, half)
    nb = half // tr
    const = lambda i: (0, 0)
    rc, rs = pl.pallas_call(
        _chan_dft_kernel,
        grid=(nb,),
        in_specs=[pl.BlockSpec((tr, w), lambda i: (i, 0)),
                  pl.BlockSpec((tr, w), lambda i: (i + nb, 0)),
                  pl.BlockSpec((FFT_GROUP, FFT_GROUP), const),
                  pl.BlockSpec((FFT_GROUP, FFT_GROUP), const)],
        out_specs=[pl.BlockSpec((tr, 2 * w), lambda i: (i, 0))] * 2,
        out_shape=[jax.ShapeDtypeStruct((half, 2 * w), BF16)] * 2,
        compiler_params=_cparams(("arbitrary",)),
        name="fft_chan",
    )(z_all, z_all, cc, sc)
    tk = min(FFT_KTILE, half)
    nk = half // tk
    alt = jnp.where(jnp.arange(half) % 2 == 0, 1.0, -1.0).astype(BF16).reshape(1, half)
    out = pl.pallas_call(
        functools.partial(_posdft_kernel, scale=1.0 / math.sqrt(n * FFT_GROUP)),
        grid=(2, nk),
        in_specs=[pl.BlockSpec((tk, bcos.shape[1]), lambda q, i: (i, 0)),
                  pl.BlockSpec((tk, bcos.shape[1]), lambda q, i: (i, 0)),
                  pl.BlockSpec((tk, pcos.shape[1]), lambda q, i: (i, 0)),
                  pl.BlockSpec((tk, pcos.shape[1]), lambda q, i: (i, 0)),
                  pl.BlockSpec((1, half), lambda q, i: (0, 0)),
                  pl.BlockSpec((half, 2 * w), lambda q, i: (0, 0), pipeline_mode=pl.Buffered(1)),
                  pl.BlockSpec((half, 2 * w), lambda q, i: (0, 0), pipeline_mode=pl.Buffered(1))],
        out_specs=pl.BlockSpec((tk, w), lambda q, i: (q * nk + i, 0)),
        out_shape=jax.ShapeDtypeStruct((z_all.shape[0], w), F32),
        compiler_params=_cparams(("arbitrary", "arbitrary")),
        name="fft_posdft",
    )(bcos, bsin, pcos, psin, alt, rc, rs)
    return out


def _fourier_ctx(z_all, y_all, n_lat, cc, sc):
    n_all, w = z_all.shape
    n = n_all - n_lat
    assert n_lat % n == 0
    q = jnp.arange(n, dtype=I32)
    ang = (2.0 * math.pi / n) * ((q[:, None] * q[None, :]) % n).astype(F32)
    cm, sm = jnp.cos(ang).astype(BF16), jnp.sin(ang).astype(BF16)
    full = lambda s: pl.BlockSpec(s, lambda i: (0,) * len(s))
    ctx_rows = pl.BlockSpec((n, w), lambda i: (n_lat // n, 0))
    return pl.pallas_call(
        functools.partial(_ctx_dft_kernel, scale=1.0 / math.sqrt(n * FFT_GROUP)),
        grid=(1,),
        in_specs=[ctx_rows, full((FFT_GROUP, FFT_GROUP)), full((FFT_GROUP, FFT_GROUP)),
                  full((n, n)), full((n, n)), pl.BlockSpec(memory_space=pl.ANY)],
        out_specs=ctx_rows,
        out_shape=jax.ShapeDtypeStruct((n_all, w), F32),
        input_output_aliases={5: 0},
        compiler_params=_cparams(("arbitrary",)),
        name="fft_ctx",
    )(z_all, cc, sc, cm, sm, y_all)


def _merge_kernel(h_ref, zg_ref, ys_ref, yf_ref, wglu_ref, gv_ref, sw_ref, sb_ref,
                  wg0_ref, wg1_ref, wg2_ref, bg0_ref, bg1_ref, bg2_ref,
                  wb0_ref, wb1_ref, wb2_ref, m_ref, feats, yscr):
    j = pl.program_id(1)
    tm = h_ref.shape[0]

    @pl.when(j == 0)
    def _():
        n_chunks = ys_ref.shape[1]
        for b in range(S5_WIDTH // LANES):
            for t in range(S5_T):
                yscr[b, pl.ds(t, n_chunks, stride=S5_T), :] = ys_ref[b, :, t * LANES:(t + 1) * LANES]
        y = _gelu(jnp.concatenate([yscr[b] for b in range(S5_WIDTH // LANES)], axis=1))
        feats[0] = (y * _sigmoid(jnp.dot(y.astype(BF16), wglu_ref[...],
                                         preferred_element_type=F32))).astype(BF16)
        feats[1] = yf_ref[...].astype(BF16)
        gz = _gelu(zg_ref[...])
        u = gz[:, :SGU_WIDTH]
        v = gz[:, SGU_WIDTH:]
        v = v * lax.rsqrt(jnp.mean(v * v, axis=-1, keepdims=True) + EPS) * gv_ref[...]
        vb = v.astype(BF16)
        for c in range(tm // CHUNK):
            rs = slice(c * CHUNK, (c + 1) * CHUNK)
            for hd in range(SGU_HEADS):
                cs = slice(hd * SGU_HEAD, (hd + 1) * SGU_HEAD)
                s = jnp.dot(sw_ref[hd], vb[rs, cs], preferred_element_type=F32) + sb_ref[hd]
                feats[2, rs, cs] = (u[rs, cs] * s).astype(BF16)

    h = h_ref[...]
    acc = None
    for k, (wg, bg, wb) in enumerate(((wg0_ref, bg0_ref, wb0_ref), (wg1_ref, bg1_ref, wb1_ref),
                                      (wg2_ref, bg2_ref, wb2_ref))):
        gate = _sigmoid(jnp.dot(h, wg[...], preferred_element_type=F32) + bg[...])
        br = jnp.dot(feats[k], wb[...], preferred_element_type=F32)
        acc = gate * br if acc is None else acc + gate * br
    m_ref[...] = acc.astype(BF16)


def _merge(h, zg, ys, yf, layer, wglu_bf, gv, sw_bf, sb_full, wgate_bf, bgate, wbranch_bf):
    n, d = h.shape
    tm, tn = MERGE_TILE, MERGE_COLS
    nj = d // tn
    row = lambda i, j: (i, 0)
    const2 = lambda i, j: (0, 0)
    const3 = lambda i, j: (0, 0, 0)
    gate_spec = lambda k: pl.BlockSpec((None, d, tn), lambda i, j: (layer, 0, k * nj + j))
    bias_spec = lambda k: pl.BlockSpec((1, tn), lambda i, j: (0, k * nj + j))
    br_spec = lambda k: pl.BlockSpec((None, None, S5_WIDTH, tn), lambda i, j: (layer, k, 0, j))
    return pl.pallas_call(
        _merge_kernel,
        grid=(n // tm, nj),
        in_specs=[pl.BlockSpec((tm, d), row),
                  pl.BlockSpec((tm, 2 * SGU_WIDTH), row),
                  pl.BlockSpec((S5_WIDTH // LANES, tm // S5_T, S5_T * LANES), lambda i, j: (0, i, 0)),
                  pl.BlockSpec((tm, FFT_WIDTH), row),
                  pl.BlockSpec((None, S5_WIDTH, S5_WIDTH), lambda i, j: (layer, 0, 0)),
                  pl.BlockSpec((1, SGU_WIDTH), const2),
                  pl.BlockSpec((None, SGU_HEADS, CHUNK, CHUNK), lambda i, j: (layer, 0, 0, 0)),
                  pl.BlockSpec((SGU_HEADS, CHUNK, SGU_HEAD), const3),
                  gate_spec(0), gate_spec(1), gate_spec(2),
                  bias_spec(0), bias_spec(1), bias_spec(2),
                  br_spec(0), br_spec(1), br_spec(2)],
        out_specs=pl.BlockSpec((tm, tn), lambda i, j: (i, j)),
        out_shape=jax.ShapeDtypeStruct((n, d), BF16),
        scratch_shapes=[pltpu.VMEM((N_BRANCH, tm, S5_WIDTH), BF16),
                        pltpu.VMEM((S5_WIDTH // LANES, tm, LANES), F32)],
        compiler_params=_cparams(("arbitrary", "arbitrary")),
        name="gated_merge",
    )(h, zg, ys, yf, wglu_bf, gv.reshape(1, SGU_WIDTH), sw_bf, sb_full,
      wgate_bf, wgate_bf, wgate_bf, bgate, bgate, bgate, wbranch_bf, wbranch_bf, wbranch_bf)


def _outproj_kernel(*refs, load_x):
    (m_ref, wo_ref, g1_ref, g2_ref, sh_ref, sc_ref, rw_ref, rb_ref,
     x1_ref, h2_ref, ti_ref, tw_ref) = refs[-12:]
    x1 = load_x(refs[:-12]) + g1_ref[...] * jnp.dot(m_ref[...], wo_ref[...], preferred_element_type=F32)
    x1_ref[...] = x1
    y = x1 * lax.rsqrt(jnp.mean(x1 * x1, axis=-1, keepdims=True) + EPS) * g2_ref[...]
    h2 = y * (1.0 + sc_ref[...]) + sh_ref[...]
    h2_ref[...] = h2
    h_hi = h2.astype(BF16)
    h_lo = (h2 - h_hi.astype(F32)).astype(BF16)
    p_hi = jnp.dot(h_hi, rw_ref[...], preferred_element_type=F32)
    p_lo = jnp.dot(h_lo, rw_ref[:, :LANES], preferred_element_type=F32)
    logits = p_hi[:, :LANES] + p_hi[:, LANES:] + p_lo + rb_ref[...]
    lane = lax.broadcasted_iota(I32, logits.shape, 1)
    vals, idxs = [], []
    for _ in range(TOP_K):
        mx = jnp.max(logits, axis=-1, keepdims=True)
        am = jnp.min(jnp.where(logits == mx, lane, LANES), axis=-1, keepdims=True)
        vals.append(mx)
        idxs.append(am)
        logits = jnp.where(lane == am, -jnp.inf, logits)
    ex = [jnp.exp(v - vals[0]) for v in vals]
    den = ex[0] + ex[1] + ex[2] + ex[3]
    ti = jnp.zeros(lane.shape, I32)
    tw = jnp.zeros(lane.shape, F32)
    for k in range(TOP_K):
        ti = jnp.where(lane == k, idxs[k], ti)
        tw = jnp.where(lane == k, ex[k] / den, tw)
    ti_ref[...] = ti
    tw_ref[...] = tw


def _outproj(x_lat, x_ctx, m, n_lat, layer, wo_bf, g1, g2n, sh2, sc2, rw_pad, rb_pad):
    n, d = m.shape
    tm = ROW_TILE
    lat_tiles = n_lat // tm
    row = lambda i: (i, 0)
    kind = lambda i: (jnp.where(i >= lat_tiles, 1, 0), 0, 0)
    const = lambda i: (0, 0)
    x_ops, x_specs, load_x = _token_rows(x_lat, x_ctx, tm)
    return pl.pallas_call(
        functools.partial(_outproj_kernel, load_x=load_x),
        grid=(n // tm,),
        in_specs=x_specs + [
                  pl.BlockSpec((tm, d), row),
                  pl.BlockSpec((None, d, d), lambda i: (layer, 0, 0)),
                  pl.BlockSpec((None, 1, d), kind),
                  pl.BlockSpec((1, d), const),
                  pl.BlockSpec((None, 1, d), kind),
                  pl.BlockSpec((None, 1, d), kind),
                  pl.BlockSpec((d, 2 * LANES), const),
                  pl.BlockSpec((1, LANES), const)],
        out_specs=[pl.BlockSpec((tm, d), row), pl.BlockSpec((tm, d), row),
                   pl.BlockSpec((tm, LANES), row), pl.BlockSpec((tm, LANES), row)],
        out_shape=[jax.ShapeDtypeStruct((n, d), F32), jax.ShapeDtypeStruct((n, d), F32),
                   jax.ShapeDtypeStruct((n, LANES), I32), jax.ShapeDtypeStruct((n, LANES), F32)],
        compiler_params=_cparams(("arbitrary",)),
        name="outproj_router",
    )(*x_ops, m, wo_bf, g1, g2n.reshape(1, d), sh2, sc2, rw_pad, rb_pad)


def _routing_tables(top_idx, n_items):
    flat_e = top_idx.reshape(-1)
    onehot = (flat_e[:, None] == jnp.arange(N_EXPERTS, dtype=I32)[None, :]).astype(I32)
    csum = jnp.cumsum(onehot, axis=0)
    rank = jnp.sum((csum - onehot) * onehot, axis=1)
    counts = csum[-1]
    padded = (counts + MOE_CH - 1) // MOE_CH * MOE_CH
    pend = jnp.cumsum(padded)
    pstart = pend - padded
    dest = jnp.sum(onehot * pstart[None, :], axis=1) + rank
    per_e = (padded + MOE_RMAX - 1) // MOE_RMAX
    iend = jnp.cumsum(per_e)
    istart = iend - per_e
    total = iend[-1]
    t = jnp.arange(n_items, dtype=I32)
    valid = t < total
    tc = jnp.minimum(t, total - 1)
    e_of = jnp.minimum(jnp.sum((tc[:, None] >= iend[None, :]).astype(I32), axis=1), N_EXPERTS - 1)
    local = tc - istart[e_of]
    row0 = pstart[e_of] + local * MOE_RMAX
    nch = jnp.clip((padded[e_of] - local * MOE_RMAX) // MOE_CH, 0, MOE_RMAX // MOE_CH)
    nch = jnp.where(valid, nch, 0)
    return (dest.astype(I32), counts.astype(I32), pstart.astype(I32), padded.astype(I32),
            e_of.astype(I32), row0.astype(I32), nch.astype(I32))


def _dispatch_kernel(cnt_ref, pst_ref, pad_ref, dest_ref, h_ref, xs_ref, zrow, sem, zsem):
    i = pl.program_id(0)
    tq = h_ref.shape[0]

    @pl.when(i == 0)
    def _():
        zrow[...] = jnp.zeros(zrow.shape, zrow.dtype)

        def per_expert(e, carry):
            first = pst_ref[e] + cnt_ref[e]
            npad = pad_ref[e] - cnt_ref[e]

            def start(r, c):
                pltpu.make_async_copy(zrow.at[pl.ds(0, 1)], xs_ref.at[pl.ds(first + r, 1)], zsem).start()
                return c

            def wait(r, c):
                pltpu.make_async_copy(zrow.at[pl.ds(0, 1)], xs_ref.at[pl.ds(first + r, 1)], zsem).wait()
                return c

            lax.fori_loop(0, npad, start, 0)
            lax.fori_loop(0, npad, wait, 0)
            return carry

        lax.fori_loop(0, N_EXPERTS, per_expert, 0)

    def start(t, c):
        src = h_ref.at[pl.ds(t, 1)]
        for k in range(TOP_K):
            pltpu.make_async_copy(src, xs_ref.at[pl.ds(dest_ref[0, t * TOP_K + k], 1)], sem).start()
        return c

    lax.fori_loop(0, tq, start, 0, unroll=2)
    for _ in range(TOP_K):
        pltpu.make_async_copy(h_ref, xs_ref.at[pl.ds(0, tq)], sem).wait()


def _dispatch(h2, dest, counts, pstart, padded, n_rows):
    n, d = h2.shape
    tq = DISP_TILE
    nt = n // tq
    dest3 = dest.reshape(nt, 1, tq * TOP_K)
    return pl.pallas_call(
        _dispatch_kernel,
        grid_spec=pltpu.PrefetchScalarGridSpec(
            num_scalar_prefetch=3,
            grid=(nt,),
            in_specs=[pl.BlockSpec((None, 1, tq * TOP_K), lambda i, *_: (i, 0, 0), memory_space=pltpu.SMEM),
                      pl.BlockSpec((tq, d), lambda i, *_: (i, 0))],
            out_specs=pl.BlockSpec(memory_space=pl.ANY),
            scratch_shapes=[pltpu.VMEM((8, d), F32), pltpu.SemaphoreType.DMA(()), pltpu.SemaphoreType.DMA(())]),
        out_shape=jax.ShapeDtypeStruct((n_rows, d), F32),
        compiler_params=_cparams(("arbitrary",)),
        name="moe_dispatch",
    )(counts, pstart, padded, dest3, h2)


def _experts_kernel(ie_ref, ir_ref, in_ref, xs_ref, wg_ref, wu_ref, wd_ref, bg_ref, bu_ref,
                    ys_ref, xstage, xb, acc, pending, sem_in, sem_out, *, units_per_step):
    t = pl.program_id(0)
    j = pl.program_id(1)
    nf = pl.num_programs(1)
    nch = in_ref[t]
    row0 = ir_ref[t]
    ch = MOE_CH

    def hbm_rows(c):
        return pl.ds(pl.multiple_of(row0 + c * ch, ch), ch)

    def vmem_rows(c, size=ch):
        return pl.ds(pl.multiple_of(c * ch, ch), size)

    def in_copy(c, slot):
        return pltpu.make_async_copy(xs_ref.at[hbm_rows(c)], xstage.at[slot], sem_in.at[slot])

    def out_copy(c):
        return pltpu.make_async_copy(acc.at[vmem_rows(c)], ys_ref.at[hbm_rows(c)], sem_out)

    def drain():
        def body(c, carry):
            out_copy(0).wait()
            return carry
        lax.fori_loop(0, pending[0], body, 0)
        pending[0] = 0

    cur = t & 1
    n_items = pl.num_programs(0)
    t_next = jnp.minimum(t + 1, n_items - 1)
    nch_next = jnp.where(t + 1 < n_items, in_ref[t_next], 0)
    row0_next = ir_ref[t_next]

    def next_copy(u, k):
        src = xs_ref.at[pl.ds(pl.multiple_of(row0_next + u * ch, ch), ch)]
        return pltpu.make_async_copy(src, xstage.at[k], sem_in.at[k])

    @pl.when(jnp.logical_and(t == 0, j == 0))
    def _():
        pending[0] = 0

    @pl.when(jnp.logical_and(jnp.logical_and(t == 0, j == 0), nch > 0))
    def _():
        in_copy(0, 0).start()

        def body(c, carry):
            slot = c & 1

            @pl.when(c + 1 < nch)
            def _():
                in_copy(c + 1, 1 - slot).start()

            in_copy(c, slot).wait()
            xb[0, vmem_rows(c), :] = xstage[slot].astype(BF16)
            return carry

        lax.fori_loop(0, nch, body, 0)

    for k in range(units_per_step):
        @pl.when(j * units_per_step + k < nch_next)
        def _():
            next_copy(j * units_per_step + k, k).start()

    @pl.when(nch > 0)
    def _():
        def partial_out(c, size):
            rows = vmem_rows(c, size)
            x = xb[cur, rows, :]
            gate = jnp.dot(x, wg_ref[...].astype(BF16), preferred_element_type=F32) + bg_ref[...]
            up = jnp.dot(x, wu_ref[...].astype(BF16), preferred_element_type=F32) + bu_ref[...]
            gate = jnp.minimum(gate, SWIGLU_LIMIT)
            up = jnp.clip(up, -SWIGLU_LIMIT, SWIGLU_LIMIT)
            act = gate * _sigmoid(SWIGLU_ALPHA * gate) * (up + 1.0)
            return rows, jnp.dot(act.astype(BF16), wd_ref[...].astype(BF16), preferred_element_type=F32)

        def accumulate(first):
            def store(rows, part):
                if first:
                    acc[rows, :] = part
                else:
                    acc[rows, :] += part

            def body(c4, carry):
                store(*partial_out(4 * c4, 4 * ch))
                return carry

            quads = lax.shift_right_logical(nch, 2)
            lax.fori_loop(0, quads, body, 0)

            @pl.when((nch & 2) != 0)
            def _():
                store(*partial_out(4 * quads, 2 * ch))

            @pl.when((nch & 1) != 0)
            def _():
                store(*partial_out(nch - 1, ch))

        @pl.when(j == 0)
        def _():
            drain()
            accumulate(True)

        @pl.when(j > 0)
        def _():
            accumulate(False)

        @pl.when(j == nf - 1)
        def _():
            def body(c, carry):
                out_copy(c).start()
                return carry
            lax.fori_loop(0, nch, body, 0)
            pending[0] = nch

    for k in range(units_per_step):
        @pl.when(j * units_per_step + k < nch_next)
        def _():
            next_copy(j * units_per_step + k, k).wait()
            xb[1 - cur, vmem_rows(j * units_per_step + k), :] = xstage[k].astype(BF16)

    @pl.when(jnp.logical_and(t == n_items - 1, j == nf - 1))
    def _():
        drain()


def _experts(xs, layer, w_up, b_up, w_down, item_e, item_row0, item_nch):
    n_rows, d = xs.shape
    n_layers, n_exp, _, two_f = w_up.shape
    f = two_f // 2
    tf = MOE_TF
    nf = f // tf
    assert nf >= 2
    units_per_step = max(2, -(-(MOE_RMAX // MOE_CH) // nf))
    n_items = item_e.shape[0]
    b_up4 = b_up.reshape(n_layers, n_exp, 1, two_f)

    def jj(t, j, ie, ir, inn):
        return jnp.where(inn[t] > 0, j, nf - 1)

    return pl.pallas_call(
        functools.partial(_experts_kernel, units_per_step=units_per_step),
        grid_spec=pltpu.PrefetchScalarGridSpec(
            num_scalar_prefetch=3,
            grid=(n_items, nf),
            in_specs=[pl.BlockSpec(memory_space=pl.ANY),
                      pl.BlockSpec((None, None, d, tf), lambda t, j, ie, ir, inn: (layer, ie[t], 0, jj(t, j, ie, ir, inn))),
                      pl.BlockSpec((None, None, d, tf), lambda t, j, ie, ir, inn: (layer, ie[t], 0, nf + jj(t, j, ie, ir, inn))),
                      pl.BlockSpec((None, None, tf, d), lambda t, j, ie, ir, inn: (layer, ie[t], jj(t, j, ie, ir, inn), 0)),
                      pl.BlockSpec((None, None, 1, tf), lambda t, j, ie, ir, inn: (layer, ie[t], 0, jj(t, j, ie, ir, inn))),
                      pl.BlockSpec((None, None, 1, tf), lambda t, j, ie, ir, inn: (layer, ie[t], 0, nf + jj(t, j, ie, ir, inn)))],
            out_specs=pl.BlockSpec(memory_space=pl.ANY),
            scratch_shapes=[pltpu.VMEM((units_per_step, MOE_CH, d), F32),
                            pltpu.VMEM((2, MOE_RMAX, d), BF16),
                            pltpu.VMEM((MOE_RMAX, d), F32),
                            pltpu.SMEM((1,), I32),
                            pltpu.SemaphoreType.DMA((units_per_step,)),
                            pltpu.SemaphoreType.DMA(())]),
        out_shape=jax.ShapeDtypeStruct((n_rows, d), F32),
        compiler_params=_cparams(("arbitrary", "arbitrary")),
        name="moe_experts",
    )(item_e, item_row0, item_nch, xs, w_up, w_up, w_down, b_up4, b_up4)


def _combine_kernel(dc_ref, dn_ref, tw_ref, ti_ref, x1_ref, g2_ref, fg_ref, bd_ref, ys_ref, o_ref,
                    gbuf, sem, *, final):
    i = pl.program_id(0)
    n = pl.num_programs(0)
    tq = x1_ref.shape[0]

    def issue(dref, slot):
        def body(t, c):
            for k in range(TOP_K):
                pltpu.make_async_copy(ys_ref.at[pl.ds(dref[0, t * TOP_K + k], 1)],
                                      gbuf.at[slot, k, pl.ds(t, 1)], sem.at[slot]).start()
            return c
        lax.fori_loop(0, tq, body, 0, unroll=2)

    slot = i & 1

    @pl.when(i == 0)
    def _():
        issue(dc_ref, 0)

    @pl.when(i + 1 < n)
    def _():
        issue(dn_ref, 1 - slot)

    for k in range(TOP_K):
        pltpu.make_async_copy(ys_ref.at[pl.ds(0, tq)], gbuf.at[slot, k], sem.at[slot]).wait()
    tw = tw_ref[...]
    ti = ti_ref[...]
    lane = lax.broadcasted_iota(I32, tw.shape, 1)
    ew = jnp.zeros(tw.shape, F32)
    for k in range(TOP_K):
        ew = ew + jnp.where(lane == ti[:, k:k + 1], tw[:, k:k + 1], 0.0)
    y = jnp.dot(ew.astype(BF16), bd_ref[...], preferred_element_type=F32)
    for k in range(TOP_K):
        y = y + tw[:, k:k + 1] * gbuf[slot, k]
    x2 = x1_ref[...] + g2_ref[...] * y
    if final:
        x2 = x2 * lax.rsqrt(jnp.mean(x2 * x2, axis=-1, keepdims=True) + EPS) * fg_ref[...]
    o_ref[...] = x2


def _combine(ys, dest, top_w, top_idx, b_down, x1, n_lat, g2, final_g, final):
    n, d = x1.shape
    bd_pad = jnp.zeros((LANES, d), BF16).at[:N_EXPERTS].set(b_down.astype(BF16))
    tq = DISP_TILE
    lat_tiles = n_lat // tq
    dest3 = dest.reshape(n // tq, 1, tq * TOP_K)
    if final:
        n = n_lat
    nt = n // tq
    kind = lambda i: (jnp.where(i >= lat_tiles, 1, 0), 0, 0)
    return pl.pallas_call(
        functools.partial(_combine_kernel, final=final),
        grid=(nt,),
        in_specs=[pl.BlockSpec((None, 1, tq * TOP_K), lambda i: (i, 0, 0), memory_space=pltpu.SMEM),
                  pl.BlockSpec((None, 1, tq * TOP_K), lambda i: (jnp.minimum(i + 1, nt - 1), 0, 0),
                               memory_space=pltpu.SMEM),
                  pl.BlockSpec((tq, LANES), lambda i: (i, 0)),
                  pl.BlockSpec((tq, LANES), lambda i: (i, 0)),
                  pl.BlockSpec((tq, d), lambda i: (i, 0)),
                  pl.BlockSpec((None, 1, d), kind),
                  pl.BlockSpec((1, d), lambda i: (0, 0)),
                  pl.BlockSpec((LANES, d), lambda i: (0, 0)),
                  pl.BlockSpec(memory_space=pl.ANY)],
        out_specs=pl.BlockSpec((tq, d), lambda i: (i, 0)),
        out_shape=jax.ShapeDtypeStruct((n, d), F32),
        scratch_shapes=[pltpu.VMEM((2, TOP_K, tq, d), F32), pltpu.SemaphoreType.DMA((2,))],
        compiler_params=_cparams(("arbitrary",)),
        name="moe_combine",
    )(dest3, dest3, top_w, top_idx, x1, g2, final_g.reshape(1, d), bd_pad, ys)


def kernel(x, c, ctx, c_ctx, w_mod, b_mod, norm1_g, norm2_g, w_in, s5_a_re, s5_a_im, s5_log_dt,
           s5_b_re, s5_b_im, s5_c_re, s5_c_im, s5_d, s5_w_glu, sgu_norm_g, sgu_w, sgu_b, w_branch,
           w_gate, b_gate, w_out, router_w, router_b, moe_w_up, moe_b_up, moe_w_down, moe_b_down,
           final_g):
    bsz, n_lat, d = x.shape
    n_ctx = ctx.shape[1]
    assert bsz == 1
    n_layers = w_mod.shape[0]
    n = n_lat + n_ctx
    x_lat, x_ctx = x.reshape(n_lat, d).astype(F32), ctx.reshape(n_ctx, d).astype(F32)
    w_in_bf, w_out_bf, w_gate_bf, w_branch_bf = (t.astype(BF16) for t in (w_in, w_out, w_gate, w_branch))
    w_glu_bf, sgu_w_bf = s5_w_glu.astype(BF16), sgu_w.astype(BF16)

    cond8 = jnp.zeros((8, d), F32).at[0].set(c[0]).at[1].set(c_ctx)
    mods = _adaln(cond8, w_mod, b_mod)[:, :2, :].reshape(n_layers, 2, N_MOD, 1, d)

    n_chunks = n // S5_T
    n_steps = max(1, (n_chunks - 1).bit_length())
    dft_tables = _dft_tables(n_lat)
    cc, sc = _channel_tables()
    sel, selt = _s5_select_tables(S5_T)
    n_assign = n * TOP_K
    n_rows = n_assign + N_EXPERTS * MOE_CH
    n_items = -(-n_assign // MOE_RMAX) + N_EXPERTS

    for i in range(n_layers):
        last = i == n_layers - 1
        sh1, sc1, g1, sh2, sc2, g2 = [mods[i, :, k] for k in range(N_MOD)]

        h, z_s5, z_fft, z_sgu = _inproj(x_lat, x_ctx, n, n_lat, i, norm1_g[i], sh1, sc1, w_in_bf)
        mtz, wop, vop, pq = _s5_operators(s5_a_re[i], s5_a_im[i], s5_log_dt[i], s5_b_re[i], s5_b_im[i],
                                     s5_c_re[i], s5_c_im[i], s5_d[i], S5_T, n_steps)
        y_s5 = _s5_mix(z_s5, n_lat, sel, selt, mtz, wop, vop, pq, S5_T, n_steps)
        y_fft = _fourier_ctx(z_fft, _fourier_latent(z_fft, n_lat, dft_tables, cc, sc), n_lat, cc, sc)
        sb_full = jnp.broadcast_to(sgu_b[i].astype(F32)[:, :, None], (SGU_HEADS, CHUNK, SGU_HEAD))
        m = _merge(h, z_sgu, y_s5, y_fft, i, w_glu_bf, sgu_norm_g[i].astype(F32), sgu_w_bf, sb_full,
                   w_gate_bf, b_gate[i].astype(F32).reshape(1, N_BRANCH * d), w_branch_bf)

        rw32 = jnp.zeros((d, LANES), F32).at[:, :N_EXPERTS].set(router_w[i].astype(F32))
        rw_hi = rw32.astype(BF16)
        rw_pad = jnp.concatenate([rw_hi, (rw32 - rw_hi.astype(F32)).astype(BF16)], axis=1)
        rb_pad = jnp.full((1, LANES), -1e30, F32).at[0, :N_EXPERTS].set(router_b[i].astype(F32))
        x1, h2, top_idx, top_w = _outproj(x_lat, x_ctx, m, n_lat, i, w_out_bf, g1, norm2_g[i].astype(F32),
                                          sh2, sc2, rw_pad, rb_pad)
        dest, counts, pstart, padded, item_e, item_row0, item_nch = _routing_tables(
            top_idx[:, :TOP_K], n_items)
        xs = _dispatch(h2, dest, counts, pstart, padded, n_rows)
        ys = _experts(xs, i, moe_w_up, moe_b_up, moe_w_down, item_e, item_row0, item_nch)
        x_lat = _combine(ys, dest, top_w, top_idx, moe_b_down[i], x1, n_lat, g2, final_g.astype(F32), last)
        x_ctx = None

    return x_lat.reshape(bsz, n_lat, d).astype(x.dtype)
```

```python
import functools
import math

import jax
import jax.numpy as jnp
from jax import lax
from jax.experimental import pallas as pl
from jax.experimental.pallas import tpu as pltpu

F32 = jnp.float32
BF16 = jnp.bfloat16
I32 = jnp.int32

EPS = 1e-6
S5_WIDTH = 512
S5_GROUP = 16
S5_GROUPS = 32
S5_STATE = 64
FFT_WIDTH = 512
FFT_GROUPS = 4
FFT_GROUP = 128
SGU_WIDTH = 512
SGU_HEADS = 4
SGU_HEAD = 128
CHUNK = 128
N_BRANCH = 3
N_EXPERTS = 32
TOP_K = 4
SWIGLU_LIMIT = 7.0
SWIGLU_ALPHA = 1.702
N_MOD = 6

LANES = 128
S5_PER_BLOCK = LANES // S5_GROUP
VMEM_LIMIT_BYTES = 56 * 1024 * 1024

S5_T = 16
ROW_TILE = 256
MERGE_TILE = 384
MERGE_COLS = 512
FFT_KTILE = 256
DFT_BLK = 256
MOE_CH = 128
MOE_RMAX = 1536
MOE_TF = 512
DISP_TILE = 256


def _cparams(sem, vmem=VMEM_LIMIT_BYTES):
    return pltpu.CompilerParams(dimension_semantics=sem, vmem_limit_bytes=vmem)


def _gelu(x):
    return 0.5 * x * (1.0 + jnp.tanh(math.sqrt(2.0 / math.pi) * (x + 0.044715 * x * x * x)))


def _sigmoid(x):
    return 1.0 / (1.0 + jnp.exp(-x))


def _adaln_kernel(c_ref, w_ref, b_ref, o_ref):
    c = c_ref[...]
    s = c * _sigmoid(c)
    o_ref[...] = jnp.dot(s, w_ref[...], preferred_element_type=F32,
                         precision=lax.Precision.HIGHEST) + b_ref[...]


def _adaln(cond8, w_mod, b_mod):
    n_layers, d, nm = w_mod.shape
    tn = 1024
    return pl.pallas_call(
        _adaln_kernel,
        grid=(n_layers, nm // tn),
        in_specs=[pl.BlockSpec((8, d), lambda l, j: (0, 0)),
                  pl.BlockSpec((None, d, tn), lambda l, j: (l, 0, j)),
                  pl.BlockSpec((None, 1, tn), lambda l, j: (l, 0, j))],
        out_specs=pl.BlockSpec((None, 8, tn), lambda l, j: (l, 0, j)),
        out_shape=jax.ShapeDtypeStruct((n_layers, 8, nm), F32),
        compiler_params=_cparams(("arbitrary", "arbitrary")),
        name="adaln",
    )(cond8, w_mod, b_mod.reshape(n_layers, 1, nm))


def _token_rows(x_lat, x_ctx, tm):
    d = x_lat.shape[1]
    if x_ctx is None:
        return (x_lat,), [pl.BlockSpec((tm, d), lambda i: (i, 0))], lambda refs: refs[0][...]
    lat_tiles = x_lat.shape[0] // tm
    specs = [pl.BlockSpec((tm, d), lambda i: (jnp.minimum(i, lat_tiles - 1), 0)),
             pl.BlockSpec((tm, d), lambda i: (jnp.maximum(i - lat_tiles, 0), 0))]
    load = lambda refs: jnp.where(pl.program_id(0) >= lat_tiles, refs[1][...], refs[0][...])
    return (x_lat, x_ctx), specs, load


def _inproj_kernel(*refs, load_x):
    g_ref, sh_ref, sc_ref, w_ref, h_ref, zs_ref, zf_ref, zg_ref, zscr = refs[-9:]
    x = load_x(refs[:-9])
    y = x * lax.rsqrt(jnp.mean(x * x, axis=-1, keepdims=True) + EPS) * g_ref[...]
    h = (y * (1.0 + sc_ref[...]) + sh_ref[...]).astype(BF16)
    h_ref[...] = h
    z = jnp.dot(h, w_ref[...], preferred_element_type=F32)
    zf_ref[...] = z[:, S5_WIDTH:S5_WIDTH + FFT_WIDTH]
    zg_ref[...] = z[:, S5_WIDTH + FFT_WIDTH:]
    n_chunks = zs_ref.shape[1]
    for b in range(S5_WIDTH // LANES):
        zscr[b] = z[:, b * LANES:(b + 1) * LANES]
        for j in range(S5_T):
            zs_ref[b, :, j * LANES:(j + 1) * LANES] = \
                zscr[b, pl.ds(j, n_chunks, stride=S5_T), :].astype(BF16)


def _inproj(x_lat, x_ctx, n, n_lat, layer, g, sh, sc, w_in_bf):
    d = x_lat.shape[1]
    tm = ROW_TILE
    lat_tiles = n_lat // tm
    row = lambda i: (i, 0)
    kind = lambda i: (jnp.where(i >= lat_tiles, 1, 0), 0, 0)
    x_ops, x_specs, load_x = _token_rows(x_lat, x_ctx, tm)
    return pl.pallas_call(
        functools.partial(_inproj_kernel, load_x=load_x),
        grid=(n // tm,),
        in_specs=x_specs + [
                  pl.BlockSpec((1, d), lambda i: (0, 0)),
                  pl.BlockSpec((None, 1, d), kind),
                  pl.BlockSpec((None, 1, d), kind),
                  pl.BlockSpec((None, d, w_in_bf.shape[2]), lambda i: (layer, 0, 0))],
        out_specs=[pl.BlockSpec((tm, d), row),
                   pl.BlockSpec((S5_WIDTH // LANES, tm // S5_T, S5_T * LANES), lambda i: (0, i, 0)),
                   pl.BlockSpec((tm, FFT_WIDTH), row),
                   pl.BlockSpec((tm, 2 * SGU_WIDTH), row)],
        out_shape=[jax.ShapeDtypeStruct((n, d), BF16),
                   jax.ShapeDtypeStruct((S5_WIDTH // LANES, n // S5_T, S5_T * LANES), BF16),
                   jax.ShapeDtypeStruct((n, FFT_WIDTH), F32),
                   jax.ShapeDtypeStruct((n, 2 * SGU_WIDTH), F32)],
        scratch_shapes=[pltpu.VMEM((S5_WIDTH // LANES, tm, LANES), F32)],
        compiler_params=_cparams(("arbitrary",)),
        name="inproj",
    )(*x_ops, g.reshape(1, d), sh, sc, w_in_bf)


def _s5_operators(a_re, a_im, log_dt, b_re, b_im, c_re, c_im, d_skip, t_len, n_steps):
    lam = lax.complex(a_re.astype(F32), a_im.astype(F32))
    dt = jnp.exp(log_dt.astype(F32))[..., None]
    ldt = lam * dt
    tau = jnp.arange(t_len + 1, dtype=F32)
    pw = jnp.exp(ldt[:, :, None, :] * tau[None, None, :, None])
    a_bar = jnp.exp(ldt)
    bb = ((a_bar - 1.0) / lam)[..., None] * lax.complex(b_re.astype(F32), b_im.astype(F32))
    cc = lax.complex(c_re.astype(F32), c_im.astype(F32))
    k = jnp.real(jnp.einsum('dghp,dgtp,dgpk->dgthk', cc, pw[:, :, :t_len], bb))
    g = S5_GROUPS
    tw = t_len * S5_GROUP
    dd = d_skip.astype(F32).reshape(g, S5_GROUP)
    k0 = k[0][:, :1] + k[1][:, :1] + (jnp.eye(S5_GROUP, dtype=F32)[None] * dd[:, :, None])[:, None]
    kk = jnp.concatenate([k[1][:, :0:-1], k0, k[0][:, 1:]], axis=1)
    kk = kk.transpose(0, 3, 1, 2).reshape(g, S5_GROUP, (2 * t_len - 1) * S5_GROUP).astype(BF16)
    m = jnp.concatenate([kk[:, :, (t_len - 1 - j) * S5_GROUP:(2 * t_len - 1 - j) * S5_GROUP]
                         for j in range(t_len)], axis=1)
    wf = pw[0][:, ::-1][:, 1:, :, None] * bb[0][:, None, :, :]
    wb = pw[1][:, :t_len, :, None] * bb[1][:, None, :, :]

    def state_cols(w):
        w = w.transpose(0, 1, 3, 2)
        return jnp.concatenate([jnp.real(w), jnp.imag(w)], axis=-1).reshape(g, tw, 2 * S5_STATE)

    wop = jnp.concatenate([state_cols(wf), state_cols(wb)], axis=-1).astype(BF16)
    gf = cc[0][:, None, :, :] * pw[0][:, 1:, None, :]
    gb = cc[1][:, None, :, :] * pw[1][:, ::-1][:, :t_len, None, :]

    def state_rows(gm):
        gm = gm.transpose(0, 3, 1, 2).reshape(g, S5_STATE, tw)
        return jnp.concatenate([jnp.real(gm), -jnp.imag(gm)], axis=1)

    vop = jnp.concatenate([state_rows(gf), state_rows(gb)], axis=1).astype(BF16)
    steps = (t_len * (2.0 ** jnp.arange(n_steps, dtype=F32)))
    am = jnp.exp(ldt[:, :, None, :] * steps[None, None, :, None])
    pvec = jnp.concatenate([jnp.real(am), jnp.real(am)], axis=-1)
    qvec = jnp.concatenate([-jnp.imag(am), jnp.imag(am)], axis=-1)
    pq = jnp.stack([pvec, qvec], axis=3)
    pq = pq.transpose(1, 0, 2, 3, 4).reshape(g, 4 * n_steps, 2 * S5_STATE)
    return m, wop, vop, pq


def _s5_kernel(zs_ref, sel_ref, selt_ref, m_ref, wop_ref, vop_ref, pq_ref, y_ref, *, lat_chunks, n_steps):
    nc = zs_ref.shape[0]
    sw = 2 * S5_STATE
    u = jnp.dot(zs_ref[...], sel_ref[...], preferred_element_type=F32).astype(BF16)
    xloc = jnp.dot(u, wop_ref[...], preferred_element_type=F32)
    row = lax.broadcasted_iota(I32, (nc, sw), 0)
    pos_f = jnp.where(row >= lat_chunks, row - lat_chunks, row + (nc - lat_chunks))
    pos_b = (nc - 1) - row

    def chain(xl, pos, base, forward):
        x = jnp.where(pos >= 1, pltpu.roll(xl, 1 if forward else nc - 1, 0), 0.0)
        for m in range(n_steps):
            k = 1 << m
            s = jnp.where(pos >= k, pltpu.roll(x, k if forward else nc - k, 0), 0.0)
            pv = pq_ref[base + 2 * m:base + 2 * m + 1, :]
            qv = pq_ref[base + 2 * m + 1:base + 2 * m + 2, :]
            x = x + pv * s + qv * pltpu.roll(s, S5_STATE, 1)
        return x

    xf = chain(xloc[:, :sw], pos_f, 0, True)
    xb = chain(xloc[:, sw:], pos_b, 2 * n_steps, False)
    xin = jnp.concatenate([xf, xb], axis=1).astype(BF16)
    y = (jnp.dot(u, m_ref[...], preferred_element_type=F32)
         + jnp.dot(xin, vop_ref[...], preferred_element_type=F32))
    contrib = jnp.dot(y.astype(BF16), selt_ref[...], preferred_element_type=F32)

    @pl.when(pl.program_id(0) % S5_PER_BLOCK == 0)
    def _():
        y_ref[...] = contrib

    @pl.when(pl.program_id(0) % S5_PER_BLOCK != 0)
    def _():
        y_ref[...] += contrib


def _s5_select_tables(t_len):
    r = jnp.arange(t_len * LANES, dtype=I32)[None, :, None]
    c = jnp.arange(t_len * S5_GROUP, dtype=I32)[None, None, :]
    gl = jnp.arange(S5_PER_BLOCK, dtype=I32)[:, None, None]
    sel = (r == (c // S5_GROUP) * LANES + gl * S5_GROUP + c % S5_GROUP).astype(BF16)
    return sel, sel.transpose(0, 2, 1)


def _s5_mix(zs, n_lat, sel, selt, m, wop, vop, pq, t_len, n_steps):
    nb, nc, bw = zs.shape
    g = S5_GROUPS
    tw = t_len * S5_GROUP
    blk = lambda i: (i // S5_PER_BLOCK, 0, 0)
    loc = lambda i: (i % S5_PER_BLOCK, 0, 0)
    return pl.pallas_call(
        functools.partial(_s5_kernel, lat_chunks=n_lat // t_len, n_steps=n_steps),
        grid=(g,),
        in_specs=[pl.BlockSpec((None, nc, bw), blk),
                  pl.BlockSpec((None, bw, tw), loc),
                  pl.BlockSpec((None, tw, bw), loc),
                  pl.BlockSpec((None, tw, tw), lambda i: (i, 0, 0)),
                  pl.BlockSpec((None, tw, 4 * S5_STATE), lambda i: (i, 0, 0)),
                  pl.BlockSpec((None, 4 * S5_STATE, tw), lambda i: (i, 0, 0)),
                  pl.BlockSpec((None, 4 * n_steps, 2 * S5_STATE), lambda i: (i, 0, 0))],
        out_specs=pl.BlockSpec((None, nc, bw), blk),
        out_shape=jax.ShapeDtypeStruct((nb, nc, bw), F32),
        compiler_params=_cparams(("arbitrary",)),
        name="s5_mix",
    )(zs, sel, selt, m, wop, vop, pq)


def _dft_tables(n_pos):
    half = n_pos // 2
    blk = min(DFT_BLK, half)
    kk = jnp.arange(half, dtype=I32)[:, None]
    ang = lambda prod: (2.0 * math.pi / n_pos) * (prod % n_pos).astype(F32)
    base = ang(kk * jnp.arange(blk, dtype=I32)[None, :])
    phase = ang(kk * (blk * jnp.arange(half // blk, dtype=I32))[None, :])
    return jnp.cos(base), jnp.sin(base), jnp.cos(phase), jnp.sin(phase)


def _channel_tables():
    q = jnp.arange(FFT_GROUP, dtype=I32)
    ang = (2.0 * math.pi / FFT_GROUP) * ((q[:, None] * q[None, :]) % FFT_GROUP).astype(F32)
    return jnp.cos(ang).astype(BF16), jnp.sin(ang).astype(BF16)


def _chan_dft_kernel(zl_ref, zh_ref, cc_ref, sc_ref, rc_ref, rs_ref):
    w = zl_ref.shape[1]
    for hf, zref in enumerate((zl_ref, zh_ref)):
        z = zref[...].astype(BF16)
        for g in range(FFT_GROUPS):
            sl = slice(g * FFT_GROUP, (g + 1) * FFT_GROUP)
            osl = slice(hf * w + g * FFT_GROUP, hf * w + (g + 1) * FFT_GROUP)
            rc_ref[:, osl] = jnp.dot(z[:, sl], cc_ref[...], preferred_element_type=F32).astype(BF16)
            rs_ref[:, osl] = jnp.dot(z[:, sl], sc_ref[...], preferred_element_type=F32).astype(BF16)


def _posdft_kernel(bc_ref, bs_ref, pc_ref, ps_ref, alt_ref, rc_ref, rs_ref, o_ref, *, scale):
    tk, w = o_ref.shape
    bc = bc_ref[...]
    bs = bs_ref[...]
    cparts, sparts = [], []
    for t in range(pc_ref.shape[1]):
        pc = pc_ref[:, t:t + 1]
        ps = ps_ref[:, t:t + 1]
        cparts.append((pc * bc - ps * bs).astype(BF16))
        sparts.append((ps * bc + pc * bs).astype(BF16))
    cm = jnp.concatenate(cparts, axis=1)
    sm = jnp.concatenate(sparts, axis=1)
    alt = jnp.where(pl.program_id(0) == 1, alt_ref[...], jnp.ones(alt_ref.shape, BF16))
    krow = lax.broadcasted_iota(I32, (tk, w), 0) + pl.program_id(1) * tk
    sign = jnp.where((krow & 1) == 0, 1.0, -1.0)
    u = (jnp.dot(cm * alt, rc_ref[...], preferred_element_type=F32)
         - jnp.dot(sm * alt, rs_ref[...], preferred_element_type=F32))
    o_ref[...] = (u[:, :w] + sign * u[:, w:]) * scale


def _ctx_dft_kernel(z_ref, cc_ref, sc_ref, cm_ref, sm_ref, lat_ref, o_ref, *, scale):
    del lat_ref
    z = z_ref[...].astype(BF16)
    for g in range(FFT_GROUPS):
        sl = slice(g * FFT_GROUP, (g + 1) * FFT_GROUP)
        zc = jnp.dot(z[:, sl], cc_ref[...], preferred_element_type=F32).astype(BF16)
        zs = jnp.dot(z[:, sl], sc_ref[...], preferred_element_type=F32).astype(BF16)
        o_ref[:, sl] = (jnp.dot(cm_ref[...], zc, preferred_element_type=F32)
                        - jnp.dot(sm_ref[...], zs, preferred_element_type=F32)) * scale


def _fourier_latent(z_all, n, tables, cc, sc):
    bcos, bsin, pcos, psin = tables
    w = z_all.shape[1]
    half = n // 2
    assert half % 2 == 0
    tr = min(512, half)
    nb = half // tr
    const = lambda i: (0, 0)
    rc, rs = pl.pallas_call(
        _chan_dft_kernel,
        grid=(nb,),
        in_specs=[pl.BlockSpec((tr, w), lambda i: (i, 0)),
                  pl.BlockSpec((tr, w), lambda i: (i + nb, 0)),
                  pl.BlockSpec((FFT_GROUP, FFT_GROUP), const),
                  pl.BlockSpec((FFT_GROUP, FFT_GROUP), const)],
        out_specs=[pl.BlockSpec((tr, 2 * w), lambda i: (i, 0))] * 2,
        out_shape=[jax.ShapeDtypeStruct((half, 2 * w), BF16)] * 2,
        compiler_params=_cparams(("arbitrary",)),
        name="fft_chan",
    )(z_all, z_all, cc, sc)
    tk = min(FFT_KTILE, half)
    nk = half // tk
    alt = jnp.where(jnp.arange(half) % 2 == 0, 1.0, -1.0).astype(BF16).reshape(1, half)
    out = pl.pallas_call(
        functools.partial(_posdft_kernel, scale=1.0 / math.sqrt(n * FFT_GROUP)),
        grid=(2, nk),
        in_specs=[pl.BlockSpec((tk, bcos.shape[1]), lambda q, i: (i, 0)),
                  pl.BlockSpec((tk, bcos.shape[1]), lambda q, i: (i, 0)),
                  pl.BlockSpec((tk, pcos.shape[1]), lambda q, i: (i, 0)),
                  pl.BlockSpec((tk, pcos.shape[1]), lambda q, i: (i, 0)),
                  pl.BlockSpec((1, half), lambda q, i: (0, 0)),
                  pl.BlockSpec((half, 2 * w), lambda q, i: (0, 0), pipeline_mode=pl.Buffered(1)),
                  pl.BlockSpec((half, 2 * w), lambda q, i: (0, 0), pipeline_mode=pl.Buffered(1))],
        out_specs=pl.BlockSpec((tk, w), lambda q, i: (q * nk + i, 0)),
        out_shape=jax.ShapeDtypeStruct((z_all.shape[0], w), F32),
        compiler_params=_cparams(("arbitrary", "arbitrary")),
        name="fft_posdft",
    )(bcos, bsin, pcos, psin, alt, rc, rs)
    return out


def _fourier_ctx(z_all, y_all, n_lat, cc, sc):
    n_all, w = z_all.shape
    n = n_all - n_lat
    assert n_lat % n == 0
    q = jnp.arange(n, dtype=I32)
    ang = (2.0 * math.pi / n) * ((q[:, None] * q[None, :]) % n).astype(F32)
    cm, sm = jnp.cos(ang).astype(BF16), jnp.sin(ang).astype(BF16)
    full = lambda s: pl.BlockSpec(s, lambda i: (0,) * len(s))
    ctx_rows = pl.BlockSpec((n, w), lambda i: (n_lat // n, 0))
    return pl.pallas_call(
        functools.partial(_ctx_dft_kernel, scale=1.0 / math.sqrt(n * FFT_GROUP)),
        grid=(1,),
        in_specs=[ctx_rows, full((FFT_GROUP, FFT_GROUP)), full((FFT_GROUP, FFT_GROUP)),
                  full((n, n)), full((n, n)), pl.BlockSpec(memory_space=pl.ANY)],
        out_specs=ctx_rows,
        out_shape=jax.ShapeDtypeStruct((n_all, w), F32),
        input_output_aliases={5: 0},
        compiler_params=_cparams(("arbitrary",)),
        name="fft_ctx",
    )(z_all, cc, sc, cm, sm, y_all)


def _merge_kernel(h_ref, zg_ref, ys_ref, yf_ref, wglu_ref, gv_ref, sw_ref, sb_ref,
                  wg0_ref, wg1_ref, wg2_ref, bg0_ref, bg1_ref, bg2_ref,
                  wb0_ref, wb1_ref, wb2_ref, m_ref, feats, yscr):
    j = pl.program_id(1)
    tm = h_ref.shape[0]

    @pl.when(j == 0)
    def _():
        n_chunks = ys_ref.shape[1]
        for b in range(S5_WIDTH // LANES):
            for t in range(S5_T):
                yscr[b, pl.ds(t, n_chunks, stride=S5_T), :] = ys_ref[b, :, t * LANES:(t + 1) * LANES]
        y = _gelu(jnp.concatenate([yscr[b] for b in range(S5_WIDTH // LANES)], axis=1))
        feats[0] = (y * _sigmoid(jnp.dot(y.astype(BF16), wglu_ref[...],
                                         preferred_element_type=F32))).astype(BF16)
        feats[1] = yf_ref[...].astype(BF16)
        gz = _gelu(zg_ref[...])
        u = gz[:, :SGU_WIDTH]
        v = gz[:, SGU_WIDTH:]
        v = v * lax.rsqrt(jnp.mean(v * v, axis=-1, keepdims=True) + EPS) * gv_ref[...]
        vb = v.astype(BF16)
        for c in range(tm // CHUNK):
            rs = slice(c * CHUNK, (c + 1) * CHUNK)
            for hd in range(SGU_HEADS):
                cs = slice(hd * SGU_HEAD, (hd + 1) * SGU_HEAD)
                s = jnp.dot(sw_ref[hd], vb[rs, cs], preferred_element_type=F32) + sb_ref[hd]
                feats[2, rs, cs] = (u[rs, cs] * s).astype(BF16)

    h = h_ref[...]
    acc = None
    for k, (wg, bg, wb) in enumerate(((wg0_ref, bg0_ref, wb0_ref), (wg1_ref, bg1_ref, wb1_ref),
                                      (wg2_ref, bg2_ref, wb2_ref))):
        gate = _sigmoid(jnp.dot(h, wg[...], preferred_element_type=F32) + bg[...])
        br = jnp.dot(feats[k], wb[...], preferred_element_type=F32)
        acc = gate * br if acc is None else acc + gate * br
    m_ref[...] = acc.astype(BF16)


def _merge(h, zg, ys, yf, layer, wglu_bf, gv, sw_bf, sb_full, wgate_bf, bgate, wbranch_bf):
    n, d = h.shape
    tm, tn = MERGE_TILE, MERGE_COLS
    nj = d // tn
    row = lambda i, j: (i, 0)
    const2 = lambda i, j: (0, 0)
    const3 = lambda i, j: (0, 0, 0)
    gate_spec = lambda k: pl.BlockSpec((None, d, tn), lambda i, j: (layer, 0, k * nj + j))
    bias_spec = lambda k: pl.BlockSpec((1, tn), lambda i, j: (0, k * nj + j))
    br_spec = lambda k: pl.BlockSpec((None, None, S5_WIDTH, tn), lambda i, j: (layer, k, 0, j))
    return pl.pallas_call(
        _merge_kernel,
        grid=(n // tm, nj),
        in_specs=[pl.BlockSpec((tm, d), row),
                  pl.BlockSpec((tm, 2 * SGU_WIDTH), row),
                  pl.BlockSpec((S5_WIDTH // LANES, tm // S5_T, S5_T * LANES), lambda i, j: (0, i, 0)),
                  pl.BlockSpec((tm, FFT_WIDTH), row),
                  pl.BlockSpec((None, S5_WIDTH, S5_WIDTH), lambda i, j: (layer, 0, 0)),
                  pl.BlockSpec((1, SGU_WIDTH), const2),
                  pl.BlockSpec((None, SGU_HEADS, CHUNK, CHUNK), lambda i, j: (layer, 0, 0, 0)),
                  pl.BlockSpec((SGU_HEADS, CHUNK, SGU_HEAD), const3),
                  gate_spec(0), gate_spec(1), gate_spec(2),
                  bias_spec(0), bias_spec(1), bias_spec(2),
                  br_spec(0), br_spec(1), br_spec(2)],
        out_specs=pl.BlockSpec((tm, tn), lambda i, j: (i, j)),
        out_shape=jax.ShapeDtypeStruct((n, d), BF16),
        scratch_shapes=[pltpu.VMEM((N_BRANCH, tm, S5_WIDTH), BF16),
                        pltpu.VMEM((S5_WIDTH // LANES, tm, LANES), F32)],
        compiler_params=_cparams(("arbitrary", "arbitrary")),
        name="gated_merge",
    )(h, zg, ys, yf, wglu_bf, gv.reshape(1, SGU_WIDTH), sw_bf, sb_full,
      wgate_bf, wgate_bf, wgate_bf, bgate, bgate, bgate, wbranch_bf, wbranch_bf, wbranch_bf)


def _outproj_kernel(*refs, load_x):
    (m_ref, wo_ref, g1_ref, g2_ref, sh_ref, sc_ref, rw_ref, rb_ref,
     x1_ref, h2_ref, ti_ref, tw_ref) = refs[-12:]
    x1 = load_x(refs[:-12]) + g1_ref[...] * jnp.dot(m_ref[...], wo_ref[...], preferred_element_type=F32)
    x1_ref[...] = x1
    y = x1 * lax.rsqrt(jnp.mean(x1 * x1, axis=-1, keepdims=True) + EPS) * g2_ref[...]
    h2 = y * (1.0 + sc_ref[...]) + sh_ref[...]
    h2_ref[...] = h2
    h_hi = h2.astype(BF16)
    h_lo = (h2 - h_hi.astype(F32)).astype(BF16)
    p_hi = jnp.dot(h_hi, rw_ref[...], preferred_element_type=F32)
    p_lo = jnp.dot(h_lo, rw_ref[:, :LANES], preferred_element_type=F32)
    logits = p_hi[:, :LANES] + p_hi[:, LANES:] + p_lo + rb_ref[...]
    lane = lax.broadcasted_iota(I32, logits.shape, 1)
    vals, idxs = [], []
    for _ in range(TOP_K):
        mx = jnp.max(logits, axis=-1, keepdims=True)
        am = jnp.min(jnp.where(logits == mx, lane, LANES), axis=-1, keepdims=True)
        vals.append(mx)
        idxs.append(am)
        logits = jnp.where(lane == am, -jnp.inf, logits)
    ex = [jnp.exp(v - vals[0]) for v in vals]
    den = ex[0] + ex[1] + ex[2] + ex[3]
    ti = jnp.zeros(lane.shape, I32)
    tw = jnp.zeros(lane.shape, F32)
    for k in range(TOP_K):
        ti = jnp.where(lane == k, idxs[k], ti)
        tw = jnp.where(lane == k, ex[k] / den, tw)
    ti_ref[...] = ti
    tw_ref[...] = tw


def _outproj(x_lat, x_ctx, m, n_lat, layer, wo_bf, g1, g2n, sh2, sc2, rw_pad, rb_pad):
    n, d = m.shape
    tm = ROW_TILE
    lat_tiles = n_lat // tm
    row = lambda i: (i, 0)
    kind = lambda i: (jnp.where(i >= lat_tiles, 1, 0), 0, 0)
    const = lambda i: (0, 0)
    x_ops, x_specs, load_x = _token_rows(x_lat, x_ctx, tm)
    return pl.pallas_call(
        functools.partial(_outproj_kernel, load_x=load_x),
        grid=(n // tm,),
        in_specs=x_specs + [
                  pl.BlockSpec((tm, d), row),
                  pl.BlockSpec((None, d, d), lambda i: (layer, 0, 0)),
                  pl.BlockSpec((None, 1, d), kind),
                  pl.BlockSpec((1, d), const),
                  pl.BlockSpec((None, 1, d), kind),
                  pl.BlockSpec((None, 1, d), kind),
                  pl.BlockSpec((d, 2 * LANES), const),
                  pl.BlockSpec((1, LANES), const)],
        out_specs=[pl.BlockSpec((tm, d), row), pl.BlockSpec((tm, d), row),
                   pl.BlockSpec((tm, LANES), row), pl.BlockSpec((tm, LANES), row)],
        out_shape=[jax.ShapeDtypeStruct((n, d), F32), jax.ShapeDtypeStruct((n, d), F32),
                   jax.ShapeDtypeStruct((n, LANES), I32), jax.ShapeDtypeStruct((n, LANES), F32)],
        compiler_params=_cparams(("arbitrary",)),
        name="outproj_router",
    )(*x_ops, m, wo_bf, g1, g2n.reshape(1, d), sh2, sc2, rw_pad, rb_pad)


def _routing_tables(top_idx, n_items):
    flat_e = top_idx.reshape(-1)
    onehot = (flat_e[:, None] == jnp.arange(N_EXPERTS, dtype=I32)[None, :]).astype(I32)
    csum = jnp.cumsum(onehot, axis=0)
    rank = jnp.sum((csum - onehot) * onehot, axis=1)
    counts = csum[-1]
    padded = (counts + MOE_CH - 1) // MOE_CH * MOE_CH
    pend = jnp.cumsum(padded)
    pstart = pend - padded
    dest = jnp.sum(onehot * pstart[None, :], axis=1) + rank
    per_e = (padded + MOE_RMAX - 1) // MOE_RMAX
    iend = jnp.cumsum(per_e)
    istart = iend - per_e
    total = iend[-1]
    t = jnp.arange(n_items, dtype=I32)
    valid = t < total
    tc = jnp.minimum(t, total - 1)
    e_of = jnp.minimum(jnp.sum((tc[:, None] >= iend[None, :]).astype(I32), axis=1), N_EXPERTS - 1)
    local = tc - istart[e_of]
    row0 = pstart[e_of] + local * MOE_RMAX
    nch = jnp.clip((padded[e_of] - local * MOE_RMAX) // MOE_CH, 0, MOE_RMAX // MOE_CH)
    nch = jnp.where(valid, nch, 0)
    return (dest.astype(I32), counts.astype(I32), pstart.astype(I32), padded.astype(I32),
            e_of.astype(I32), row0.astype(I32), nch.astype(I32))


def _dispatch_kernel(cnt_ref, pst_ref, pad_ref, dest_ref, h_ref, xs_ref, zrow, sem, zsem):
    i = pl.program_id(0)
    tq = h_ref.shape[0]

    @pl.when(i == 0)
    def _():
        zrow[...] = jnp.zeros(zrow.shape, zrow.dtype)

        def per_expert(e, carry):
            first = pst_ref[e] + cnt_ref[e]
            npad = pad_ref[e] - cnt_ref[e]

            def start(r, c):
                pltpu.make_async_copy(zrow.at[pl.ds(0, 1)], xs_ref.at[pl.ds(first + r, 1)], zsem).start()
                return c

            def wait(r, c):
                pltpu.make_async_copy(zrow.at[pl.ds(0, 1)], xs_ref.at[pl.ds(first + r, 1)], zsem).wait()
                return c

            lax.fori_loop(0, npad, start, 0)
            lax.fori_loop(0, npad, wait, 0)
            return carry

        lax.fori_loop(0, N_EXPERTS, per_expert, 0)

    def start(t, c):
        src = h_ref.at[pl.ds(t, 1)]
        for k in range(TOP_K):
            pltpu.make_async_copy(src, xs_ref.at[pl.ds(dest_ref[0, t * TOP_K + k], 1)], sem).start(priority=k % 2)
        return c

    lax.fori_loop(0, tq, start, 0, unroll=2)
    for _ in range(TOP_K):
        pltpu.make_async_copy(h_ref, xs_ref.at[pl.ds(0, tq)], sem).wait()


def _dispatch(h2, dest, counts, pstart, padded, n_rows):
    d = h2.shape[1]
    tq = DISP_TILE
    nt = dest.shape[0] // (tq * TOP_K)
    dest3 = dest.reshape(nt, 1, tq * TOP_K)
    return pl.pallas_call(
        _dispatch_kernel,
        grid_spec=pltpu.PrefetchScalarGridSpec(
            num_scalar_prefetch=3,
            grid=(nt,),
            in_specs=[pl.BlockSpec((None, 1, tq * TOP_K), lambda i, *_: (i, 0, 0), memory_space=pltpu.SMEM),
                      pl.BlockSpec((tq, d), lambda i, *_: (i, 0))],
            out_specs=pl.BlockSpec(memory_space=pl.ANY),
            scratch_shapes=[pltpu.VMEM((8, d), F32), pltpu.SemaphoreType.DMA(()), pltpu.SemaphoreType.DMA(())]),
        out_shape=jax.ShapeDtypeStruct((n_rows, d), F32),
        compiler_params=_cparams(("arbitrary",)),
        name="moe_dispatch",
    )(counts, pstart, padded, dest3, h2)


def _experts_kernel(ie_ref, ir_ref, in_ref, xs_ref, wg_ref, wu_ref, wd_ref, bg_ref, bu_ref,
                    ys_ref, xstage, xb, acc, pending, sem_in, sem_out, *, units_per_step):
    t = pl.program_id(0)
    j = pl.program_id(1)
    nf = pl.num_programs(1)
    nch = in_ref[t]
    row0 = ir_ref[t]
    ch = MOE_CH

    def hbm_rows(c):
        return pl.ds(pl.multiple_of(row0 + c * ch, ch), ch)

    def vmem_rows(c, size=ch):
        return pl.ds(pl.multiple_of(c * ch, ch), size)

    def in_copy(c, slot):
        return pltpu.make_async_copy(xs_ref.at[hbm_rows(c)], xstage.at[slot], sem_in.at[slot])

    def out_copy(c):
        return pltpu.make_async_copy(acc.at[vmem_rows(c)], ys_ref.at[hbm_rows(c)], sem_out)

    def drain():
        def body(c, carry):
            out_copy(0).wait()
            return carry
        lax.fori_loop(0, pending[0], body, 0)
        pending[0] = 0

    cur = t & 1
    n_items = pl.num_programs(0)
    t_next = jnp.minimum(t + 1, n_items - 1)
    nch_next = jnp.where(t + 1 < n_items, in_ref[t_next], 0)
    row0_next = ir_ref[t_next]

    def next_copy(u, k):
        src = xs_ref.at[pl.ds(pl.multiple_of(row0_next + u * ch, ch), ch)]
        return pltpu.make_async_copy(src, xstage.at[k], sem_in.at[k])

    @pl.when(jnp.logical_and(t == 0, j == 0))
    def _():
        pending[0] = 0

    @pl.when(jnp.logical_and(jnp.logical_and(t == 0, j == 0), nch > 0))
    def _():
        in_copy(0, 0).start()

        def body(c, carry):
            slot = c & 1

            @pl.when(c + 1 < nch)
            def _():
                in_copy(c + 1, 1 - slot).start()

            in_copy(c, slot).wait()
            xb[0, vmem_rows(c), :] = xstage[slot].astype(BF16)
            return carry

        lax.fori_loop(0, nch, body, 0)

    for k in range(units_per_step):
        @pl.when(j * units_per_step + k < nch_next)
        def _():
            next_copy(j * units_per_step + k, k).start()

    @pl.when(nch > 0)
    def _():
        def partial_out(c, size):
            rows = vmem_rows(c, size)
            x = xb[cur, rows, :]
            gate = jnp.dot(x, wg_ref[...].astype(BF16), preferred_element_type=F32) + bg_ref[...]
            up = jnp.dot(x, wu_ref[...].astype(BF16), preferred_element_type=F32) + bu_ref[...]
            gate = jnp.minimum(gate, SWIGLU_LIMIT)
            up = jnp.clip(up, -SWIGLU_LIMIT, SWIGLU_LIMIT)
            act = gate * _sigmoid(SWIGLU_ALPHA * gate) * (up + 1.0)
            return rows, jnp.dot(act.astype(BF16), wd_ref[...].astype(BF16), preferred_element_type=F32)

        def accumulate(first):
            def store(rows, part):
                if first:
                    acc[rows, :] = part
                else:
                    acc[rows, :] += part

            def body(c4, carry):
                store(*partial_out(4 * c4, 4 * ch))
                return carry

            quads = lax.shift_right_logical(nch, 2)
            lax.fori_loop(0, quads, body, 0)

            @pl.when((nch & 2) != 0)
            def _():
                store(*partial_out(4 * quads, 2 * ch))

            @pl.when((nch & 1) != 0)
            def _():
                store(*partial_out(nch - 1, ch))

        @pl.when(j == 0)
        def _():
            drain()
            accumulate(True)

        @pl.when(j > 0)
        def _():
            accumulate(False)

        @pl.when(j == nf - 1)
        def _():
            def body(c, carry):
                out_copy(c).start()
                return carry
            lax.fori_loop(0, nch, body, 0)
            pending[0] = nch

    for k in range(units_per_step):
        @pl.when(j * units_per_step + k < nch_next)
        def _():
            next_copy(j * units_per_step + k, k).wait()
            xb[1 - cur, vmem_rows(j * units_per_step + k), :] = xstage[k].astype(BF16)

    @pl.when(jnp.logical_and(t == n_items - 1, j == nf - 1))
    def _():
        drain()


def _experts(xs, layer, w_up, b_up, w_down, item_e, item_row0, item_nch):
    n_rows, d = xs.shape
    n_layers, n_exp, _, two_f = w_up.shape
    f = two_f // 2
    tf = MOE_TF
    nf = f // tf
    assert nf >= 2
    units_per_step = max(2, -(-(MOE_RMAX // MOE_CH) // nf))
    n_items = item_e.shape[0]
    b_up4 = b_up.reshape(n_layers, n_exp, 1, two_f)

    def jj(t, j, ie, ir, inn):
        return jnp.where(inn[t] > 0, j, nf - 1)

    return pl.pallas_call(
        functools.partial(_experts_kernel, units_per_step=units_per_step),
        grid_spec=pltpu.PrefetchScalarGridSpec(
            num_scalar_prefetch=3,
            grid=(n_items, nf),
            in_specs=[pl.BlockSpec(memory_space=pl.ANY),
                      pl.BlockSpec((None, None, d, tf), lambda t, j, ie, ir, inn: (layer, ie[t], 0, jj(t, j, ie, ir, inn))),
                      pl.BlockSpec((None, None, d, tf), lambda t, j, ie, ir, inn: (layer, ie[t], 0, nf + jj(t, j, ie, ir, inn))),
                      pl.BlockSpec((None, None, tf, d), lambda t, j, ie, ir, inn: (layer, ie[t], jj(t, j, ie, ir, inn), 0)),
                      pl.BlockSpec((None, None, 1, tf), lambda t, j, ie, ir, inn: (layer, ie[t], 0, jj(t, j, ie, ir, inn))),
                      pl.BlockSpec((None, None, 1, tf), lambda t, j, ie, ir, inn: (layer, ie[t], 0, nf + jj(t, j, ie, ir, inn)))],
            out_specs=pl.BlockSpec(memory_space=pl.ANY),
            scratch_shapes=[pltpu.VMEM((units_per_step, MOE_CH, d), F32),
                            pltpu.VMEM((2, MOE_RMAX, d), BF16),
                            pltpu.VMEM((MOE_RMAX, d), F32),
                            pltpu.SMEM((1,), I32),
                            pltpu.SemaphoreType.DMA((units_per_step,)),
                            pltpu.SemaphoreType.DMA(())]),
        out_shape=jax.ShapeDtypeStruct((n_rows, d), F32),
        compiler_params=_cparams(("arbitrary", "arbitrary")),
        name="moe_experts",
    )(item_e, item_row0, item_nch, xs, w_up, w_up, w_down, b_up4, b_up4)


def _combine_kernel(dc_ref, dn_ref, tw_ref, ti_ref, x1_ref, g2_ref, fg_ref, bd_ref, ys_ref, o_ref,
                    gbuf, sem, *, final):
    i = pl.program_id(0)
    n = pl.num_programs(0)
    tq = x1_ref.shape[0]

    def issue(dref, slot):
        def body(t, c):
            for k in range(TOP_K):
                pltpu.make_async_copy(ys_ref.at[pl.ds(dref[0, t * TOP_K + k], 1)],
                                      gbuf.at[slot, k, pl.ds(t, 1)], sem.at[slot]).start(priority=k % 2)
            return c
        lax.fori_loop(0, tq, body, 0, unroll=2)

    slot = i & 1

    @pl.when(i == 0)
    def _():
        issue(dc_ref, 0)

    @pl.when(i + 1 < n)
    def _():
        issue(dn_ref, 1 - slot)

    for k in range(TOP_K):
        pltpu.make_async_copy(ys_ref.at[pl.ds(0, tq)], gbuf.at[slot, k], sem.at[slot]).wait()
    tw = tw_ref[...]
    ti = ti_ref[...]
    lane = lax.broadcasted_iota(I32, tw.shape, 1)
    ew = jnp.zeros(tw.shape, F32)
    for k in range(TOP_K):
        ew = ew + jnp.where(lane == ti[:, k:k + 1], tw[:, k:k + 1], 0.0)
    y = jnp.dot(ew.astype(BF16), bd_ref[...], preferred_element_type=F32)
    for k in range(TOP_K):
        y = y + tw[:, k:k + 1] * gbuf[slot, k]
    x2 = x1_ref[...] + g2_ref[...] * y
    if final:
        x2 = x2 * lax.rsqrt(jnp.mean(x2 * x2, axis=-1, keepdims=True) + EPS) * fg_ref[...]
    o_ref[...] = x2


def _combine(ys, dest, top_w, top_idx, b_down, x1, n_lat, g2, final_g, final):
    n, d = x1.shape
    bd_pad = jnp.zeros((LANES, d), BF16).at[:N_EXPERTS].set(b_down.astype(BF16))
    tq = DISP_TILE
    lat_tiles = n_lat // tq
    dest3 = dest.reshape(-1, 1, tq * TOP_K)
    if final:
        n = n_lat
    nt = n // tq
    kind = lambda i: (jnp.where(i >= lat_tiles, 1, 0), 0, 0)
    return pl.pallas_call(
        functools.partial(_combine_kernel, final=final),
        grid=(nt,),
        in_specs=[pl.BlockSpec((None, 1, tq * TOP_K), lambda i: (i, 0, 0), memory_space=pltpu.SMEM),
                  pl.BlockSpec((None, 1, tq * TOP_K), lambda i: (jnp.minimum(i + 1, nt - 1), 0, 0),
                               memory_space=pltpu.SMEM),
                  pl.BlockSpec((tq, LANES), lambda i: (i, 0)),
                  pl.BlockSpec((tq, LANES), lambda i: (i, 0)),
                  pl.BlockSpec((tq, d), lambda i: (i, 0)),
                  pl.BlockSpec((None, 1, d), kind),
                  pl.BlockSpec((1, d), lambda i: (0, 0)),
                  pl.BlockSpec((LANES, d), lambda i: (0, 0)),
                  pl.BlockSpec(memory_space=pl.ANY)],
        out_specs=pl.BlockSpec((tq, d), lambda i: (i, 0)),
        out_shape=jax.ShapeDtypeStruct((n, d), F32),
        scratch_shapes=[pltpu.VMEM((2, TOP_K, tq, d), F32), pltpu.SemaphoreType.DMA((2,))],
        compiler_params=_cparams(("arbitrary",)),
        name="moe_combine",
    )(dest3, dest3, top_w, top_idx, x1, g2, final_g.reshape(1, d), bd_pad, ys)


def kernel(x, c, ctx, c_ctx, w_mod, b_mod, norm1_g, norm2_g, w_in, s5_a_re, s5_a_im, s5_log_dt,
           s5_b_re, s5_b_im, s5_c_re, s5_c_im, s5_d, s5_w_glu, sgu_norm_g, sgu_w, sgu_b, w_branch,
           w_gate, b_gate, w_out, router_w, router_b, moe_w_up, moe_b_up, moe_w_down, moe_b_down,
           final_g):
    bsz, n_lat, d = x.shape
    n_ctx = ctx.shape[1]
    assert bsz == 1
    n_layers = w_mod.shape[0]
    n = n_lat + n_ctx
    x_lat, x_ctx = x.reshape(n_lat, d).astype(F32), ctx.reshape(n_ctx, d).astype(F32)
    w_in_bf, w_out_bf, w_gate_bf, w_branch_bf = (t.astype(BF16) for t in (w_in, w_out, w_gate, w_branch))
    w_glu_bf, sgu_w_bf = s5_w_glu.astype(BF16), sgu_w.astype(BF16)

    cond8 = jnp.zeros((8, d), F32).at[0].set(c[0]).at[1].set(c_ctx)
    mods = _adaln(cond8, w_mod, b_mod)[:, :2, :].reshape(n_layers, 2, N_MOD, 1, d)

    n_chunks = n // S5_T
    n_steps = max(1, (n_chunks - 1).bit_length())
    dft_tables = _dft_tables(n_lat)
    cc, sc = _channel_tables()
    sel, selt = _s5_select_tables(S5_T)
    n_assign = n * TOP_K
    n_rows = n_assign + N_EXPERTS * MOE_CH
    n_items = -(-n_assign // MOE_RMAX) + N_EXPERTS

    for i in range(n_layers):
        last = i == n_layers - 1
        sh1, sc1, g1, sh2, sc2, g2 = [mods[i, :, k] for k in range(N_MOD)]

        h, z_s5, z_fft, z_sgu = _inproj(x_lat, x_ctx, n, n_lat, i, norm1_g[i], sh1, sc1, w_in_bf)
        mtz, wop, vop, pq = _s5_operators(s5_a_re[i], s5_a_im[i], s5_log_dt[i], s5_b_re[i], s5_b_im[i],
                                     s5_c_re[i], s5_c_im[i], s5_d[i], S5_T, n_steps)
        y_s5 = _s5_mix(z_s5, n_lat, sel, selt, mtz, wop, vop, pq, S5_T, n_steps)
        y_fft = _fourier_ctx(z_fft, _fourier_latent(z_fft, n_lat, dft_tables, cc, sc), n_lat, cc, sc)
        sb_full = jnp.broadcast_to(sgu_b[i].astype(F32)[:, :, None], (SGU_HEADS, CHUNK, SGU_HEAD))
        m = _merge(h, z_sgu, y_s5, y_fft, i, w_glu_bf, sgu_norm_g[i].astype(F32), sgu_w_bf, sb_full,
                   w_gate_bf, b_gate[i].astype(F32).reshape(1, N_BRANCH * d), w_branch_bf)

        rw32 = jnp.zeros((d, LANES), F32).at[:, :N_EXPERTS].set(router_w[i].astype(F32))
        rw_hi = rw32.astype(BF16)
        rw_pad = jnp.concatenate([rw_hi, (rw32 - rw_hi.astype(F32)).astype(BF16)], axis=1)
        rb_pad = jnp.full((1, LANES), -1e30, F32).at[0, :N_EXPERTS].set(router_b[i].astype(F32))
        x1, h2, top_idx, top_w = _outproj(x_lat, x_ctx, m, n_lat, i, w_out_bf, g1, norm2_g[i].astype(F32),
                                          sh2, sc2, rw_pad, rb_pad)
        n_routed = n_lat if last else n
        dest, counts, pstart, padded, item_e, item_row0, item_nch = _routing_tables(
            top_idx[:n_routed, :TOP_K], n_items)
        xs = _dispatch(h2, dest, counts, pstart, padded, n_rows)
        ys = _experts(xs, i, moe_w_up, moe_b_up, moe_w_down, item_e, item_row0, item_nch)
        x_lat = _combine(ys, dest, top_w, top_idx, moe_b_down[i], x1, n_lat, g2, final_g.astype(F32), last)
        x_ctx = None

    return x_lat.reshape(bsz, n_lat, d).astype(x.dtype)
```

```python
import functools
import math

import jax
import jax.numpy as jnp
from jax import lax
from jax.experimental import pallas as pl
from jax.experimental.pallas import tpu as pltpu

F32 = jnp.float32
BF16 = jnp.bfloat16
I32 = jnp.int32

EPS = 1e-6
S5_WIDTH = 512
S5_GROUP = 16
S5_GROUPS = 32
S5_STATE = 64
FFT_WIDTH = 512
FFT_GROUPS = 4
FFT_GROUP = 128
SGU_WIDTH = 512
SGU_HEADS = 4
SGU_HEAD = 128
CHUNK = 128
N_BRANCH = 3
N_EXPERTS = 32
TOP_K = 4
SWIGLU_LIMIT = 7.0
SWIGLU_ALPHA = 1.702
N_MOD = 6

LANES = 128
S5_PER_BLOCK = LANES // S5_GROUP
VMEM_LIMIT_BYTES = 56 * 1024 * 1024

S5_T = 16
ROW_TILE = 256
MERGE_TILE = 384
MERGE_COLS = 512
FFT_KTILE = 256
DFT_BLK = 256
MOE_CH = 128
MOE_RMAX = 1536
MOE_TF = 512
DISP_TILE = 256


def _cparams(sem, vmem=VMEM_LIMIT_BYTES):
    return pltpu.CompilerParams(dimension_semantics=sem, vmem_limit_bytes=vmem)


def _gelu(x):
    return 0.5 * x * (1.0 + jnp.tanh(math.sqrt(2.0 / math.pi) * (x + 0.044715 * x * x * x)))


def _sigmoid(x):
    return 1.0 / (1.0 + jnp.exp(-x))


def _adaln_kernel(c_ref, w_ref, b_ref, o_ref):
    c = c_ref[...]
    s = c * _sigmoid(c)
    o_ref[...] = jnp.dot(s, w_ref[...], preferred_element_type=F32,
                         precision=lax.Precision.HIGHEST) + b_ref[...]


def _adaln(cond8, w_mod, b_mod):
    n_layers, d, nm = w_mod.shape
    tn = 1024
    return pl.pallas_call(
        _adaln_kernel,
        grid=(n_layers, nm // tn),
        in_specs=[pl.BlockSpec((8, d), lambda l, j: (0, 0)),
                  pl.BlockSpec((None, d, tn), lambda l, j: (l, 0, j)),
                  pl.BlockSpec((None, 1, tn), lambda l, j: (l, 0, j))],
        out_specs=pl.BlockSpec((None, 8, tn), lambda l, j: (l, 0, j)),
        out_shape=jax.ShapeDtypeStruct((n_layers, 8, nm), F32),
        compiler_params=_cparams(("arbitrary", "arbitrary")),
        name="adaln",
    )(cond8, w_mod, b_mod.reshape(n_layers, 1, nm))


def _token_rows(x_lat, x_ctx, tm):
    d = x_lat.shape[1]
    if x_ctx is None:
        return (x_lat,), [pl.BlockSpec((tm, d), lambda i: (i, 0))], lambda refs: refs[0][...]
    lat_tiles = x_lat.shape[0] // tm
    specs = [pl.BlockSpec((tm, d), lambda i: (jnp.minimum(i, lat_tiles - 1), 0)),
             pl.BlockSpec((tm, d), lambda i: (jnp.maximum(i - lat_tiles, 0), 0))]
    load = lambda refs: jnp.where(pl.program_id(0) >= lat_tiles, refs[1][...], refs[0][...])
    return (x_lat, x_ctx), specs, load


def _inproj_kernel(*refs, load_x):
    g_ref, sh_ref, sc_ref, w_ref, h_ref, zs_ref, zf_ref, zg_ref, zscr = refs[-9:]
    x = load_x(refs[:-9])
    y = x * lax.rsqrt(jnp.mean(x * x, axis=-1, keepdims=True) + EPS) * g_ref[...]
    h = (y * (1.0 + sc_ref[...]) + sh_ref[...]).astype(BF16)
    h_ref[...] = h
    z = jnp.dot(h, w_ref[...], preferred_element_type=F32)
    zf_ref[...] = z[:, S5_WIDTH:S5_WIDTH + FFT_WIDTH]
    zg_ref[...] = z[:, S5_WIDTH + FFT_WIDTH:]
    n_chunks = zs_ref.shape[1]
    for b in range(S5_WIDTH // LANES):
        zscr[b] = z[:, b * LANES:(b + 1) * LANES]
        for j in range(S5_T):
            zs_ref[b, :, j * LANES:(j + 1) * LANES] = \
                zscr[b, pl.ds(j, n_chunks, stride=S5_T), :].astype(BF16)


def _inproj(x_lat, x_ctx, n, n_lat, layer, g, sh, sc, w_in_bf):
    d = x_lat.shape[1]
    tm = ROW_TILE
    lat_tiles = n_lat // tm
    row = lambda i: (i, 0)
    kind = lambda i: (jnp.where(i >= lat_tiles, 1, 0), 0, 0)
    x_ops, x_specs, load_x = _token_rows(x_lat, x_ctx, tm)
    return pl.pallas_call(
        functools.partial(_inproj_kernel, load_x=load_x),
        grid=(n // tm,),
        in_specs=x_specs + [
                  pl.BlockSpec((1, d), lambda i: (0, 0)),
                  pl.BlockSpec((None, 1, d), kind),
                  pl.BlockSpec((None, 1, d), kind),
                  pl.BlockSpec((None, d, w_in_bf.shape[2]), lambda i: (layer, 0, 0))],
        out_specs=[pl.BlockSpec((tm, d), row),
                   pl.BlockSpec((S5_WIDTH // LANES, tm // S5_T, S5_T * LANES), lambda i: (0, i, 0)),
                   pl.BlockSpec((tm, FFT_WIDTH), row),
                   pl.BlockSpec((tm, 2 * SGU_WIDTH), row)],
        out_shape=[jax.ShapeDtypeStruct((n, d), BF16),
                   jax.ShapeDtypeStruct((S5_WIDTH // LANES, n // S5_T, S5_T * LANES), BF16),
                   jax.ShapeDtypeStruct((n, FFT_WIDTH), F32),
                   jax.ShapeDtypeStruct((n, 2 * SGU_WIDTH), F32)],
        scratch_shapes=[pltpu.VMEM((S5_WIDTH // LANES, tm, LANES), F32)],
        compiler_params=_cparams(("arbitrary",)),
        name="inproj",
    )(*x_ops, g.reshape(1, d), sh, sc, w_in_bf)


def _s5_operators(a_re, a_im, log_dt, b_re, b_im, c_re, c_im, d_skip, t_len, n_steps):
    lam = lax.complex(a_re.astype(F32), a_im.astype(F32))
    dt = jnp.exp(log_dt.astype(F32))[..., None]
    ldt = lam * dt
    tau = jnp.arange(t_len + 1, dtype=F32)
    pw = jnp.exp(ldt[:, :, None, :] * tau[None, None, :, None])
    a_bar = jnp.exp(ldt)
    bb = ((a_bar - 1.0) / lam)[..., None] * lax.complex(b_re.astype(F32), b_im.astype(F32))
    cc = lax.complex(c_re.astype(F32), c_im.astype(F32))
    k = jnp.real(jnp.einsum('dghp,dgtp,dgpk->dgthk', cc, pw[:, :, :t_len], bb))
    g = S5_GROUPS
    tw = t_len * S5_GROUP
    dd = d_skip.astype(F32).reshape(g, S5_GROUP)
    k0 = k[0][:, :1] + k[1][:, :1] + (jnp.eye(S5_GROUP, dtype=F32)[None] * dd[:, :, None])[:, None]
    kk = jnp.concatenate([k[1][:, :0:-1], k0, k[0][:, 1:]], axis=1)
    kk = kk.transpose(0, 3, 1, 2).reshape(g, S5_GROUP, (2 * t_len - 1) * S5_GROUP).astype(BF16)
    m = jnp.concatenate([kk[:, :, (t_len - 1 - j) * S5_GROUP:(2 * t_len - 1 - j) * S5_GROUP]
                         for j in range(t_len)], axis=1)
    wf = pw[0][:, ::-1][:, 1:, :, None] * bb[0][:, None, :, :]
    wb = pw[1][:, :t_len, :, None] * bb[1][:, None, :, :]

    def state_cols(w):
        w = w.transpose(0, 1, 3, 2)
        return jnp.concatenate([jnp.real(w), jnp.imag(w)], axis=-1).reshape(g, tw, 2 * S5_STATE)

    wop = jnp.concatenate([state_cols(wf), state_cols(wb)], axis=-1).astype(BF16)
    gf = cc[0][:, None, :, :] * pw[0][:, 1:, None, :]
    gb = cc[1][:, None, :, :] * pw[1][:, ::-1][:, :t_len, None, :]

    def state_rows(gm):
        gm = gm.transpose(0, 3, 1, 2).reshape(g, S5_STATE, tw)
        return jnp.concatenate([jnp.real(gm), -jnp.imag(gm)], axis=1)

    vop = jnp.concatenate([state_rows(gf), state_rows(gb)], axis=1).astype(BF16)
    steps = (t_len * (2.0 ** jnp.arange(n_steps, dtype=F32)))
    am = jnp.exp(ldt[:, :, None, :] * steps[None, None, :, None])
    pvec = jnp.concatenate([jnp.real(am), jnp.real(am)], axis=-1)
    qvec = jnp.concatenate([-jnp.imag(am), jnp.imag(am)], axis=-1)
    pq = jnp.stack([pvec, qvec], axis=3)
    pq = pq.transpose(1, 0, 2, 3, 4).reshape(g, 4 * n_steps, 2 * S5_STATE)
    return m, wop, vop, pq


def _s5_kernel(zs_ref, sel_ref, selt_ref, m_ref, wop_ref, vop_ref, pq_ref, y_ref, *, lat_chunks, n_steps):
    nc = zs_ref.shape[0]
    sw = 2 * S5_STATE
    u = jnp.dot(zs_ref[...], sel_ref[...], preferred_element_type=F32).astype(BF16)
    xloc = jnp.dot(u, wop_ref[...], preferred_element_type=F32)
    row = lax.broadcasted_iota(I32, (nc, sw), 0)
    pos_f = jnp.where(row >= lat_chunks, row - lat_chunks, row + (nc - lat_chunks))
    pos_b = (nc - 1) - row

    def chain(xl, pos, base, forward):
        x = jnp.where(pos >= 1, pltpu.roll(xl, 1 if forward else nc - 1, 0), 0.0)
        for m in range(n_steps):
            k = 1 << m
            s = jnp.where(pos >= k, pltpu.roll(x, k if forward else nc - k, 0), 0.0)
            pv = pq_ref[base + 2 * m:base + 2 * m + 1, :]
            qv = pq_ref[base + 2 * m + 1:base + 2 * m + 2, :]
            x = x + pv * s + qv * pltpu.roll(s, S5_STATE, 1)
        return x

    xf = chain(xloc[:, :sw], pos_f, 0, True)
    xb = chain(xloc[:, sw:], pos_b, 2 * n_steps, False)
    xin = jnp.concatenate([xf, xb], axis=1).astype(BF16)
    y = (jnp.dot(u, m_ref[...], preferred_element_type=F32)
         + jnp.dot(xin, vop_ref[...], preferred_element_type=F32))
    contrib = jnp.dot(y.astype(BF16), selt_ref[...], preferred_element_type=F32)

    @pl.when(pl.program_id(0) % S5_PER_BLOCK == 0)
    def _():
        y_ref[...] = contrib

    @pl.when(pl.program_id(0) % S5_PER_BLOCK != 0)
    def _():
        y_ref[...] += contrib


def _s5_select_tables(t_len):
    r = jnp.arange(t_len * LANES, dtype=I32)[None, :, None]
    c = jnp.arange(t_len * S5_GROUP, dtype=I32)[None, None, :]
    gl = jnp.arange(S5_PER_BLOCK, dtype=I32)[:, None, None]
    sel = (r == (c // S5_GROUP) * LANES + gl * S5_GROUP + c % S5_GROUP).astype(BF16)
    return sel, sel.transpose(0, 2, 1)


def _s5_mix(zs, n_lat, sel, selt, m, wop, vop, pq, t_len, n_steps):
    nb, nc, bw = zs.shape
    g = S5_GROUPS
    tw = t_len * S5_GROUP
    blk = lambda i: (i // S5_PER_BLOCK, 0, 0)
    loc = lambda i: (i % S5_PER_BLOCK, 0, 0)
    return pl.pallas_call(
        functools.partial(_s5_kernel, lat_chunks=n_lat // t_len, n_steps=n_steps),
        grid=(g,),
        in_specs=[pl.BlockSpec((None, nc, bw), blk),
                  pl.BlockSpec((None, bw, tw), loc),
                  pl.BlockSpec((None, tw, bw), loc),
                  pl.BlockSpec((None, tw, tw), lambda i: (i, 0, 0)),
                  pl.BlockSpec((None, tw, 4 * S5_STATE), lambda i: (i, 0, 0)),
                  pl.BlockSpec((None, 4 * S5_STATE, tw), lambda i: (i, 0, 0)),
                  pl.BlockSpec((None, 4 * n_steps, 2 * S5_STATE), lambda i: (i, 0, 0))],
        out_specs=pl.BlockSpec((None, nc, bw), blk),
        out_shape=jax.ShapeDtypeStruct((nb, nc, bw), F32),
        compiler_params=_cparams(("arbitrary",)),
        name="s5_mix",
    )(zs, sel, selt, m, wop, vop, pq)


def _dft_tables(n_pos):
    half = n_pos // 2
    blk = min(DFT_BLK, half)
    kk = jnp.arange(half, dtype=I32)[:, None]
    ang = lambda prod: (2.0 * math.pi / n_pos) * (prod % n_pos).astype(F32)
    base = ang(kk * jnp.arange(blk, dtype=I32)[None, :])
    phase = ang(kk * (blk * jnp.arange(half // blk, dtype=I32))[None, :])
    return jnp.cos(base), jnp.sin(base), jnp.cos(phase), jnp.sin(phase)


def _channel_tables():
    q = jnp.arange(FFT_GROUP, dtype=I32)
    ang = (2.0 * math.pi / FFT_GROUP) * ((q[:, None] * q[None, :]) % FFT_GROUP).astype(F32)
    return jnp.cos(ang).astype(BF16), jnp.sin(ang).astype(BF16)


def _chan_dft_kernel(zl_ref, zh_ref, cc_ref, sc_ref, rc_ref, rs_ref):
    w = zl_ref.shape[1]
    for hf, zref in enumerate((zl_ref, zh_ref)):
        z = zref[...].astype(BF16)
        for g in range(FFT_GROUPS):
            sl = slice(g * FFT_GROUP, (g + 1) * FFT_GROUP)
            osl = slice(hf * w + g * FFT_GROUP, hf * w + (g + 1) * FFT_GROUP)
            rc_ref[:, osl] = jnp.dot(z[:, sl], cc_ref[...], preferred_element_type=F32).astype(BF16)
            rs_ref[:, osl] = jnp.dot(z[:, sl], sc_ref[...], preferred_element_type=F32).astype(BF16)


def _posdft_kernel(bc_ref, bs_ref, pc_ref, ps_ref, alt_ref, rc_ref, rs_ref, o_ref, *, scale):
    tk, w = o_ref.shape
    bc = bc_ref[...]
    bs = bs_ref[...]
    cparts, sparts = [], []
    for t in range(pc_ref.shape[1]):
        pc = pc_ref[:, t:t + 1]
        ps = ps_ref[:, t:t + 1]
        cparts.append((pc * bc - ps * bs).astype(BF16))
        sparts.append((ps * bc + pc * bs).astype(BF16))
    cm = jnp.concatenate(cparts, axis=1)
    sm = jnp.concatenate(sparts, axis=1)
    alt = jnp.where(pl.program_id(0) == 1, alt_ref[...], jnp.ones(alt_ref.shape, BF16))
    krow = lax.broadcasted_iota(I32, (tk, w), 0) + pl.program_id(1) * tk
    sign = jnp.where((krow & 1) == 0, 1.0, -1.0)
    u = (jnp.dot(cm * alt, rc_ref[...], preferred_element_type=F32)
         - jnp.dot(sm * alt, rs_ref[...], preferred_element_type=F32))
    o_ref[...] = (u[:, :w] + sign * u[:, w:]) * scale


def _ctx_dft_kernel(z_ref, cc_ref, sc_ref, cm_ref, sm_ref, lat_ref, o_ref, *, scale):
    del lat_ref
    z = z_ref[...].astype(BF16)
    for g in range(FFT_GROUPS):
        sl = slice(g * FFT_GROUP, (g + 1) * FFT_GROUP)
        zc = jnp.dot(z[:, sl], cc_ref[...], preferred_element_type=F32).astype(BF16)
        zs = jnp.dot(z[:, sl], sc_ref[...], preferred_element_type=F32).astype(BF16)
        o_ref[:, sl] = (jnp.dot(cm_ref[...], zc, preferred_element_type=F32)
                        - jnp.dot(sm_ref[...], zs, preferred_element_type=F32)) * scale


def _fourier_latent(z_all, n, tables, cc, sc):
    bcos, bsin, pcos, psin = tables
    w = z_all.shape[1]
    half = n // 2
    assert half % 2 == 0
    tr = min(512, half)
    nb = half // tr
    const = lambda i: (0, 0)
    rc, rs = pl.pallas_call(
        _chan_dft_kernel,
        grid=(nb,),
        in_specs=[pl.BlockSpec((tr, w), lambda i: (i, 0)),
                  pl.BlockSpec((tr, w), lambda i: (i + nb, 0)),
                  pl.BlockSpec((FFT_GROUP, FFT_GROUP), const),
                  pl.BlockSpec((FFT_GROUP, FFT_GROUP), const)],
        out_specs=[pl.BlockSpec((tr, 2 * w), lambda i: (i, 0))] * 2,
        out_shape=[jax.ShapeDtypeStruct((half, 2 * w), BF16)] * 2,
        compiler_params=_cparams(("arbitrary",)),
        name="fft_chan",
    )(z_all, z_all, cc, sc)
    tk = min(FFT_KTILE, half)
    nk = half // tk
    alt = jnp.where(jnp.arange(half) % 2 == 0, 1.0, -1.0).astype(BF16).reshape(1, half)
    out = pl.pallas_call(
        functools.partial(_posdft_kernel, scale=1.0 / math.sqrt(n * FFT_GROUP)),
        grid=(2, nk),
        in_specs=[pl.BlockSpec((tk, bcos.shape[1]), lambda q, i: (i, 0)),
                  pl.BlockSpec((tk, bcos.shape[1]), lambda q, i: (i, 0)),
                  pl.BlockSpec((tk, pcos.shape[1]), lambda q, i: (i, 0)),
                  pl.BlockSpec((tk, pcos.shape[1]), lambda q, i: (i, 0)),
                  pl.BlockSpec((1, half), lambda q, i: (0, 0)),
                  pl.BlockSpec((half, 2 * w), lambda q, i: (0, 0), pipeline_mode=pl.Buffered(1)),
                  pl.BlockSpec((half, 2 * w), lambda q, i: (0, 0), pipeline_mode=pl.Buffered(1))],
        out_specs=pl.BlockSpec((tk, w), lambda q, i: (q * nk + i, 0)),
        out_shape=jax.ShapeDtypeStruct((z_all.shape[0], w), F32),
        compiler_params=_cparams(("arbitrary", "arbitrary")),
        name="fft_posdft",
    )(bcos, bsin, pcos, psin, alt, rc, rs)
    return out


def _fourier_ctx(z_all, y_all, n_lat, cc, sc):
    n_all, w = z_all.shape
    n = n_all - n_lat
    assert n_lat % n == 0
    q = jnp.arange(n, dtype=I32)
    ang = (2.0 * math.pi / n) * ((q[:, None] * q[None, :]) % n).astype(F32)
    cm, sm = jnp.cos(ang).astype(BF16), jnp.sin(ang).astype(BF16)
    full = lambda s: pl.BlockSpec(s, lambda i: (0,) * len(s))
    ctx_rows = pl.BlockSpec((n, w), lambda i: (n_lat // n, 0))
    return pl.pallas_call(
        functools.partial(_ctx_dft_kernel, scale=1.0 / math.sqrt(n * FFT_GROUP)),
        grid=(1,),
        in_specs=[ctx_rows, full((FFT_GROUP, FFT_GROUP)), full((FFT_GROUP, FFT_GROUP)),
                  full((n, n)), full((n, n)), pl.BlockSpec(memory_space=pl.ANY)],
        out_specs=ctx_rows,
        out_shape=jax.ShapeDtypeStruct((n_all, w), F32),
        input_output_aliases={5: 0},
        compiler_params=_cparams(("arbitrary",)),
        name="fft_ctx",
    )(z_all, cc, sc, cm, sm, y_all)


def _merge_kernel(h_ref, zg_ref, ys_ref, yf_ref, wglu_ref, gv_ref, sw_ref, sb_ref,
                  wg0_ref, wg1_ref, wg2_ref, bg0_ref, bg1_ref, bg2_ref,
                  wb0_ref, wb1_ref, wb2_ref, m_ref, feats, yscr):
    j = pl.program_id(1)
    tm = h_ref.shape[0]

    @pl.when(j == 0)
    def _():
        n_chunks = ys_ref.shape[1]
        for b in range(S5_WIDTH // LANES):
            for t in range(S5_T):
                yscr[b, pl.ds(t, n_chunks, stride=S5_T), :] = ys_ref[b, :, t * LANES:(t + 1) * LANES]
        y = _gelu(jnp.concatenate([yscr[b] for b in range(S5_WIDTH // LANES)], axis=1))
        feats[0] = (y * _sigmoid(jnp.dot(y.astype(BF16), wglu_ref[...],
                                         preferred_element_type=F32))).astype(BF16)
        feats[1] = yf_ref[...].astype(BF16)
        gz = _gelu(zg_ref[...])
        u = gz[:, :SGU_WIDTH]
        v = gz[:, SGU_WIDTH:]
        v = v * lax.rsqrt(jnp.mean(v * v, axis=-1, keepdims=True) + EPS) * gv_ref[...]
        vb = v.astype(BF16)
        for c in range(tm // CHUNK):
            rs = slice(c * CHUNK, (c + 1) * CHUNK)
            for hd in range(SGU_HEADS):
                cs = slice(hd * SGU_HEAD, (hd + 1) * SGU_HEAD)
                s = jnp.dot(sw_ref[hd], vb[rs, cs], preferred_element_type=F32) + sb_ref[hd]
                feats[2, rs, cs] = (u[rs, cs] * s).astype(BF16)

    h = h_ref[...]
    acc = None
    for k, (wg, bg, wb) in enumerate(((wg0_ref, bg0_ref, wb0_ref), (wg1_ref, bg1_ref, wb1_ref),
                                      (wg2_ref, bg2_ref, wb2_ref))):
        gate = _sigmoid(jnp.dot(h, wg[...], preferred_element_type=F32) + bg[...])
        br = jnp.dot(feats[k], wb[...], preferred_element_type=F32)
        acc = gate * br if acc is None else acc + gate * br
    m_ref[...] = acc.astype(BF16)


def _merge(h, zg, ys, yf, layer, wglu_bf, gv, sw_bf, sb_full, wgate_bf, bgate, wbranch_bf):
    n, d = h.shape
    tm, tn = MERGE_TILE, MERGE_COLS
    nj = d // tn
    row = lambda i, j: (i, 0)
    const2 = lambda i, j: (0, 0)
    const3 = lambda i, j: (0, 0, 0)
    gate_spec = lambda k: pl.BlockSpec((None, d, tn), lambda i, j: (layer, 0, k * nj + j))
    bias_spec = lambda k: pl.BlockSpec((1, tn), lambda i, j: (0, k * nj + j))
    br_spec = lambda k: pl.BlockSpec((None, None, S5_WIDTH, tn), lambda i, j: (layer, k, 0, j))
    return pl.pallas_call(
        _merge_kernel,
        grid=(n // tm, nj),
        in_specs=[pl.BlockSpec((tm, d), row),
                  pl.BlockSpec((tm, 2 * SGU_WIDTH), row),
                  pl.BlockSpec((S5_WIDTH // LANES, tm // S5_T, S5_T * LANES), lambda i, j: (0, i, 0)),
                  pl.BlockSpec((tm, FFT_WIDTH), row),
                  pl.BlockSpec((None, S5_WIDTH, S5_WIDTH), lambda i, j: (layer, 0, 0)),
                  pl.BlockSpec((1, SGU_WIDTH), const2),
                  pl.BlockSpec((None, SGU_HEADS, CHUNK, CHUNK), lambda i, j: (layer, 0, 0, 0)),
                  pl.BlockSpec((SGU_HEADS, CHUNK, SGU_HEAD), const3),
                  gate_spec(0), gate_spec(1), gate_spec(2),
                  bias_spec(0), bias_spec(1), bias_spec(2),
                  br_spec(0), br_spec(1), br_spec(2)],
        out_specs=pl.BlockSpec((tm, tn), lambda i, j: (i, j)),
        out_shape=jax.ShapeDtypeStruct((n, d), BF16),
        scratch_shapes=[pltpu.VMEM((N_BRANCH, tm, S5_WIDTH), BF16),
                        pltpu.VMEM((S5_WIDTH // LANES, tm, LANES), F32)],
        compiler_params=_cparams(("arbitrary", "arbitrary")),
        name="gated_merge",
    )(h, zg, ys, yf, wglu_bf, gv.reshape(1, SGU_WIDTH), sw_bf, sb_full,
      wgate_bf, wgate_bf, wgate_bf, bgate, bgate, bgate, wbranch_bf, wbranch_bf, wbranch_bf)


def _outproj_kernel(*refs, load_x):
    (m_ref, wo_ref, g1_ref, g2_ref, sh_ref, sc_ref, rw_ref, rb_ref,
     x1_ref, h2_ref, ti_ref, tw_ref) = refs[-12:]
    x1 = load_x(refs[:-12]) + g1_ref[...] * jnp.dot(m_ref[...], wo_ref[...], preferred_element_type=F32)
    x1_ref[...] = x1
    y = x1 * lax.rsqrt(jnp.mean(x1 * x1, axis=-1, keepdims=True) + EPS) * g2_ref[...]
    h2 = y * (1.0 + sc_ref[...]) + sh_ref[...]
    h2_ref[...] = h2
    h_hi = h2.astype(BF16)
    h_lo = (h2 - h_hi.astype(F32)).astype(BF16)
    p_hi = jnp.dot(h_hi, rw_ref[...], preferred_element_type=F32)
    p_lo = jnp.dot(h_lo, rw_ref[:, :LANES], preferred_element_type=F32)
    logits = p_hi[:, :LANES] + p_hi[:, LANES:] + p_lo + rb_ref[...]
    lane = lax.broadcasted_iota(I32, logits.shape, 1)
    vals, idxs = [], []
    for _ in range(TOP_K):
        mx = jnp.max(logits, axis=-1, keepdims=True)
        am = jnp.min(jnp.where(logits == mx, lane, LANES), axis=-1, keepdims=True)
        vals.append(mx)
        idxs.append(am)
        logits = jnp.where(lane == am, -jnp.inf, logits)
    ex = [jnp.exp(v - vals[0]) for v in vals]
    den = ex[0] + ex[1] + ex[2] + ex[3]
    ti = jnp.zeros(lane.shape, I32)
    tw = jnp.zeros(lane.shape, F32)
    for k in range(TOP_K):
        ti = jnp.where(lane == k, idxs[k], ti)
        tw = jnp.where(lane == k, ex[k] / den, tw)
    ti_ref[...] = ti
    tw_ref[...] = tw


def _outproj(x_lat, x_ctx, m, n_lat, layer, wo_bf, g1, g2n, sh2, sc2, rw_pad, rb_pad):
    n, d = m.shape
    tm = ROW_TILE
    lat_tiles = n_lat // tm
    row = lambda i: (i, 0)
    kind = lambda i: (jnp.where(i >= lat_tiles, 1, 0), 0, 0)
    const = lambda i: (0, 0)
    x_ops, x_specs, load_x = _token_rows(x_lat, x_ctx, tm)
    return pl.pallas_call(
        functools.partial(_outproj_kernel, load_x=load_x),
        grid=(n // tm,),
        in_specs=x_specs + [
                  pl.BlockSpec((tm, d), row),
                  pl.BlockSpec((None, d, d), lambda i: (layer, 0, 0)),
                  pl.BlockSpec((None, 1, d), kind),
                  pl.BlockSpec((1, d), const),
                  pl.BlockSpec((None, 1, d), kind),
                  pl.BlockSpec((None, 1, d), kind),
                  pl.BlockSpec((d, 2 * LANES), const),
                  pl.BlockSpec((1, LANES), const)],
        out_specs=[pl.BlockSpec((tm, d), row), pl.BlockSpec((tm, d), row),
                   pl.BlockSpec((tm, LANES), row), pl.BlockSpec((tm, LANES), row)],
        out_shape=[jax.ShapeDtypeStruct((n, d), F32), jax.ShapeDtypeStruct((n, d), F32),
                   jax.ShapeDtypeStruct((n, LANES), I32), jax.ShapeDtypeStruct((n, LANES), F32)],
        compiler_params=_cparams(("arbitrary",)),
        name="outproj_router",
    )(*x_ops, m, wo_bf, g1, g2n.reshape(1, d), sh2, sc2, rw_pad, rb_pad)


def _routing_tables(top_idx, n_items):
    flat_e = top_idx.reshape(-1)
    onehot = (flat_e[:, None] == jnp.arange(N_EXPERTS, dtype=I32)[None, :]).astype(I32)
    csum = jnp.cumsum(onehot, axis=0)
    rank = jnp.sum((csum - onehot) * onehot, axis=1)
    counts = csum[-1]
    padded = (counts + MOE_CH - 1) // MOE_CH * MOE_CH
    pend = jnp.cumsum(padded)
    pstart = pend - padded
    dest = jnp.sum(onehot * pstart[None, :], axis=1) + rank
    per_e = (padded + MOE_RMAX - 1) // MOE_RMAX
    iend = jnp.cumsum(per_e)
    istart = iend - per_e
    total = iend[-1]
    t = jnp.arange(n_items, dtype=I32)
    valid = t < total
    tc = jnp.minimum(t, total - 1)
    e_of = jnp.minimum(jnp.sum((tc[:, None] >= iend[None, :]).astype(I32), axis=1), N_EXPERTS - 1)
    local = tc - istart[e_of]
    row0 = pstart[e_of] + local * MOE_RMAX
    nch = jnp.clip((padded[e_of] - local * MOE_RMAX) // MOE_CH, 0, MOE_RMAX // MOE_CH)
    nch = jnp.where(valid, nch, 0)
    return (dest.astype(I32), counts.astype(I32), pstart.astype(I32), padded.astype(I32),
            e_of.astype(I32), row0.astype(I32), nch.astype(I32))


def _dispatch_kernel(cnt_ref, pst_ref, pad_ref, dest_ref, h_ref, xs_ref, zrow, sem, zsem):
    i = pl.program_id(0)
    tq = h_ref.shape[0]

    @pl.when(i == 0)
    def _():
        zrow[...] = jnp.zeros(zrow.shape, zrow.dtype)

        def per_expert(e, carry):
            first = pst_ref[e] + cnt_ref[e]
            npad = pad_ref[e] - cnt_ref[e]

            def start(r, c):
                pltpu.make_async_copy(zrow.at[pl.ds(0, 1)], xs_ref.at[pl.ds(first + r, 1)], zsem).start()
                return c

            def wait(r, c):
                pltpu.make_async_copy(zrow.at[pl.ds(0, 1)], xs_ref.at[pl.ds(first + r, 1)], zsem).wait()
                return c

            lax.fori_loop(0, npad, start, 0)
            lax.fori_loop(0, npad, wait, 0)
            return carry

        lax.fori_loop(0, N_EXPERTS, per_expert, 0)

    def start(t, c):
        src = h_ref.at[pl.ds(t, 1)]
        for k in range(TOP_K):
            pltpu.make_async_copy(src, xs_ref.at[pl.ds(dest_ref[0, t * TOP_K + k], 1)], sem).start()
        return c

    lax.fori_loop(0, tq, start, 0, unroll=2)
    for _ in range(TOP_K):
        pltpu.make_async_copy(h_ref, xs_ref.at[pl.ds(0, tq)], sem).wait()


def _dispatch(h2, dest, counts, pstart, padded, n_rows):
    d = h2.shape[1]
    tq = DISP_TILE
    nt = dest.shape[0] // (tq * TOP_K)
    dest3 = dest.reshape(nt, 1, tq * TOP_K)
    return pl.pallas_call(
        _dispatch_kernel,
        grid_spec=pltpu.PrefetchScalarGridSpec(
            num_scalar_prefetch=3,
            grid=(nt,),
            in_specs=[pl.BlockSpec((None, 1, tq * TOP_K), lambda i, *_: (i, 0, 0), memory_space=pltpu.SMEM),
                      pl.BlockSpec((tq, d), lambda i, *_: (i, 0))],
            out_specs=pl.BlockSpec(memory_space=pl.ANY),
            scratch_shapes=[pltpu.VMEM((8, d), F32), pltpu.SemaphoreType.DMA(()), pltpu.SemaphoreType.DMA(())]),
        out_shape=jax.ShapeDtypeStruct((n_rows, d), F32),
        compiler_params=_cparams(("arbitrary",)),
        name="moe_dispatch",
    )(counts, pstart, padded, dest3, h2)


def _experts_kernel(ie_ref, ir_ref, in_ref, xs_ref, wg_ref, wu_ref, wd_ref, bg_ref, bu_ref,
                    ys_ref, xstage, xb, acc, pending, sem_in, sem_out, *, units_per_step):
    t = pl.program_id(0)
    j = pl.program_id(1)
    nf = pl.num_programs(1)
    nch = in_ref[t]
    row0 = ir_ref[t]
    ch = MOE_CH

    def hbm_rows(c):
        return pl.ds(pl.multiple_of(row0 + c * ch, ch), ch)

    def vmem_rows(c, size=ch):
        return pl.ds(pl.multiple_of(c * ch, ch), size)

    def in_copy(c, slot):
        return pltpu.make_async_copy(xs_ref.at[hbm_rows(c)], xstage.at[slot], sem_in.at[slot])

    def out_copy(c):
        return pltpu.make_async_copy(acc.at[vmem_rows(c)], ys_ref.at[hbm_rows(c)], sem_out)

    def drain():
        def body(c, carry):
            out_copy(0).wait()
            return carry
        lax.fori_loop(0, pending[0], body, 0)
        pending[0] = 0

    cur = t & 1
    n_items = pl.num_programs(0)
    t_next = jnp.minimum(t + 1, n_items - 1)
    nch_next = jnp.where(t + 1 < n_items, in_ref[t_next], 0)
    row0_next = ir_ref[t_next]

    def next_copy(u, k):
        src = xs_ref.at[pl.ds(pl.multiple_of(row0_next + u * ch, ch), ch)]
        return pltpu.make_async_copy(src, xstage.at[k], sem_in.at[k])

    @pl.when(jnp.logical_and(t == 0, j == 0))
    def _():
        pending[0] = 0

    @pl.when(jnp.logical_and(jnp.logical_and(t == 0, j == 0), nch > 0))
    def _():
        in_copy(0, 0).start()

        def body(c, carry):
            slot = c & 1

            @pl.when(c + 1 < nch)
            def _():
                in_copy(c + 1, 1 - slot).start()

            in_copy(c, slot).wait()
            xb[0, vmem_rows(c), :] = xstage[slot].astype(BF16)
            return carry

        lax.fori_loop(0, nch, body, 0)

    for k in range(units_per_step):
        @pl.when(j * units_per_step + k < nch_next)
        def _():
            next_copy(j * units_per_step + k, k).start()

    @pl.when(nch > 0)
    def _():
        def partial_out(c, size):
            rows = vmem_rows(c, size)
            x = xb[cur, rows, :]
            gate = jnp.dot(x, wg_ref[...].astype(BF16), preferred_element_type=F32) + bg_ref[...]
            up = jnp.dot(x, wu_ref[...].astype(BF16), preferred_element_type=F32) + bu_ref[...]
            gate = jnp.minimum(gate, SWIGLU_LIMIT)
            up = jnp.clip(up, -SWIGLU_LIMIT, SWIGLU_LIMIT)
            act = gate * _sigmoid(SWIGLU_ALPHA * gate) * (up + 1.0)
            return rows, jnp.dot(act.astype(BF16), wd_ref[...].astype(BF16), preferred_element_type=F32)

        def accumulate(first):
            def block(c, units):
                rows, part = partial_out(c, units * ch)
                if first:
                    acc[rows, :] = part
                else:
                    acc[rows, :] += part

                    @pl.when(j == nf - 1)
                    def _():
                        for u in range(units):
                            out_copy(c + u).start()

            def body(c4, carry):
                block(4 * c4, 4)
                return carry

            quads = lax.shift_right_logical(nch, 2)
            lax.fori_loop(0, quads, body, 0)

            @pl.when((nch & 2) != 0)
            def _():
                block(4 * quads, 2)

            @pl.when((nch & 1) != 0)
            def _():
                block(nch - 1, 1)

        @pl.when(j == 0)
        def _():
            drain()
            accumulate(True)

        @pl.when(j > 0)
        def _():
            accumulate(False)

        @pl.when(j == nf - 1)
        def _():
            pending[0] = nch

    for k in range(units_per_step):
        @pl.when(j * units_per_step + k < nch_next)
        def _():
            next_copy(j * units_per_step + k, k).wait()
            xb[1 - cur, vmem_rows(j * units_per_step + k), :] = xstage[k].astype(BF16)

    @pl.when(jnp.logical_and(t == n_items - 1, j == nf - 1))
    def _():
        drain()


def _experts(xs, layer, w_up, b_up, w_down, item_e, item_row0, item_nch):
    n_rows, d = xs.shape
    n_layers, n_exp, _, two_f = w_up.shape
    f = two_f // 2
    tf = MOE_TF
    nf = f // tf
    assert nf >= 2
    units_per_step = max(2, -(-(MOE_RMAX // MOE_CH) // nf))
    n_items = item_e.shape[0]
    b_up4 = b_up.reshape(n_layers, n_exp, 1, two_f)

    def jj(t, j, ie, ir, inn):
        return jnp.where(inn[t] > 0, j, nf - 1)

    return pl.pallas_call(
        functools.partial(_experts_kernel, units_per_step=units_per_step),
        grid_spec=pltpu.PrefetchScalarGridSpec(
            num_scalar_prefetch=3,
            grid=(n_items, nf),
            in_specs=[pl.BlockSpec(memory_space=pl.ANY),
                      pl.BlockSpec((None, None, d, tf), lambda t, j, ie, ir, inn: (layer, ie[t], 0, jj(t, j, ie, ir, inn))),
                      pl.BlockSpec((None, None, d, tf), lambda t, j, ie, ir, inn: (layer, ie[t], 0, nf + jj(t, j, ie, ir, inn))),
                      pl.BlockSpec((None, None, tf, d), lambda t, j, ie, ir, inn: (layer, ie[t], jj(t, j, ie, ir, inn), 0)),
                      pl.BlockSpec((None, None, 1, tf), lambda t, j, ie, ir, inn: (layer, ie[t], 0, jj(t, j, ie, ir, inn))),
                      pl.BlockSpec((None, None, 1, tf), lambda t, j, ie, ir, inn: (layer, ie[t], 0, nf + jj(t, j, ie, ir, inn)))],
            out_specs=pl.BlockSpec(memory_space=pl.ANY),
            scratch_shapes=[pltpu.VMEM((units_per_step, MOE_CH, d), F32),
                            pltpu.VMEM((2, MOE_RMAX, d), BF16),
                            pltpu.VMEM((MOE_RMAX, d), F32),
                            pltpu.SMEM((1,), I32),
                            pltpu.SemaphoreType.DMA((units_per_step,)),
                            pltpu.SemaphoreType.DMA(())]),
        out_shape=jax.ShapeDtypeStruct((n_rows, d), F32),
        compiler_params=_cparams(("arbitrary", "arbitrary")),
        name="moe_experts",
    )(item_e, item_row0, item_nch, xs, w_up, w_up, w_down, b_up4, b_up4)


def _combine_kernel(dc_ref, dn_ref, tw_ref, ti_ref, x1_ref, g2_ref, fg_ref, bd_ref, ys_ref, o_ref,
                    gbuf, sem, *, final):
    i = pl.program_id(0)
    n = pl.num_programs(0)
    tq = x1_ref.shape[0]

    def issue(dref, slot):
        def body(t, c):
            for k in range(TOP_K):
                pltpu.make_async_copy(ys_ref.at[pl.ds(dref[0, t * TOP_K + k], 1)],
                                      gbuf.at[slot, k, pl.ds(t, 1)], sem.at[slot]).start()
            return c
        lax.fori_loop(0, tq, body, 0, unroll=2)

    slot = i & 1

    @pl.when(i == 0)
    def _():
        issue(dc_ref, 0)

    @pl.when(i + 1 < n)
    def _():
        issue(dn_ref, 1 - slot)

    for k in range(TOP_K):
        pltpu.make_async_copy(ys_ref.at[pl.ds(0, tq)], gbuf.at[slot, k], sem.at[slot]).wait()
    tw = tw_ref[...]
    ti = ti_ref[...]
    lane = lax.broadcasted_iota(I32, tw.shape, 1)
    ew = jnp.zeros(tw.shape, F32)
    for k in range(TOP_K):
        ew = ew + jnp.where(lane == ti[:, k:k + 1], tw[:, k:k + 1], 0.0)
    y = jnp.dot(ew.astype(BF16), bd_ref[...], preferred_element_type=F32)
    for k in range(TOP_K):
        y = y + tw[:, k:k + 1] * gbuf[slot, k]
    x2 = x1_ref[...] + g2_ref[...] * y
    if final:
        x2 = x2 * lax.rsqrt(jnp.mean(x2 * x2, axis=-1, keepdims=True) + EPS) * fg_ref[...]
    o_ref[...] = x2


def _combine(ys, dest, top_w, top_idx, b_down, x1, n_lat, g2, final_g, final):
    n, d = x1.shape
    bd_pad = jnp.zeros((LANES, d), BF16).at[:N_EXPERTS].set(b_down.astype(BF16))
    tq = DISP_TILE
    lat_tiles = n_lat // tq
    dest3 = dest.reshape(-1, 1, tq * TOP_K)
    if final:
        n = n_lat
    nt = n // tq
    kind = lambda i: (jnp.where(i >= lat_tiles, 1, 0), 0, 0)
    return pl.pallas_call(
        functools.partial(_combine_kernel, final=final),
        grid=(nt,),
        in_specs=[pl.BlockSpec((None, 1, tq * TOP_K), lambda i: (i, 0, 0), memory_space=pltpu.SMEM),
                  pl.BlockSpec((None, 1, tq * TOP_K), lambda i: (jnp.minimum(i + 1, nt - 1), 0, 0),
                               memory_space=pltpu.SMEM),
                  pl.BlockSpec((tq, LANES), lambda i: (i, 0)),
                  pl.BlockSpec((tq, LANES), lambda i: (i, 0)),
                  pl.BlockSpec((tq, d), lambda i: (i, 0)),
                  pl.BlockSpec((None, 1, d), kind),
                  pl.BlockSpec((1, d), lambda i: (0, 0)),
                  pl.BlockSpec((LANES, d), lambda i: (0, 0)),
                  pl.BlockSpec(memory_space=pl.ANY)],
        out_specs=pl.BlockSpec((tq, d), lambda i: (i, 0)),
        out_shape=jax.ShapeDtypeStruct((n, d), F32),
        scratch_shapes=[pltpu.VMEM((2, TOP_K, tq, d), F32), pltpu.SemaphoreType.DMA((2,))],
        compiler_params=_cparams(("arbitrary",)),
        name="moe_combine",
    )(dest3, dest3, top_w, top_idx, x1, g2, final_g.reshape(1, d), bd_pad, ys)


def kernel(x, c, ctx, c_ctx, w_mod, b_mod, norm1_g, norm2_g, w_in, s5_a_re, s5_a_im, s5_log_dt,
           s5_b_re, s5_b_im, s5_c_re, s5_c_im, s5_d, s5_w_glu, sgu_norm_g, sgu_w, sgu_b, w_branch,
           w_gate, b_gate, w_out, router_w, router_b, moe_w_up, moe_b_up, moe_w_down, moe_b_down,
           final_g):
    bsz, n_lat, d = x.shape
    n_ctx = ctx.shape[1]
    assert bsz == 1
    n_layers = w_mod.shape[0]
    n = n_lat + n_ctx
    x_lat, x_ctx = x.reshape(n_lat, d).astype(F32), ctx.reshape(n_ctx, d).astype(F32)
    w_in_bf, w_out_bf, w_gate_bf, w_branch_bf = (t.astype(BF16) for t in (w_in, w_out, w_gate, w_branch))
    w_glu_bf, sgu_w_bf = s5_w_glu.astype(BF16), sgu_w.astype(BF16)

    cond8 = jnp.zeros((8, d), F32).at[0].set(c[0]).at[1].set(c_ctx)
    mods = _adaln(cond8, w_mod, b_mod)[:, :2, :].reshape(n_layers, 2, N_MOD, 1, d)

    n_chunks = n // S5_T
    n_steps = max(1, (n_chunks - 1).bit_length())
    dft_tables = _dft_tables(n_lat)
    cc, sc = _channel_tables()
    sel, selt = _s5_select_tables(S5_T)
    n_assign = n * TOP_K
    n_rows = n_assign + N_EXPERTS * MOE_CH
    n_items = -(-n_assign // MOE_RMAX) + N_EXPERTS

    for i in range(n_layers):
        last = i == n_layers - 1
        sh1, sc1, g1, sh2, sc2, g2 = [mods[i, :, k] for k in range(N_MOD)]

        h, z_s5, z_fft, z_sgu = _inproj(x_lat, x_ctx, n, n_lat, i, norm1_g[i], sh1, sc1, w_in_bf)
        mtz, wop, vop, pq = _s5_operators(s5_a_re[i], s5_a_im[i], s5_log_dt[i], s5_b_re[i], s5_b_im[i],
                                     s5_c_re[i], s5_c_im[i], s5_d[i], S5_T, n_steps)
        y_s5 = _s5_mix(z_s5, n_lat, sel, selt, mtz, wop, vop, pq, S5_T, n_steps)
        y_fft = _fourier_ctx(z_fft, _fourier_latent(z_fft, n_lat, dft_tables, cc, sc), n_lat, cc, sc)
        sb_full = jnp.broadcast_to(sgu_b[i].astype(F32)[:, :, None], (SGU_HEADS, CHUNK, SGU_HEAD))
        m = _merge(h, z_sgu, y_s5, y_fft, i, w_glu_bf, sgu_norm_g[i].astype(F32), sgu_w_bf, sb_full,
                   w_gate_bf, b_gate[i].astype(F32).reshape(1, N_BRANCH * d), w_branch_bf)

        rw32 = jnp.zeros((d, LANES), F32).at[:, :N_EXPERTS].set(router_w[i].astype(F32))
        rw_hi = rw32.astype(BF16)
        rw_pad = jnp.concatenate([rw_hi, (rw32 - rw_hi.astype(F32)).astype(BF16)], axis=1)
        rb_pad = jnp.full((1, LANES), -1e30, F32).at[0, :N_EXPERTS].set(router_b[i].astype(F32))
        x1, h2, top_idx, top_w = _outproj(x_lat, x_ctx, m, n_lat, i, w_out_bf, g1, norm2_g[i].astype(F32),
                                          sh2, sc2, rw_pad, rb_pad)
        n_routed = n_lat if last else n
        dest, counts, pstart, padded, item_e, item_row0, item_nch = _routing_tables(
            top_idx[:n_routed, :TOP_K], n_items)
        xs = _dispatch(h2, dest, counts, pstart, padded, n_rows)
        ys = _experts(xs, i, moe_w_up, moe_b_up, moe_w_down, item_e, item_row0, item_nch)
        x_lat = _combine(ys, dest, top_w, top_idx, moe_b_down[i], x1, n_lat, g2, final_g.astype(F32), last)
        x_ctx = None

    return x_lat.reshape(bsz, n_lat, d).astype(x.dtype)
```

```python
import functools
import math

import jax
import jax.numpy as jnp
from jax import lax
from jax.experimental import pallas as pl
from jax.experimental.pallas import tpu as pltpu

F32 = jnp.float32
BF16 = jnp.bfloat16
I32 = jnp.int32

EPS = 1e-6
S5_WIDTH = 512
S5_GROUP = 16
S5_GROUPS = 32
S5_STATE = 64
FFT_WIDTH = 512
FFT_GROUPS = 4
FFT_GROUP = 128
SGU_WIDTH = 512
SGU_HEADS = 4
SGU_HEAD = 128
CHUNK = 128
N_BRANCH = 3
N_EXPERTS = 32
TOP_K = 4
SWIGLU_LIMIT = 7.0
SWIGLU_ALPHA = 1.702
N_MOD = 6

LANES = 128
S5_PER_BLOCK = LANES // S5_GROUP
VMEM_LIMIT_BYTES = 56 * 1024 * 1024

S5_T = 16
ROW_TILE = 256
MERGE_TILE = 384
MERGE_COLS = 512
FFT_KTILE = 256
DFT_BLK = 256
MOE_CH = 128
MOE_RMAX = 1536
MOE_TF = 512
DISP_TILE = 256


def _cparams(sem, vmem=VMEM_LIMIT_BYTES):
    return pltpu.CompilerParams(dimension_semantics=sem, vmem_limit_bytes=vmem)


def _gelu(x):
    return 0.5 * x * (1.0 + jnp.tanh(math.sqrt(2.0 / math.pi) * (x + 0.044715 * x * x * x)))


def _sigmoid(x):
    return 1.0 / (1.0 + jnp.exp(-x))


def _adaln_kernel(c_ref, w_ref, b_ref, o_ref):
    c = c_ref[...]
    s = c * _sigmoid(c)
    o_ref[...] = jnp.dot(s, w_ref[...], preferred_element_type=F32,
                         precision=lax.Precision.HIGHEST) + b_ref[...]


def _adaln(cond8, w_mod, b_mod):
    n_layers, d, nm = w_mod.shape
    tn = 1024
    return pl.pallas_call(
        _adaln_kernel,
        grid=(n_layers, nm // tn),
        in_specs=[pl.BlockSpec((8, d), lambda l, j: (0, 0)),
                  pl.BlockSpec((None, d, tn), lambda l, j: (l, 0, j)),
                  pl.BlockSpec((None, 1, tn), lambda l, j: (l, 0, j))],
        out_specs=pl.BlockSpec((None, 8, tn), lambda l, j: (l, 0, j)),
        out_shape=jax.ShapeDtypeStruct((n_layers, 8, nm), F32),
        compiler_params=_cparams(("arbitrary", "arbitrary")),
        name="adaln",
    )(cond8, w_mod, b_mod.reshape(n_layers, 1, nm))


def _token_rows(x_lat, x_ctx, tm):
    d = x_lat.shape[1]
    if x_ctx is None:
        return (x_lat,), [pl.BlockSpec((tm, d), lambda i: (i, 0))], lambda refs: refs[0][...]
    lat_tiles = x_lat.shape[0] // tm
    specs = [pl.BlockSpec((tm, d), lambda i: (jnp.minimum(i, lat_tiles - 1), 0)),
             pl.BlockSpec((tm, d), lambda i: (jnp.maximum(i - lat_tiles, 0), 0))]
    load = lambda refs: jnp.where(pl.program_id(0) >= lat_tiles, refs[1][...], refs[0][...])
    return (x_lat, x_ctx), specs, load


def _inproj_kernel(*refs, load_x):
    g_ref, sh_ref, sc_ref, w_ref, h_ref, zs_ref, zf_ref, zg_ref, zscr = refs[-9:]
    x = load_x(refs[:-9])
    y = x * lax.rsqrt(jnp.mean(x * x, axis=-1, keepdims=True) + EPS) * g_ref[...]
    h = (y * (1.0 + sc_ref[...]) + sh_ref[...]).astype(BF16)
    h_ref[...] = h
    z = jnp.dot(h, w_ref[...], preferred_element_type=F32)
    zf_ref[...] = z[:, S5_WIDTH:S5_WIDTH + FFT_WIDTH]
    zg_ref[...] = z[:, S5_WIDTH + FFT_WIDTH:]
    n_chunks = zs_ref.shape[1]
    for b in range(S5_WIDTH // LANES):
        zscr[b] = z[:, b * LANES:(b + 1) * LANES]
        for j in range(S5_T):
            zs_ref[b, :, j * LANES:(j + 1) * LANES] = \
                zscr[b, pl.ds(j, n_chunks, stride=S5_T), :].astype(BF16)


def _inproj(x_lat, x_ctx, n, n_lat, layer, g, sh, sc, w_in_bf):
    d = x_lat.shape[1]
    tm = ROW_TILE
    lat_tiles = n_lat // tm
    row = lambda i: (i, 0)
    kind = lambda i: (jnp.where(i >= lat_tiles, 1, 0), 0, 0)
    x_ops, x_specs, load_x = _token_rows(x_lat, x_ctx, tm)
    return pl.pallas_call(
        functools.partial(_inproj_kernel, load_x=load_x),
        grid=(n // tm,),
        in_specs=x_specs + [
                  pl.BlockSpec((1, d), lambda i: (0, 0)),
                  pl.BlockSpec((None, 1, d), kind),
                  pl.BlockSpec((None, 1, d), kind),
                  pl.BlockSpec((None, d, w_in_bf.shape[2]), lambda i: (layer, 0, 0))],
        out_specs=[pl.BlockSpec((tm, d), row),
                   pl.BlockSpec((S5_WIDTH // LANES, tm // S5_T, S5_T * LANES), lambda i: (0, i, 0)),
                   pl.BlockSpec((tm, FFT_WIDTH), row),
                   pl.BlockSpec((tm, 2 * SGU_WIDTH), row)],
        out_shape=[jax.ShapeDtypeStruct((n, d), BF16),
                   jax.ShapeDtypeStruct((S5_WIDTH // LANES, n // S5_T, S5_T * LANES), BF16),
                   jax.ShapeDtypeStruct((n, FFT_WIDTH), F32),
                   jax.ShapeDtypeStruct((n, 2 * SGU_WIDTH), F32)],
        scratch_shapes=[pltpu.VMEM((S5_WIDTH // LANES, tm, LANES), F32)],
        compiler_params=_cparams(("arbitrary",)),
        name="inproj",
    )(*x_ops, g.reshape(1, d), sh, sc, w_in_bf)


def _s5_operators(a_re, a_im, log_dt, b_re, b_im, c_re, c_im, d_skip, t_len, n_steps):
    lam = lax.complex(a_re.astype(F32), a_im.astype(F32))
    dt = jnp.exp(log_dt.astype(F32))[..., None]
    ldt = lam * dt
    tau = jnp.arange(t_len + 1, dtype=F32)
    pw = jnp.exp(ldt[:, :, None, :] * tau[None, None, :, None])
    a_bar = jnp.exp(ldt)
    bb = ((a_bar - 1.0) / lam)[..., None] * lax.complex(b_re.astype(F32), b_im.astype(F32))
    cc = lax.complex(c_re.astype(F32), c_im.astype(F32))
    k = jnp.real(jnp.einsum('dghp,dgtp,dgpk->dgthk', cc, pw[:, :, :t_len], bb))
    g = S5_GROUPS
    tw = t_len * S5_GROUP
    dd = d_skip.astype(F32).reshape(g, S5_GROUP)
    k0 = k[0][:, :1] + k[1][:, :1] + (jnp.eye(S5_GROUP, dtype=F32)[None] * dd[:, :, None])[:, None]
    kk = jnp.concatenate([k[1][:, :0:-1], k0, k[0][:, 1:]], axis=1)
    kk = kk.transpose(0, 3, 1, 2).reshape(g, S5_GROUP, (2 * t_len - 1) * S5_GROUP).astype(BF16)
    m = jnp.concatenate([kk[:, :, (t_len - 1 - j) * S5_GROUP:(2 * t_len - 1 - j) * S5_GROUP]
                         for j in range(t_len)], axis=1)
    wf = pw[0][:, ::-1][:, 1:, :, None] * bb[0][:, None, :, :]
    wb = pw[1][:, :t_len, :, None] * bb[1][:, None, :, :]

    def state_cols(w):
        w = w.transpose(0, 1, 3, 2)
        return jnp.concatenate([jnp.real(w), jnp.imag(w)], axis=-1).reshape(g, tw, 2 * S5_STATE)

    wop = jnp.concatenate([state_cols(wf), state_cols(wb)], axis=-1).astype(BF16)
    gf = cc[0][:, None, :, :] * pw[0][:, 1:, None, :]
    gb = cc[1][:, None, :, :] * pw[1][:, ::-1][:, :t_len, None, :]

    def state_rows(gm):
        gm = gm.transpose(0, 3, 1, 2).reshape(g, S5_STATE, tw)
        return jnp.concatenate([jnp.real(gm), -jnp.imag(gm)], axis=1)

    vop = jnp.concatenate([state_rows(gf), state_rows(gb)], axis=1).astype(BF16)
    steps = (t_len * (2.0 ** jnp.arange(n_steps, dtype=F32)))
    am = jnp.exp(ldt[:, :, None, :] * steps[None, None, :, None])
    pvec = jnp.concatenate([jnp.real(am), jnp.real(am)], axis=-1)
    qvec = jnp.concatenate([-jnp.imag(am), jnp.imag(am)], axis=-1)
    pq = jnp.stack([pvec, qvec], axis=3)
    pq = pq.transpose(1, 0, 2, 3, 4).reshape(g, 4 * n_steps, 2 * S5_STATE)
    return m, wop, vop, pq


def _s5_kernel(zs_ref, sel_ref, selt_ref, m_ref, wop_ref, vop_ref, pq_ref, y_ref, *, lat_chunks, n_steps):
    nc = zs_ref.shape[0]
    sw = 2 * S5_STATE
    u = jnp.dot(zs_ref[...], sel_ref[...], preferred_element_type=F32).astype(BF16)
    xloc = jnp.dot(u, wop_ref[...], preferred_element_type=F32)
    row = lax.broadcasted_iota(I32, (nc, sw), 0)
    pos_f = jnp.where(row >= lat_chunks, row - lat_chunks, row + (nc - lat_chunks))
    pos_b = (nc - 1) - row

    def chain(xl, pos, base, forward):
        x = jnp.where(pos >= 1, pltpu.roll(xl, 1 if forward else nc - 1, 0), 0.0)
        for m in range(n_steps):
            k = 1 << m
            s = jnp.where(pos >= k, pltpu.roll(x, k if forward else nc - k, 0), 0.0)
            pv = pq_ref[base + 2 * m:base + 2 * m + 1, :]
            qv = pq_ref[base + 2 * m + 1:base + 2 * m + 2, :]
            x = x + pv * s + qv * pltpu.roll(s, S5_STATE, 1)
        return x

    xf = chain(xloc[:, :sw], pos_f, 0, True)
    xb = chain(xloc[:, sw:], pos_b, 2 * n_steps, False)
    xin = jnp.concatenate([xf, xb], axis=1).astype(BF16)
    y = (jnp.dot(u, m_ref[...], preferred_element_type=F32)
         + jnp.dot(xin, vop_ref[...], preferred_element_type=F32))
    contrib = jnp.dot(y.astype(BF16), selt_ref[...], preferred_element_type=F32)

    @pl.when(pl.program_id(0) % S5_PER_BLOCK == 0)
    def _():
        y_ref[...] = contrib

    @pl.when(pl.program_id(0) % S5_PER_BLOCK != 0)
    def _():
        y_ref[...] += contrib


def _s5_select_tables(t_len):
    r = jnp.arange(t_len * LANES, dtype=I32)[None, :, None]
    c = jnp.arange(t_len * S5_GROUP, dtype=I32)[None, None, :]
    gl = jnp.arange(S5_PER_BLOCK, dtype=I32)[:, None, None]
    sel = (r == (c // S5_GROUP) * LANES + gl * S5_GROUP + c % S5_GROUP).astype(BF16)
    return sel, sel.transpose(0, 2, 1)


def _s5_mix(zs, n_lat, sel, selt, m, wop, vop, pq, t_len, n_steps):
    nb, nc, bw = zs.shape
    g = S5_GROUPS
    tw = t_len * S5_GROUP
    blk = lambda i: (i // S5_PER_BLOCK, 0, 0)
    loc = lambda i: (i % S5_PER_BLOCK, 0, 0)
    return pl.pallas_call(
        functools.partial(_s5_kernel, lat_chunks=n_lat // t_len, n_steps=n_steps),
        grid=(g,),
        in_specs=[pl.BlockSpec((None, nc, bw), blk),
                  pl.BlockSpec((None, bw, tw), loc),
                  pl.BlockSpec((None, tw, bw), loc),
                  pl.BlockSpec((None, tw, tw), lambda i: (i, 0, 0)),
                  pl.BlockSpec((None, tw, 4 * S5_STATE), lambda i: (i, 0, 0)),
                  pl.BlockSpec((None, 4 * S5_STATE, tw), lambda i: (i, 0, 0)),
                  pl.BlockSpec((None, 4 * n_steps, 2 * S5_STATE), lambda i: (i, 0, 0))],
        out_specs=pl.BlockSpec((None, nc, bw), blk),
        out_shape=jax.ShapeDtypeStruct((nb, nc, bw), F32),
        compiler_params=_cparams(("arbitrary",)),
        name="s5_mix",
    )(zs, sel, selt, m, wop, vop, pq)


def _dft_tables(n_pos):
    half = n_pos // 2
    blk = min(DFT_BLK, half)
    kk = jnp.arange(half, dtype=I32)[:, None]
    ang = lambda prod: (2.0 * math.pi / n_pos) * (prod % n_pos).astype(F32)
    base = ang(kk * jnp.arange(blk, dtype=I32)[None, :])
    phase = ang(kk * (blk * jnp.arange(half // blk, dtype=I32))[None, :])
    return jnp.cos(base), jnp.sin(base), jnp.cos(phase), jnp.sin(phase)


def _channel_tables():
    q = jnp.arange(FFT_GROUP, dtype=I32)
    ang = (2.0 * math.pi / FFT_GROUP) * ((q[:, None] * q[None, :]) % FFT_GROUP).astype(F32)
    return jnp.cos(ang).astype(BF16), jnp.sin(ang).astype(BF16)


def _chan_dft_kernel(zl_ref, zh_ref, cc_ref, sc_ref, rc_ref, rs_ref):
    w = zl_ref.shape[1]
    for hf, zref in enumerate((zl_ref, zh_ref)):
        z = zref[...].astype(BF16)
        for g in range(FFT_GROUPS):
            sl = slice(g * FFT_GROUP, (g + 1) * FFT_GROUP)
            osl = slice(hf * w + g * FFT_GROUP, hf * w + (g + 1) * FFT_GROUP)
            rc_ref[:, osl] = jnp.dot(z[:, sl], cc_ref[...], preferred_element_type=F32).astype(BF16)
            rs_ref[:, osl] = jnp.dot(z[:, sl], sc_ref[...], preferred_element_type=F32).astype(BF16)


def _posdft_kernel(bc_ref, bs_ref, pc_ref, ps_ref, alt_ref, rc_ref, rs_ref, o_ref, *, scale):
    tk, w = o_ref.shape
    bc = bc_ref[...]
    bs = bs_ref[...]
    cparts, sparts = [], []
    for t in range(pc_ref.shape[1]):
        pc = pc_ref[:, t:t + 1]
        ps = ps_ref[:, t:t + 1]
        cparts.append((pc * bc - ps * bs).astype(BF16))
        sparts.append((ps * bc + pc * bs).astype(BF16))
    cm = jnp.concatenate(cparts, axis=1)
    sm = jnp.concatenate(sparts, axis=1)
    alt = jnp.where(pl.program_id(0) == 1, alt_ref[...], jnp.ones(alt_ref.shape, BF16))
    krow = lax.broadcasted_iota(I32, (tk, w), 0) + pl.program_id(1) * tk
    sign = jnp.where((krow & 1) == 0, 1.0, -1.0)
    u = (jnp.dot(cm * alt, rc_ref[...], preferred_element_type=F32)
         - jnp.dot(sm * alt, rs_ref[...], preferred_element_type=F32))
    o_ref[...] = (u[:, :w] + sign * u[:, w:]) * scale


def _ctx_dft_kernel(z_ref, cc_ref, sc_ref, cm_ref, sm_ref, lat_ref, o_ref, *, scale):
    del lat_ref
    z = z_ref[...].astype(BF16)
    for g in range(FFT_GROUPS):
        sl = slice(g * FFT_GROUP, (g + 1) * FFT_GROUP)
        zc = jnp.dot(z[:, sl], cc_ref[...], preferred_element_type=F32).astype(BF16)
        zs = jnp.dot(z[:, sl], sc_ref[...], preferred_element_type=F32).astype(BF16)
        o_ref[:, sl] = (jnp.dot(cm_ref[...], zc, preferred_element_type=F32)
                        - jnp.dot(sm_ref[...], zs, preferred_element_type=F32)) * scale


def _fourier_latent(z_all, n, tables, cc, sc):
    bcos, bsin, pcos, psin = tables
    w = z_all.shape[1]
    half = n // 2
    assert half % 2 == 0
    tr = min(512, half)
    nb = half // tr
    const = lambda i: (0, 0)
    rc, rs = pl.pallas_call(
        _chan_dft_kernel,
        grid=(nb,),
        in_specs=[pl.BlockSpec((tr, w), lambda i: (i, 0)),
                  pl.BlockSpec((tr, w), lambda i: (i + nb, 0)),
                  pl.BlockSpec((FFT_GROUP, FFT_GROUP), const),
                  pl.BlockSpec((FFT_GROUP, FFT_GROUP), const)],
        out_specs=[pl.BlockSpec((tr, 2 * w), lambda i: (i, 0))] * 2,
        out_shape=[jax.ShapeDtypeStruct((half, 2 * w), BF16)] * 2,
        compiler_params=_cparams(("arbitrary",)),
        name="fft_chan",
    )(z_all, z_all, cc, sc)
    tk = min(FFT_KTILE, half)
    nk = half // tk
    alt = jnp.where(jnp.arange(half) % 2 == 0, 1.0, -1.0).astype(BF16).reshape(1, half)
    out = pl.pallas_call(
        functools.partial(_posdft_kernel, scale=1.0 / math.sqrt(n * FFT_GROUP)),
        grid=(2, nk),
        in_specs=[pl.BlockSpec((tk, bcos.shape[1]), lambda q, i: (i, 0)),
                  pl.BlockSpec((tk, bcos.shape[1]), lambda q, i: (i, 0)),
                  pl.BlockSpec((tk, pcos.shape[1]), lambda q, i: (i, 0)),
                  pl.BlockSpec((tk, pcos.shape[1]), lambda q, i: (i, 0)),
                  pl.BlockSpec((1, half), lambda q, i: (0, 0)),
                  pl.BlockSpec((half, 2 * w), lambda q, i: (0, 0), pipeline_mode=pl.Buffered(1)),
                  pl.BlockSpec((half, 2 * w), lambda q, i: (0, 0), pipeline_mode=pl.Buffered(1))],
        out_specs=pl.BlockSpec((tk, w), lambda q, i: (q * nk + i, 0)),
        out_shape=jax.ShapeDtypeStruct((z_all.shape[0], w), F32),
        compiler_params=_cparams(("arbitrary", "arbitrary")),
        name="fft_posdft",
    )(bcos, bsin, pcos, psin, alt, rc, rs)
    return out


def _fourier_ctx(z_all, y_all, n_lat, cc, sc):
    n_all, w = z_all.shape
    n = n_all - n_lat
    assert n_lat % n == 0
    q = jnp.arange(n, dtype=I32)
    ang = (2.0 * math.pi / n) * ((q[:, None] * q[None, :]) % n).astype(F32)
    cm, sm = jnp.cos(ang).astype(BF16), jnp.sin(ang).astype(BF16)
    full = lambda s: pl.BlockSpec(s, lambda i: (0,) * len(s))
    ctx_rows = pl.BlockSpec((n, w), lambda i: (n_lat // n, 0))
    return pl.pallas_call(
        functools.partial(_ctx_dft_kernel, scale=1.0 / math.sqrt(n * FFT_GROUP)),
        grid=(1,),
        in_specs=[ctx_rows, full((FFT_GROUP, FFT_GROUP)), full((FFT_GROUP, FFT_GROUP)),
                  full((n, n)), full((n, n)), pl.BlockSpec(memory_space=pl.ANY)],
        out_specs=ctx_rows,
        out_shape=jax.ShapeDtypeStruct((n_all, w), F32),
        input_output_aliases={5: 0},
        compiler_params=_cparams(("arbitrary",)),
        name="fft_ctx",
    )(z_all, cc, sc, cm, sm, y_all)


def _merge_kernel(h_ref, zg_ref, ys_ref, yf_ref, wglu_ref, gv_ref, sw_ref, sb_ref,
                  wg0_ref, wg1_ref, wg2_ref, bg0_ref, bg1_ref, bg2_ref,
                  wb0_ref, wb1_ref, wb2_ref, m_ref, feats, yscr):
    j = pl.program_id(1)
    tm = h_ref.shape[0]

    @pl.when(j == 0)
    def _():
        n_chunks = ys_ref.shape[1]
        for b in range(S5_WIDTH // LANES):
            for t in range(S5_T):
                yscr[b, pl.ds(t, n_chunks, stride=S5_T), :] = ys_ref[b, :, t * LANES:(t + 1) * LANES]
        y = _gelu(jnp.concatenate([yscr[b] for b in range(S5_WIDTH // LANES)], axis=1))
        feats[0] = (y * _sigmoid(jnp.dot(y.astype(BF16), wglu_ref[...],
                                         preferred_element_type=F32))).astype(BF16)
        feats[1] = yf_ref[...].astype(BF16)
        gz = _gelu(zg_ref[...])
        u = gz[:, :SGU_WIDTH]
        v = gz[:, SGU_WIDTH:]
        v = v * lax.rsqrt(jnp.mean(v * v, axis=-1, keepdims=True) + EPS) * gv_ref[...]
        vb = v.astype(BF16)
        for c in range(tm // CHUNK):
            rs = slice(c * CHUNK, (c + 1) * CHUNK)
            for hd in range(SGU_HEADS):
                cs = slice(hd * SGU_HEAD, (hd + 1) * SGU_HEAD)
                s = jnp.dot(sw_ref[hd], vb[rs, cs], preferred_element_type=F32) + sb_ref[hd]
                feats[2, rs, cs] = (u[rs, cs] * s).astype(BF16)

    h = h_ref[...]
    acc = None
    for k, (wg, bg, wb) in enumerate(((wg0_ref, bg0_ref, wb0_ref), (wg1_ref, bg1_ref, wb1_ref),
                                      (wg2_ref, bg2_ref, wb2_ref))):
        gate = _sigmoid(jnp.dot(h, wg[...], preferred_element_type=F32) + bg[...])
        br = jnp.dot(feats[k], wb[...], preferred_element_type=F32)
        acc = gate * br if acc is None else acc + gate * br
    m_ref[...] = acc.astype(BF16)


def _merge(h, zg, ys, yf, layer, wglu_bf, gv, sw_bf, sb_full, wgate_bf, bgate, wbranch_bf):
    n, d = h.shape
    tm, tn = MERGE_TILE, MERGE_COLS
    nj = d // tn
    row = lambda i, j: (i, 0)
    const2 = lambda i, j: (0, 0)
    const3 = lambda i, j: (0, 0, 0)
    gate_spec = lambda k: pl.BlockSpec((None, d, tn), lambda i, j: (layer, 0, k * nj + j))
    bias_spec = lambda k: pl.BlockSpec((1, tn), lambda i, j: (0, k * nj + j))
    br_spec = lambda k: pl.BlockSpec((None, None, S5_WIDTH, tn), lambda i, j: (layer, k, 0, j))
    return pl.pallas_call(
        _merge_kernel,
        grid=(n // tm, nj),
        in_specs=[pl.BlockSpec((tm, d), row),
                  pl.BlockSpec((tm, 2 * SGU_WIDTH), row),
                  pl.BlockSpec((S5_WIDTH // LANES, tm // S5_T, S5_T * LANES), lambda i, j: (0, i, 0)),
                  pl.BlockSpec((tm, FFT_WIDTH), row),
                  pl.BlockSpec((None, S5_WIDTH, S5_WIDTH), lambda i, j: (layer, 0, 0)),
                  pl.BlockSpec((1, SGU_WIDTH), const2),
                  pl.BlockSpec((None, SGU_HEADS, CHUNK, CHUNK), lambda i, j: (layer, 0, 0, 0)),
                  pl.BlockSpec((SGU_HEADS, CHUNK, SGU_HEAD), const3),
                  gate_spec(0), gate_spec(1), gate_spec(2),
                  bias_spec(0), bias_spec(1), bias_spec(2),
                  br_spec(0), br_spec(1), br_spec(2)],
        out_specs=pl.BlockSpec((tm, tn), lambda i, j: (i, j)),
        out_shape=jax.ShapeDtypeStruct((n, d), BF16),
        scratch_shapes=[pltpu.VMEM((N_BRANCH, tm, S5_WIDTH), BF16),
                        pltpu.VMEM((S5_WIDTH // LANES, tm, LANES), F32)],
        compiler_params=_cparams(("arbitrary", "arbitrary")),
        name="gated_merge",
    )(h, zg, ys, yf, wglu_bf, gv.reshape(1, SGU_WIDTH), sw_bf, sb_full,
      wgate_bf, wgate_bf, wgate_bf, bgate, bgate, bgate, wbranch_bf, wbranch_bf, wbranch_bf)


def _outproj_kernel(*refs, load_x):
    (m_ref, wo_ref, g1_ref, g2_ref, sh_ref, sc_ref, rw_ref, rb_ref,
     x1_ref, h2_ref, ti_ref, tw_ref) = refs[-12:]
    x1 = load_x(refs[:-12]) + g1_ref[...] * jnp.dot(m_ref[...], wo_ref[...], preferred_element_type=F32)
    x1_ref[...] = x1
    y = x1 * lax.rsqrt(jnp.mean(x1 * x1, axis=-1, keepdims=True) + EPS) * g2_ref[...]
    h2 = y * (1.0 + sc_ref[...]) + sh_ref[...]
    h2_ref[...] = h2
    h_hi = h2.astype(BF16)
    h_lo = (h2 - h_hi.astype(F32)).astype(BF16)
    p_hi = jnp.dot(h_hi, rw_ref[...], preferred_element_type=F32)
    p_lo = jnp.dot(h_lo, rw_ref[:, :LANES], preferred_element_type=F32)
    logits = p_hi[:, :LANES] + p_hi[:, LANES:] + p_lo + rb_ref[...]
    lane = lax.broadcasted_iota(I32, logits.shape, 1)
    vals, idxs = [], []
    for _ in range(TOP_K):
        mx = jnp.max(logits, axis=-1, keepdims=True)
        am = jnp.min(jnp.where(logits == mx, lane, LANES), axis=-1, keepdims=True)
        vals.append(mx)
        idxs.append(am)
        logits = jnp.where(lane == am, -jnp.inf, logits)
    ex = [jnp.exp(v - vals[0]) for v in vals]
    den = ex[0] + ex[1] + ex[2] + ex[3]
    ti = jnp.zeros(lane.shape, I32)
    tw = jnp.zeros(lane.shape, F32)
    for k in range(TOP_K):
        ti = jnp.where(lane == k, idxs[k], ti)
        tw = jnp.where(lane == k, ex[k] / den, tw)
    ti_ref[...] = ti
    tw_ref[...] = tw


def _outproj(x_lat, x_ctx, m, n_lat, layer, wo_bf, g1, g2n, sh2, sc2, rw_pad, rb_pad):
    n, d = m.shape
    tm = ROW_TILE
    lat_tiles = n_lat // tm
    row = lambda i: (i, 0)
    kind = lambda i: (jnp.where(i >= lat_tiles, 1, 0), 0, 0)
    const = lambda i: (0, 0)
    x_ops, x_specs, load_x = _token_rows(x_lat, x_ctx, tm)
    return pl.pallas_call(
        functools.partial(_outproj_kernel, load_x=load_x),
        grid=(n // tm,),
        in_specs=x_specs + [
                  pl.BlockSpec((tm, d), row),
                  pl.BlockSpec((None, d, d), lambda i: (layer, 0, 0)),
                  pl.BlockSpec((None, 1, d), kind),
                  pl.BlockSpec((1, d), const),
                  pl.BlockSpec((None, 1, d), kind),
                  pl.BlockSpec((None, 1, d), kind),
                  pl.BlockSpec((d, 2 * LANES), const),
                  pl.BlockSpec((1, LANES), const)],
        out_specs=[pl.BlockSpec((tm, d), row), pl.BlockSpec((tm, d), row),
                   pl.BlockSpec((tm, LANES), row), pl.BlockSpec((tm, LANES), row)],
        out_shape=[jax.ShapeDtypeStruct((n, d), F32), jax.ShapeDtypeStruct((n, d), F32),
                   jax.ShapeDtypeStruct((n, LANES), I32), jax.ShapeDtypeStruct((n, LANES), F32)],
        compiler_params=_cparams(("arbitrary",)),
        name="outproj_router",
    )(*x_ops, m, wo_bf, g1, g2n.reshape(1, d), sh2, sc2, rw_pad, rb_pad)


def _routing_tables(top_idx, n_items):
    flat_e = top_idx.reshape(-1)
    onehot = (flat_e[:, None] == jnp.arange(N_EXPERTS, dtype=I32)[None, :]).astype(I32)
    csum = jnp.cumsum(onehot, axis=0)
    rank = jnp.sum((csum - onehot) * onehot, axis=1)
    counts = csum[-1]
    padded = (counts + MOE_CH - 1) // MOE_CH * MOE_CH
    pend = jnp.cumsum(padded)
    pstart = pend - padded
    dest = jnp.sum(onehot * pstart[None, :], axis=1) + rank
    per_e = (padded + MOE_RMAX - 1) // MOE_RMAX
    iend = jnp.cumsum(per_e)
    istart = iend - per_e
    total = iend[-1]
    t = jnp.arange(n_items, dtype=I32)
    valid = t < total
    tc = jnp.minimum(t, total - 1)
    e_of = jnp.minimum(jnp.sum((tc[:, None] >= iend[None, :]).astype(I32), axis=1), N_EXPERTS - 1)
    local = tc - istart[e_of]
    row0 = pstart[e_of] + local * MOE_RMAX
    nch = jnp.clip((padded[e_of] - local * MOE_RMAX) // MOE_CH, 0, MOE_RMAX // MOE_CH)
    nch = jnp.where(valid, nch, 0)
    return (dest.astype(I32), counts.astype(I32), pstart.astype(I32), padded.astype(I32),
            e_of.astype(I32), row0.astype(I32), nch.astype(I32))


def _dispatch_kernel(cnt_ref, pst_ref, pad_ref, dest_ref, h_ref, xs_ref, zrow, sem, zsem):
    i = pl.program_id(0)
    tq = h_ref.shape[0]

    @pl.when(i == 0)
    def _():
        zrow[...] = jnp.zeros(zrow.shape, zrow.dtype)

        def per_expert(e, carry):
            first = pst_ref[e] + cnt_ref[e]
            npad = pad_ref[e] - cnt_ref[e]

            def start(r, c):
                pltpu.make_async_copy(zrow.at[pl.ds(0, 1)], xs_ref.at[pl.ds(first + r, 1)], zsem).start()
                return c

            def wait(r, c):
                pltpu.make_async_copy(zrow.at[pl.ds(0, 1)], xs_ref.at[pl.ds(first + r, 1)], zsem).wait()
                return c

            lax.fori_loop(0, npad, start, 0)
            lax.fori_loop(0, npad, wait, 0)
            return carry

        lax.fori_loop(0, N_EXPERTS, per_expert, 0)

    def start(t, c):
        src = h_ref.at[pl.ds(t, 1)]
        for k in range(TOP_K):
            pltpu.make_async_copy(src, xs_ref.at[pl.ds(dest_ref[0, t * TOP_K + k], 1)], sem).start()
        return c

    lax.fori_loop(0, tq, start, 0, unroll=2)
    for _ in range(TOP_K):
        pltpu.make_async_copy(h_ref, xs_ref.at[pl.ds(0, tq)], sem).wait()


def _dispatch(h2, dest, counts, pstart, padded, n_rows):
    d = h2.shape[1]
    tq = DISP_TILE
    nt = dest.shape[0] // (tq * TOP_K)
    dest3 = dest.reshape(nt, 1, tq * TOP_K)
    return pl.pallas_call(
        _dispatch_kernel,
        grid_spec=pltpu.PrefetchScalarGridSpec(
            num_scalar_prefetch=3,
            grid=(nt,),
            in_specs=[pl.BlockSpec((None, 1, tq * TOP_K), lambda i, *_: (i, 0, 0), memory_space=pltpu.SMEM),
                      pl.BlockSpec((tq, d), lambda i, *_: (i, 0))],
            out_specs=pl.BlockSpec(memory_space=pl.ANY),
            scratch_shapes=[pltpu.VMEM((8, d), F32), pltpu.SemaphoreType.DMA(()), pltpu.SemaphoreType.DMA(())]),
        out_shape=jax.ShapeDtypeStruct((n_rows, d), F32),
        compiler_params=_cparams(("arbitrary",)),
        name="moe_dispatch",
    )(counts, pstart, padded, dest3, h2)


def _experts_kernel(ie_ref, ir_ref, in_ref, xs_ref, wg_ref, wu_ref, wd_ref, bg_ref, bu_ref,
                    ys_ref, xstage, xb, acc, pending, sem_in, sem_out, *, units_per_step):
    t = pl.program_id(0)
    j = pl.program_id(1)
    nf = pl.num_programs(1)
    nch = in_ref[t]
    row0 = ir_ref[t]
    ch = MOE_CH

    def hbm_rows(c):
        return pl.ds(pl.multiple_of(row0 + c * ch, ch), ch)

    def vmem_rows(c, size=ch):
        return pl.ds(pl.multiple_of(c * ch, ch), size)

    def in_copy(c, slot):
        return pltpu.make_async_copy(xs_ref.at[hbm_rows(c)], xstage.at[slot], sem_in.at[slot])

    def out_copy(c):
        return pltpu.make_async_copy(acc.at[vmem_rows(c)], ys_ref.at[hbm_rows(c)], sem_out)

    def drain():
        def body(c, carry):
            out_copy(0).wait()
            return carry
        lax.fori_loop(0, pending[0], body, 0)
        pending[0] = 0

    cur = t & 1
    n_items = pl.num_programs(0)
    t_next = jnp.minimum(t + 1, n_items - 1)
    nch_next = jnp.where(t + 1 < n_items, in_ref[t_next], 0)
    row0_next = ir_ref[t_next]

    def next_copy(u, k):
        src = xs_ref.at[pl.ds(pl.multiple_of(row0_next + u * ch, ch), ch)]
        return pltpu.make_async_copy(src, xstage.at[k], sem_in.at[k])

    @pl.when(jnp.logical_and(t == 0, j == 0))
    def _():
        pending[0] = 0

    @pl.when(jnp.logical_and(jnp.logical_and(t == 0, j == 0), nch > 0))
    def _():
        in_copy(0, 0).start()

        def body(c, carry):
            slot = c & 1

            @pl.when(c + 1 < nch)
            def _():
                in_copy(c + 1, 1 - slot).start()

            in_copy(c, slot).wait()
            xb[0, vmem_rows(c), :] = xstage[slot].astype(BF16)
            return carry

        lax.fori_loop(0, nch, body, 0)

    for k in range(units_per_step):
        @pl.when(j * units_per_step + k < nch_next)
        def _():
            next_copy(j * units_per_step + k, k).start()

    @pl.when(nch > 0)
    def _():
        def partial_out(c, size):
            rows = vmem_rows(c, size)
            x = xb[cur, rows, :]
            gate = jnp.dot(x, wg_ref[...].astype(BF16), preferred_element_type=F32) + bg_ref[...]
            up = jnp.dot(x, wu_ref[...].astype(BF16), preferred_element_type=F32) + bu_ref[...]
            gate = jnp.minimum(gate, SWIGLU_LIMIT)
            up = jnp.clip(up, -SWIGLU_LIMIT, SWIGLU_LIMIT)
            act = gate * _sigmoid(SWIGLU_ALPHA * gate) * (up + 1.0)
            return rows, jnp.dot(act.astype(BF16), wd_ref[...].astype(BF16), preferred_element_type=F32)

        def accumulate(first):
            def block(c, units):
                rows, part = partial_out(c, units * ch)
                if first:
                    acc[rows, :] = part
                else:
                    acc[rows, :] += part

                    @pl.when(j == nf - 1)
                    def _():
                        for u in range(units):
                            out_copy(c + u).start()

            def body(c4, carry):
                block(4 * c4, 4)
                return carry

            quads = lax.shift_right_logical(nch, 2)
            five = jnp.logical_and((nch & 3) == 1, quads >= 1)
            full = quads - five.astype(I32)
            lax.fori_loop(0, full, body, 0)

            @pl.when(five)
            def _():
                block(4 * full, 5)

            @pl.when((nch & 2) != 0)
            def _():
                block(4 * quads, 2)

            @pl.when(jnp.logical_and((nch & 1) != 0, jnp.logical_not(five)))
            def _():
                block(nch - 1, 1)

        @pl.when(j == 0)
        def _():
            drain()
            accumulate(True)

        @pl.when(j > 0)
        def _():
            accumulate(False)

        @pl.when(j == nf - 1)
        def _():
            pending[0] = nch

    for k in range(units_per_step):
        @pl.when(j * units_per_step + k < nch_next)
        def _():
            next_copy(j * units_per_step + k, k).wait()
            xb[1 - cur, vmem_rows(j * units_per_step + k), :] = xstage[k].astype(BF16)

    @pl.when(jnp.logical_and(t == n_items - 1, j == nf - 1))
    def _():
        drain()


def _experts(xs, layer, w_up, b_up, w_down, item_e, item_row0, item_nch):
    n_rows, d = xs.shape
    n_layers, n_exp, _, two_f = w_up.shape
    f = two_f // 2
    tf = MOE_TF
    nf = f // tf
    assert nf >= 2
    units_per_step = max(2, -(-(MOE_RMAX // MOE_CH) // nf))
    n_items = item_e.shape[0]
    b_up4 = b_up.reshape(n_layers, n_exp, 1, two_f)

    def jj(t, j, ie, ir, inn):
        return jnp.where(inn[t] > 0, j, nf - 1)

    return pl.pallas_call(
        functools.partial(_experts_kernel, units_per_step=units_per_step),
        grid_spec=pltpu.PrefetchScalarGridSpec(
            num_scalar_prefetch=3,
            grid=(n_items, nf),
            in_specs=[pl.BlockSpec(memory_space=pl.ANY),
                      pl.BlockSpec((None, None, d, tf), lambda t, j, ie, ir, inn: (layer, ie[t], 0, jj(t, j, ie, ir, inn))),
                      pl.BlockSpec((None, None, d, tf), lambda t, j, ie, ir, inn: (layer, ie[t], 0, nf + jj(t, j, ie, ir, inn))),
                      pl.BlockSpec((None, None, tf, d), lambda t, j, ie, ir, inn: (layer, ie[t], jj(t, j, ie, ir, inn), 0)),
                      pl.BlockSpec((None, None, 1, tf), lambda t, j, ie, ir, inn: (layer, ie[t], 0, jj(t, j, ie, ir, inn))),
                      pl.BlockSpec((None, None, 1, tf), lambda t, j, ie, ir, inn: (layer, ie[t], 0, nf + jj(t, j, ie, ir, inn)))],
            out_specs=pl.BlockSpec(memory_space=pl.ANY),
            scratch_shapes=[pltpu.VMEM((units_per_step, MOE_CH, d), F32),
                            pltpu.VMEM((2, MOE_RMAX, d), BF16),
                            pltpu.VMEM((MOE_RMAX, d), F32),
                            pltpu.SMEM((1,), I32),
                            pltpu.SemaphoreType.DMA((units_per_step,)),
                            pltpu.SemaphoreType.DMA(())]),
        out_shape=jax.ShapeDtypeStruct((n_rows, d), F32),
        compiler_params=_cparams(("arbitrary", "arbitrary")),
        name="moe_experts",
    )(item_e, item_row0, item_nch, xs, w_up, w_up, w_down, b_up4, b_up4)


def _combine_kernel(dc_ref, dn_ref, tw_ref, ti_ref, x1_ref, g2_ref, fg_ref, bd_ref, ys_ref, o_ref,
                    gbuf, sem, *, final):
    i = pl.program_id(0)
    n = pl.num_programs(0)
    tq = x1_ref.shape[0]

    def issue(dref, slot):
        def body(t, c):
            for k in range(TOP_K):
                pltpu.make_async_copy(ys_ref.at[pl.ds(dref[0, t * TOP_K + k], 1)],
                                      gbuf.at[slot, k, pl.ds(t, 1)], sem.at[slot]).start()
            return c
        lax.fori_loop(0, tq, body, 0, unroll=2)

    slot = i & 1

    @pl.when(i == 0)
    def _():
        issue(dc_ref, 0)

    @pl.when(i + 1 < n)
    def _():
        issue(dn_ref, 1 - slot)

    for k in range(TOP_K):
        pltpu.make_async_copy(ys_ref.at[pl.ds(0, tq)], gbuf.at[slot, k], sem.at[slot]).wait()
    tw = tw_ref[...]
    ti = ti_ref[...]
    lane = lax.broadcasted_iota(I32, tw.shape, 1)
    ew = jnp.zeros(tw.shape, F32)
    for k in range(TOP_K):
        ew = ew + jnp.where(lane == ti[:, k:k + 1], tw[:, k:k + 1], 0.0)
    y = jnp.dot(ew.astype(BF16), bd_ref[...], preferred_element_type=F32)
    for k in range(TOP_K):
        y = y + tw[:, k:k + 1] * gbuf[slot, k]
    x2 = x1_ref[...] + g2_ref[...] * y
    if final:
        x2 = x2 * lax.rsqrt(jnp.mean(x2 * x2, axis=-1, keepdims=True) + EPS) * fg_ref[...]
    o_ref[...] = x2


def _combine(ys, dest, top_w, top_idx, b_down, x1, n_lat, g2, final_g, final):
    n, d = x1.shape
    bd_pad = jnp.zeros((LANES, d), BF16).at[:N_EXPERTS].set(b_down.astype(BF16))
    tq = DISP_TILE
    lat_tiles = n_lat // tq
    dest3 = dest.reshape(-1, 1, tq * TOP_K)
    if final:
        n = n_lat
    nt = n // tq
    kind = lambda i: (jnp.where(i >= lat_tiles, 1, 0), 0, 0)
    return pl.pallas_call(
        functools.partial(_combine_kernel, final=final),
        grid=(nt,),
        in_specs=[pl.BlockSpec((None, 1, tq * TOP_K), lambda i: (i, 0, 0), memory_space=pltpu.SMEM),
                  pl.BlockSpec((None, 1, tq * TOP_K), lambda i: (jnp.minimum(i + 1, nt - 1), 0, 0),
                               memory_space=pltpu.SMEM),
                  pl.BlockSpec((tq, LANES), lambda i: (i, 0)),
                  pl.BlockSpec((tq, LANES), lambda i: (i, 0)),
                  pl.BlockSpec((tq, d), lambda i: (i, 0)),
                  pl.BlockSpec((None, 1, d), kind),
                  pl.BlockSpec((1, d), lambda i: (0, 0)),
                  pl.BlockSpec((LANES, d), lambda i: (0, 0)),
                  pl.BlockSpec(memory_space=pl.ANY)],
        out_specs=pl.BlockSpec((tq, d), lambda i: (i, 0)),
        out_shape=jax.ShapeDtypeStruct((n, d), F32),
        scratch_shapes=[pltpu.VMEM((2, TOP_K, tq, d), F32), pltpu.SemaphoreType.DMA((2,))],
        compiler_params=_cparams(("arbitrary",)),
        name="moe_combine",
    )(dest3, dest3, top_w, top_idx, x1, g2, final_g.reshape(1, d), bd_pad, ys)


def kernel(x, c, ctx, c_ctx, w_mod, b_mod, norm1_g, norm2_g, w_in, s5_a_re, s5_a_im, s5_log_dt,
           s5_b_re, s5_b_im, s5_c_re, s5_c_im, s5_d, s5_w_glu, sgu_norm_g, sgu_w, sgu_b, w_branch,
           w_gate, b_gate, w_out, router_w, router_b, moe_w_up, moe_b_up, moe_w_down, moe_b_down,
           final_g):
    bsz, n_lat, d = x.shape
    n_ctx = ctx.shape[1]
    assert bsz == 1
    n_layers = w_mod.shape[0]
    n = n_lat + n_ctx
    x_lat, x_ctx = x.reshape(n_lat, d).astype(F32), ctx.reshape(n_ctx, d).astype(F32)
    w_in_bf, w_out_bf, w_gate_bf, w_branch_bf = (t.astype(BF16) for t in (w_in, w_out, w_gate, w_branch))
    w_glu_bf, sgu_w_bf = s5_w_glu.astype(BF16), sgu_w.astype(BF16)

    cond8 = jnp.zeros((8, d), F32).at[0].set(c[0]).at[1].set(c_ctx)
    mods = _adaln(cond8, w_mod, b_mod)[:, :2, :].reshape(n_layers, 2, N_MOD, 1, d)

    n_chunks = n // S5_T
    n_steps = max(1, (n_chunks - 1).bit_length())
    dft_tables = _dft_tables(n_lat)
    cc, sc = _channel_tables()
    sel, selt = _s5_select_tables(S5_T)
    n_assign = n * TOP_K
    n_rows = n_assign + N_EXPERTS * MOE_CH
    n_items = -(-n_assign // MOE_RMAX) + N_EXPERTS

    for i in range(n_layers):
        last = i == n_layers - 1
        sh1, sc1, g1, sh2, sc2, g2 = [mods[i, :, k] for k in range(N_MOD)]

        h, z_s5, z_fft, z_sgu = _inproj(x_lat, x_ctx, n, n_lat, i, norm1_g[i], sh1, sc1, w_in_bf)
        mtz, wop, vop, pq = _s5_operators(s5_a_re[i], s5_a_im[i], s5_log_dt[i], s5_b_re[i], s5_b_im[i],
                                     s5_c_re[i], s5_c_im[i], s5_d[i], S5_T, n_steps)
        y_s5 = _s5_mix(z_s5, n_lat, sel, selt, mtz, wop, vop, pq, S5_T, n_steps)
        y_fft = _fourier_ctx(z_fft, _fourier_latent(z_fft, n_lat, dft_tables, cc, sc), n_lat, cc, sc)
        sb_full = jnp.broadcast_to(sgu_b[i].astype(F32)[:, :, None], (SGU_HEADS, CHUNK, SGU_HEAD))
        m = _merge(h, z_sgu, y_s5, y_fft, i, w_glu_bf, sgu_norm_g[i].astype(F32), sgu_w_bf, sb_full,
                   w_gate_bf, b_gate[i].astype(F32).reshape(1, N_BRANCH * d), w_branch_bf)

        rw32 = jnp.zeros((d, LANES), F32).at[:, :N_EXPERTS].set(router_w[i].astype(F32))
        rw_hi = rw32.astype(BF16)
        rw_pad = jnp.concatenate([rw_hi, (rw32 - rw_hi.astype(F32)).astype(BF16)], axis=1)
        rb_pad = jnp.full((1, LANES), -1e30, F32).at[0, :N_EXPERTS].set(router_b[i].astype(F32))
        x1, h2, top_idx, top_w = _outproj(x_lat, x_ctx, m, n_lat, i, w_out_bf, g1, norm2_g[i].astype(F32),
                                          sh2, sc2, rw_pad, rb_pad)
        n_routed = n_lat if last else n
        dest, counts, pstart, padded, item_e, item_row0, item_nch = _routing_tables(
            top_idx[:n_routed, :TOP_K], n_items)
        xs = _dispatch(h2, dest, counts, pstart, padded, n_rows)
        ys = _experts(xs, i, moe_w_up, moe_b_up, moe_w_down, item_e, item_row0, item_nch)
        x_lat = _combine(ys, dest, top_w, top_idx, moe_b_down[i], x1, n_lat, g2, final_g.astype(F32), last)
        x_ctx = None

    return x_lat.reshape(bsz, n_lat, d).astype(x.dtype)
```
